```python
import math
import jax, jax.numpy as jnp
from jax import lax
import numpy as np

D_MODEL = 2048
BATCH = 2
SEQ = 4096
DEPTH = 1

N_HEADS = 16
HEAD_DIM = 128
ATTN_WIDTH = N_HEADS * HEAD_DIM
ROPE_DIM = HEAD_DIM // 4
ROPE_THETA = 500000.0
MOBA_BLOCK = 256
MOBA_TOPK = 3
Q_CHUNK = 128
CONV_WIDTH = D_MODEL
CONV_GROUPS = 16
CONV_K = 3
N_EXPERTS = 32
TOP_K = 4
D_FF = D_MODEL
SWIGLU_LIMIT = 7.0
SWIGLU_ALPHA = 1.702
EXPERT_BLOCK = 128
NORM_EPS = 1e-5
NEG = -1e30
IN_COLS = 3 * CONV_WIDTH + 3 * ATTN_WIDTH + 2 * D_MODEL

kernel_name = 'hybrid_shortconv_moba_moe_block'


def rms_norm(x, g):
    xf = x.astype(jnp.float32)
    y = xf * lax.rsqrt(jnp.mean(xf * xf, axis=-1, keepdims=True) + NORM_EPS)
    return (y * g.astype(jnp.float32)).astype(x.dtype)


def partial_rope(t, pos):
    half = ROPE_DIM // 2
    inv_freq = jnp.exp(-math.log(ROPE_THETA) * jnp.arange(half, dtype=jnp.float32) / half)
    ang = pos.astype(jnp.float32)[:, None] * inv_freq[None, :]
    cos = jnp.cos(ang)[None, :, None, :]
    sin = jnp.sin(ang)[None, :, None, :]
    r1 = t[..., :half].astype(jnp.float32)
    r2 = t[..., half:ROPE_DIM].astype(jnp.float32)
    rot = jnp.concatenate([r1 * cos - r2 * sin, r2 * cos + r1 * sin], axis=-1).astype(t.dtype)
    return jnp.concatenate([rot, t[..., ROPE_DIM:]], axis=-1)


def short_conv_mixer(b_gate, c_gate, h, conv_w):
    u = c_gate * h
    y = lax.conv_general_dilated(
        u, conv_w[:, None, :].astype(u.dtype), window_strides=(1,),
        padding=[(CONV_K - 1, 0)], dimension_numbers=('NWC', 'WIO', 'NWC'),
        feature_group_count=CONV_WIDTH)
    return b_gate * y


def moba_attention(q, k, v):
    B, S, H, Dh = q.shape
    n_kb = -(-S // MOBA_BLOCK)
    pad = n_kb * MOBA_BLOCK - S
    k_sel_n = min(MOBA_TOPK, n_kb)
    n_qc = S // Q_CHUNK
    scale = Dh ** -0.5

    def to_bh(t):
        return t.transpose(0, 2, 1, 3).reshape(B * H, S, Dh)

    qf, kf, vf = to_bh(q), to_bh(k), to_bh(v)
    kb = jnp.pad(kf, ((0, 0), (0, pad), (0, 0))).reshape(B * H, n_kb, MOBA_BLOCK, Dh)
    vb = jnp.pad(vf, ((0, 0), (0, pad), (0, 0))).reshape(B * H, n_kb, MOBA_BLOCK, Dh)
    k_mean = jnp.mean(kb.astype(jnp.float32), axis=2)
    blk_ids = jnp.arange(n_kb)
    key_off = jnp.arange(MOBA_BLOCK)

    def one_chunk(idx):
        bh = idx // n_qc
        q0 = (idx % n_qc) * Q_CHUNK
        qc = lax.dynamic_slice(qf, (bh, q0, 0), (1, Q_CHUNK, Dh))[0].astype(jnp.float32)
        qpos = q0 + jnp.arange(Q_CHUNK)
        own = q0 // MOBA_BLOCK
        kb_bh, vb_bh = kb[bh], vb[bh]
        gate = qc @ k_mean[bh].T
        gate = jnp.where(blk_ids[None, :] < own, gate, NEG)
        _, sel = lax.top_k(gate, k_sel_n)
        sel_ok = sel < own
        k_sel = kb_bh[sel].astype(jnp.float32)
        v_sel = vb_bh[sel].astype(jnp.float32)
        s_sel = jnp.einsum('qd,qjkd->qjk', qc, k_sel) * scale
        s_sel = jnp.where(sel_ok[:, :, None], s_sel, NEG).reshape(Q_CHUNK, k_sel_n * MOBA_BLOCK)
        k_own = kb_bh[own].astype(jnp.float32)
        v_own = vb_bh[own].astype(jnp.float32)
        s_own = (qc @ k_own.T) * scale
        kpos = own * MOBA_BLOCK + key_off
        s_own = jnp.where(kpos[None, :] <= qpos[:, None], s_own, NEG)
        p = jax.nn.softmax(jnp.concatenate([s_sel, s_own], axis=-1), axis=-1)
        p_sel = p[:, :k_sel_n * MOBA_BLOCK].reshape(Q_CHUNK, k_sel_n, MOBA_BLOCK)
        p_own = p[:, k_sel_n * MOBA_BLOCK:]
        o = jnp.einsum('qjk,qjkd->qd', p_sel, v_sel) + p_own @ v_own
        return o.astype(q.dtype)

    out = lax.map(one_chunk, jnp.arange(B * H * n_qc))
    return out.reshape(B, H, S, Dh).transpose(0, 2, 1, 3)


def hybrid_mixer(x, w_in, conv_w, w_conv_out, w_attn_out, w_o, pos):
    B, S, _ = x.shape
    proj = x @ w_in
    cw, aw = CONV_WIDTH, ATTN_WIDTH
    splits = [cw, 2 * cw, 3 * cw, 3 * cw + aw, 3 * cw + 2 * aw, 3 * cw + 3 * aw, 3 * cw + 3 * aw + D_MODEL]
    b_g, c_g, h_c, q, k, v, g_conv, g_attn = jnp.split(proj, splits, axis=-1)
    y_conv = short_conv_mixer(b_g, c_g, h_c, conv_w) @ w_conv_out
    q = partial_rope(q.reshape(B, S, N_HEADS, HEAD_DIM), pos)
    k = partial_rope(k.reshape(B, S, N_HEADS, HEAD_DIM), pos)
    v = v.reshape(B, S, N_HEADS, HEAD_DIM)
    y_attn = moba_attention(q, k, v).reshape(B, S, ATTN_WIDTH) @ w_attn_out
    merged = jax.nn.sigmoid(g_conv) * y_conv + jax.nn.sigmoid(g_attn) * y_attn
    return merged @ w_o


def moe_ffn(x, w_router, b_router, w_gate, b_gate, w_up, b_up, w_down, b_down):
    B, S, D = x.shape
    T = B * S
    xt = x.reshape(T, D)
    logits = xt.astype(jnp.float32) @ w_router.astype(jnp.float32) + b_router.astype(jnp.float32)
    top_logit, top_e = lax.top_k(logits, TOP_K)
    top_w = jax.nn.softmax(top_logit, axis=-1)
    A = T * TOP_K
    flat_e = top_e.reshape(A)
    flat_tok = jnp.repeat(jnp.arange(T, dtype=jnp.int32), TOP_K)
    flat_w = top_w.reshape(A)
    order = jnp.argsort(flat_e)
    se, stok, sw = flat_e[order], flat_tok[order], flat_w[order]
    counts = jnp.bincount(flat_e, length=N_EXPERTS)
    starts = jnp.cumsum(counts) - counts
    padded = (counts + EXPERT_BLOCK - 1) // EXPERT_BLOCK * EXPERT_BLOCK
    pad_end = jnp.cumsum(padded)
    pad_start = pad_end - padded
    dest = pad_start[se] + (jnp.arange(A) - starts[se])
    n_blocks = (A + N_EXPERTS * (EXPERT_BLOCK - 1)) // EXPERT_BLOCK + 1
    rows = n_blocks * EXPERT_BLOCK
    row_tok = jnp.zeros((rows,), jnp.int32).at[dest].set(stok)
    row_w = jnp.zeros((rows,), jnp.float32).at[dest].set(sw)
    block_e = jnp.minimum(
        jnp.searchsorted(pad_end, jnp.arange(n_blocks) * EXPERT_BLOCK, side='right'), N_EXPERTS - 1)
    xb = xt[row_tok].reshape(n_blocks, EXPERT_BLOCK, D)

    def expert_block(args):
        e, xe = args
        g = xe @ w_gate[e] + b_gate[e]
        u = xe @ w_up[e] + b_up[e]
        g = jnp.minimum(g, SWIGLU_LIMIT)
        u = jnp.clip(u, -SWIGLU_LIMIT, SWIGLU_LIMIT)
        hid = (u + 1.0) * (g * jax.nn.sigmoid(SWIGLU_ALPHA * g))
        return hid @ w_down[e] + b_down[e]

    yb = lax.map(expert_block, (block_e, xb))
    y = jax.ops.segment_sum(yb.reshape(rows, D).astype(jnp.float32) * row_w[:, None],
                            row_tok, num_segments=T)
    return y.astype(x.dtype).reshape(B, S, D)


def setup_inputs(seed: int = 0) -> dict:
    key = jax.random.key(seed)
    ks = jax.random.split(key, 18)
    f32 = jnp.float32
    L, D, E, F = DEPTH, D_MODEL, N_EXPERTS, D_FF

    def nrm(k, shape, scale):
        return jax.random.normal(k, shape, f32) * scale

    return {
        'x': nrm(ks[0], (BATCH, SEQ, D), 1.0),
        'norm_mix': 1.0 + nrm(ks[1], (L, D), 0.02),
        'w_in': nrm(ks[2], (L, D, IN_COLS), D ** -0.5),
        'conv_w': nrm(ks[3], (L, CONV_K, CONV_WIDTH), CONV_K ** -0.5),
        'w_conv_out': nrm(ks[4], (L, CONV_WIDTH, D), CONV_WIDTH ** -0.5),
        'w_attn_out': nrm(ks[5], (L, ATTN_WIDTH, D), ATTN_WIDTH ** -0.5),
        'w_o': nrm(ks[6], (L, D, D), D ** -0.5),
        'norm_ffn': 1.0 + nrm(ks[7], (L, D), 0.02),
        'w_router': nrm(ks[8], (L, D, E), D ** -0.5),
        'b_router': nrm(ks[9], (L, E), 0.01),
        'w_gate': nrm(ks[10], (L, E, D, F), D ** -0.5),
        'b_gate': nrm(ks[11], (L, E, F), 0.01),
        'w_up': nrm(ks[12], (L, E, D, F), D ** -0.5),
        'b_up': nrm(ks[13], (L, E, F), 0.01),
        'w_down': nrm(ks[14], (L, E, F, D), F ** -0.5),
        'b_down': nrm(ks[15], (L, E, D), 0.01),
        'norm_final': 1.0 + nrm(ks[16], (D,), 0.02),
    }


def reference(x, norm_mix, w_in, conv_w, w_conv_out, w_attn_out, w_o, norm_ffn,
              w_router, b_router, w_gate, b_gate, w_up, b_up, w_down, b_down, norm_final):
    pos = jnp.arange(x.shape[1], dtype=jnp.int32)
    h = x
    for l in range(DEPTH):
        h = h + hybrid_mixer(rms_norm(h, norm_mix[l]), w_in[l], conv_w[l], w_conv_out[l],
                             w_attn_out[l], w_o[l], pos)
        h = h + moe_ffn(rms_norm(h, norm_ffn[l]), w_router[l], b_router[l], w_gate[l], b_gate[l],
                        w_up[l], b_up[l], w_down[l], b_down[l])
    return rms_norm(h, norm_final)
```

```python
import functools
import math

import jax
import jax.numpy as jnp
from jax import lax
from jax.experimental import pallas as pl
from jax.experimental.pallas import tpu as pltpu

HEAD_DIM = 128
ROPE_DIM = HEAD_DIM // 4
ROPE_THETA = 500000.0
MOBA_BLOCK = 256
MOBA_TOPK = 3
CONV_K = 3
TOP_K = 4
SWIGLU_LIMIT = 7.0
SWIGLU_ALPHA = 1.702
NORM_EPS = 1e-5
NEG = -1e30

V7X_LANES = 128
V7X_BF16_SUBLANES = 16
V7X_VMEM_BYTES = 64 * 1024 * 1024
VMEM_LIMIT = V7X_VMEM_BYTES - 8 * 1024 * 1024

F32 = jnp.float32
BF16 = jnp.bfloat16
NT_DIMS = (((1,), (1,)), ((), ()))


def _params(*sem):
    return pltpu.CompilerParams(dimension_semantics=sem, vmem_limit_bytes=VMEM_LIMIT)


def _rmsnorm_kernel(x_ref, g_ref, o_ref):
    x = x_ref[...]
    ms = jnp.mean(x * x, axis=-1, keepdims=True)
    o_ref[...] = (x * lax.rsqrt(ms + NORM_EPS) * g_ref[...]).astype(o_ref.dtype)


def _rmsnorm(x, g, tm, out_dtype):
    t, d = x.shape
    return pl.pallas_call(
        _rmsnorm_kernel,
        grid=(t // tm,),
        in_specs=[pl.BlockSpec((tm, d), lambda i: (i, 0)), pl.BlockSpec((1, d), lambda i: (0, 0))],
        out_specs=pl.BlockSpec((tm, d), lambda i: (i, 0)),
        out_shape=jax.ShapeDtypeStruct((t, d), out_dtype),
        compiler_params=_params("arbitrary"),
        name="rmsnorm",
    )(x, g.reshape(1, d))


def _inproj_kernel(x_ref, w_ref, o_ref, wbf_ref):
    @pl.when(pl.program_id(1) == 0)
    def _():
        wbf_ref[...] = w_ref[...].astype(BF16)

    o_ref[...] = jnp.dot(x_ref[...], wbf_ref[...], preferred_element_type=F32).astype(o_ref.dtype)


def _inproj(xn, w, tm, tn):
    t, d = xn.shape
    n = w.shape[1]
    return pl.pallas_call(
        _inproj_kernel,
        grid=(n // tn, t // tm),
        in_specs=[pl.BlockSpec((tm, d), lambda j, i: (i, 0)), pl.BlockSpec((d, tn), lambda j, i: (0, j))],
        out_specs=pl.BlockSpec((tm, tn), lambda j, i: (i, j)),
        out_shape=jax.ShapeDtypeStruct((t, n), BF16),
        scratch_shapes=[pltpu.VMEM((d, tn), BF16)],
        compiler_params=_params("arbitrary", "arbitrary"),
        name="inproj",
    )(xn, w)


def _rope(t, c, sa, sb):
    half = ROPE_DIM // 2
    return t * c + pltpu.roll(t, HEAD_DIM - half, 1) * sa + pltpu.roll(t, half, 1) * sb


def _attn_kernel(q_ref, k_ref, v_ref, c_ref, sa_ref, sb_ref, o_ref,
                 kaug_ref, kmean_ref, m_ref, l_ref, acc_ref, *, n_kb, scale):
    blk = MOBA_BLOCK
    qi = pl.program_id(1)
    lane = lax.broadcasted_iota(jnp.int32, (blk, HEAD_DIM), 1)

    @pl.when(qi == 0)
    def _prep():
        kmean_ref[...] = jnp.zeros_like(kmean_ref)

        def body(j, carry):
            rows = pl.ds(pl.multiple_of(j * blk, blk), blk)
            kr = _rope(k_ref[rows, :].astype(F32), c_ref[rows, :], sa_ref[rows, :], sb_ref[rows, :])
            kaug_ref[rows, 0:HEAD_DIM] = kr.astype(BF16)
            kaug_ref[rows, HEAD_DIM:2 * HEAD_DIM] = jnp.where(lane == j, 1.0, 0.0).astype(BF16)
            kmean_ref[pl.ds(j, 1), :] = jnp.mean(kr, axis=0, keepdims=True)
            return carry

        lax.fori_loop(0, n_kb, body, 0)

    rows = pl.ds(pl.multiple_of(qi * blk, blk), blk)
    qr = _rope(q_ref[...].astype(F32), c_ref[rows, :], sa_ref[rows, :], sb_ref[rows, :])

    gate = lax.dot_general(qr, kmean_ref[...], NT_DIMS, precision=lax.Precision.HIGHEST,
                           preferred_element_type=F32)
    valid = lane < qi
    g = jnp.where(valid, gate, -jnp.inf)
    bias = jnp.full((blk, HEAD_DIM), NEG, F32)
    for _ in range(MOBA_TOPK):
        top = jnp.max(g, axis=-1, keepdims=True)
        idx = jnp.min(jnp.where(g == top, lane, HEAD_DIM), axis=-1, keepdims=True)
        pick = lane == idx
        bias = jnp.where(pick, jnp.where(valid, 0.0, bias), bias)
        g = jnp.where(pick, -jnp.inf, g)

    q_bf = qr.astype(BF16)
    q_aug = jnp.concatenate([q_bf, bias.astype(BF16)], axis=1)

    s = lax.dot_general(q_bf, kaug_ref[rows, 0:HEAD_DIM], NT_DIMS, preferred_element_type=F32) * scale
    r_i = lax.broadcasted_iota(jnp.int32, (blk, blk), 0)
    c_i = lax.broadcasted_iota(jnp.int32, (blk, blk), 1)
    s = jnp.where(c_i <= r_i, s, NEG)
    m0 = jnp.max(s, axis=-1, keepdims=True)
    p = jnp.exp(s - m0)
    m_ref[...] = m0
    l_ref[...] = jnp.sum(p, axis=-1, keepdims=True)
    acc_ref[...] = jnp.dot(p.astype(BF16), v_ref[rows, :], preferred_element_type=F32)

    def body(j, carry):
        kr = pl.ds(pl.multiple_of(j * blk, blk), blk)
        sj = lax.dot_general(q_aug, kaug_ref[kr, :], NT_DIMS, preferred_element_type=F32) * scale
        m_prev = m_ref[...]
        m_new = jnp.maximum(m_prev, jnp.max(sj, axis=-1, keepdims=True))
        alpha = jnp.exp(m_prev - m_new)
        pj = jnp.exp(sj - m_new)
        l_ref[...] = alpha * l_ref[...] + jnp.sum(pj, axis=-1, keepdims=True)
        acc_ref[...] = alpha * acc_ref[...] + jnp.dot(pj.astype(BF16), v_ref[kr, :],
                                                      preferred_element_type=F32)
        m_ref[...] = m_new
        return carry

    lax.fori_loop(0, qi, body, 0)
    o_ref[...] = (acc_ref[...] / l_ref[...]).astype(o_ref.dtype)


def _attention(proj, rope_c, rope_sa, rope_sb, batch, seq, n_heads, q_col, k_col, v_col):
    t = proj.shape[0]
    blk = MOBA_BLOCK
    n_kb = seq // blk
    kernel = functools.partial(_attn_kernel, n_kb=n_kb, scale=HEAD_DIM ** -0.5)
    tab = pl.BlockSpec((seq, HEAD_DIM), lambda bh, qi: (0, 0))
    return pl.pallas_call(
        kernel,
        grid=(batch * n_heads, n_kb),
        in_specs=[
            pl.BlockSpec((blk, HEAD_DIM), lambda bh, qi: ((bh // n_heads) * n_kb + qi, q_col + bh % n_heads)),
            pl.BlockSpec((seq, HEAD_DIM), lambda bh, qi: (bh // n_heads, k_col + bh % n_heads)),
            pl.BlockSpec((seq, HEAD_DIM), lambda bh, qi: (bh // n_heads, v_col + bh % n_heads)),
            tab, tab, tab,
        ],
        out_specs=pl.BlockSpec((blk, HEAD_DIM), lambda bh, qi: ((bh // n_heads) * n_kb + qi, bh % n_heads)),
        out_shape=jax.ShapeDtypeStruct((t, n_heads * HEAD_DIM), BF16),
        scratch_shapes=[
            pltpu.VMEM((seq, 2 * HEAD_DIM), BF16),
            pltpu.VMEM((HEAD_DIM, HEAD_DIM), F32),
            pltpu.VMEM((blk, 1), F32),
            pltpu.VMEM((blk, 1), F32),
            pltpu.VMEM((blk, HEAD_DIM), F32),
        ],
        compiler_params=_params("arbitrary", "arbitrary"),
        name="moba_attention",
    )(proj, proj, proj, rope_c, rope_sa, rope_sb)


def _merge_kernel(b_ref, c_ref, h_ref, cp_ref, hp_ref, gc_ref, ga_ref, at_ref, cw_ref, wc_ref, wa_ref,
                  o_ref, *, tiles_per_seq):
    hal = V7X_BF16_SUBLANES
    u = c_ref[...].astype(F32) * h_ref[...].astype(F32)
    up = cp_ref[...].astype(F32) * hp_ref[...].astype(F32)
    up = jnp.where(pl.program_id(0) % tiles_per_seq == 0, 0.0, up)
    row = lax.broadcasted_iota(jnp.int32, u.shape, 0)
    u1 = jnp.where(row == 0, up[hal - 1:hal, :], pltpu.roll(u, 1, 0))
    u2 = jnp.where(row == 0, up[hal - 2:hal - 1, :],
                   jnp.where(row == 1, up[hal - 1:hal, :], pltpu.roll(u, 2, 0)))
    y = cw_ref[0:1, :] * u2 + cw_ref[1:2, :] * u1 + cw_ref[2:3, :] * u
    cm = (b_ref[...].astype(F32) * y).astype(BF16)
    yc = jnp.dot(cm, wc_ref[...], preferred_element_type=F32)
    ya = jnp.dot(at_ref[...], wa_ref[...], preferred_element_type=F32)
    merged = jax.nn.sigmoid(gc_ref[...].astype(F32)) * yc + jax.nn.sigmoid(ga_ref[...].astype(F32)) * ya
    o_ref[...] = merged.astype(o_ref.dtype)


def _merge(proj, attn, conv_w, wc_bf, wa_bf, seq, cw, aw, d, tm):
    t = proj.shape[0]
    hal = V7X_BF16_SUBLANES
    kernel = functools.partial(_merge_kernel, tiles_per_seq=seq // tm)
    col = lambda cidx: pl.BlockSpec((tm, cw), lambda i: (i, cidx))
    halo = lambda cidx: pl.BlockSpec((hal, cw), lambda i: (jnp.maximum(i * (tm // hal) - 1, 0), cidx))
    const = lambda shape: pl.BlockSpec(shape, lambda i: (0, 0))
    return pl.pallas_call(
        kernel,
        grid=(t // tm,),
        in_specs=[col(0), col(1), col(2), halo(1), halo(2), col(6), col(7),
                  pl.BlockSpec((tm, aw), lambda i: (i, 0)),
                  const((CONV_K, cw)), const((cw, d)), const((aw, d))],
        out_specs=pl.BlockSpec((tm, d), lambda i: (i, 0)),
        out_shape=jax.ShapeDtypeStruct((t, d), BF16),
        compiler_params=_params("arbitrary"),
        name="merge_branches",
    )(proj, proj, proj, proj, proj, proj, proj, attn, conv_w, wc_bf, wa_bf)


def _oproj_kernel(mg_ref, x_ref, wo_ref, g_ref, wrt_ref, br_ref, h1_ref, xn_ref, te_ref, tw_ref):
    h1 = x_ref[...] + jnp.dot(mg_ref[...], wo_ref[...], preferred_element_type=F32)
    h1_ref[...] = h1
    ms = jnp.mean(h1 * h1, axis=-1, keepdims=True)
    xn = h1 * lax.rsqrt(ms + NORM_EPS) * g_ref[...]
    xn_ref[...] = xn
    lg = lax.dot_general(wrt_ref[...], xn, NT_DIMS, precision=lax.Precision.HIGHEST,
                         preferred_element_type=F32) + br_ref[...]
    n_e = lg.shape[0]
    sub = lax.broadcasted_iota(jnp.int32, lg.shape, 0)
    vals, idxs = [], []
    for _ in range(TOP_K):
        top = jnp.max(lg, axis=0, keepdims=True)
        idx = jnp.min(jnp.where(lg == top, sub, n_e), axis=0, keepdims=True)
        vals.append(top)
        idxs.append(idx)
        lg = jnp.where(sub == idx, -jnp.inf, lg)
    v = jnp.concatenate(vals, axis=0)
    ex = jnp.exp(v - vals[0])
    te_ref[...] = jnp.concatenate(idxs, axis=0)
    tw_ref[...] = ex / jnp.sum(ex, axis=0, keepdims=True)


def _oproj(merged, x2, wo_bf, g, wr_t, br, tm):
    t, d = x2.shape
    n_e = wr_t.shape[0]
    const = lambda shape: pl.BlockSpec(shape, lambda i: (0, 0))
    row = pl.BlockSpec((tm, d), lambda i: (i, 0))
    top = pl.BlockSpec((TOP_K, tm), lambda i: (0, i))
    return pl.pallas_call(
        _oproj_kernel,
        grid=(t // tm,),
        in_specs=[row, row, const((d, d)), const((1, d)), const((n_e, d)), const((n_e, 1))],
        out_specs=[row, row, top, top],
        out_shape=[jax.ShapeDtypeStruct((t, d), F32), jax.ShapeDtypeStruct((t, d), F32),
                   jax.ShapeDtypeStruct((TOP_K, t), jnp.int32), jax.ShapeDtypeStruct((TOP_K, t), F32)],
        compiler_params=_params("arbitrary"),
        name="oproj_router",
    )(merged, x2, wo_bf, g.reshape(1, d), wr_t, br.reshape(n_e, 1))


def _dispatch_kernel(dest_ref, x_ref, xs_in_ref, xs_ref, sem):
    del xs_in_ref
    tm = x_ref.shape[0]

    def copy(k, r):
        return pltpu.make_async_copy(x_ref.at[pl.ds(r, 1), :], xs_ref.at[pl.ds(dest_ref[k, r], 1), :], sem)

    def start(r, carry):
        for k in range(TOP_K):
            copy(k, r).start()
        return carry

    def wait(r, carry):
        for k in range(TOP_K):
            copy(k, r).wait()
        return carry

    lax.fori_loop(0, tm, start, 0)
    lax.fori_loop(0, tm, wait, 0)


def _dispatch(dest, xn, xs_zero, tm):
    t, d = xn.shape
    return pl.pallas_call(
        _dispatch_kernel,
        grid=(t // tm,),
        in_specs=[pl.BlockSpec((TOP_K, tm), lambda i: (0, i), memory_space=pltpu.SMEM),
                  pl.BlockSpec((tm, d), lambda i: (i, 0)),
                  pl.BlockSpec(memory_space=pl.ANY)],
        out_specs=pl.BlockSpec(memory_space=pl.ANY),
        out_shape=jax.ShapeDtypeStruct(xs_zero.shape, xs_zero.dtype),
        scratch_shapes=[pltpu.SemaphoreType.DMA(())],
        input_output_aliases={2: 0},
        compiler_params=_params("arbitrary"),
        name="moe_dispatch",
    )(dest, xn, xs_zero)


def _up_kernel(xblk_ref, oblk_ref, wcol_ref, ocol_ref, exp_ref, new_ref, ok_ref,
               x_ref, wg_ref, wu_ref, bg_ref, bu_ref, o_ref, wgb_ref, wub_ref):
    i = pl.program_id(0)

    @pl.when(new_ref[i] == 1)
    def _():
        wgb_ref[...] = wg_ref[0].astype(BF16)
        wub_ref[...] = wu_ref[0].astype(BF16)

    @pl.when(ok_ref[i] == 1)
    def _():
        x = x_ref[...].astype(BF16)
        g = jnp.dot(x, wgb_ref[...], preferred_element_type=F32) + bg_ref[0]
        u = jnp.dot(x, wub_ref[...], preferred_element_type=F32) + bu_ref[0]
        g = jnp.minimum(g, SWIGLU_LIMIT)
        u = jnp.clip(u, -SWIGLU_LIMIT, SWIGLU_LIMIT)
        o_ref[...] = ((u + 1.0) * (g * jax.nn.sigmoid(SWIGLU_ALPHA * g))).astype(o_ref.dtype)

    @pl.when(ok_ref[i] == 0)
    def _():
        o_ref[...] = jnp.zeros_like(o_ref)


def _down_kernel(xblk_ref, oblk_ref, wcol_ref, ocol_ref, exp_ref, new_ref, ok_ref,
                 h_ref, wd_ref, bd_ref, o_ref, wdb_ref):
    i = pl.program_id(0)

    @pl.when(new_ref[i] == 1)
    def _():
        wdb_ref[...] = wd_ref[0].astype(BF16)

    @pl.when(ok_ref[i] == 1)
    def _():
        o_ref[...] = jnp.dot(h_ref[...], wdb_ref[...], preferred_element_type=F32) + bd_ref[0]

    @pl.when(ok_ref[i] == 0)
    def _():
        o_ref[...] = jnp.zeros_like(o_ref)


def _expert_items(blk_start, blk_count, n_col, n_blocks):
    n_items = n_blocks * n_col
    item_end = jnp.cumsum(blk_count * n_col)
    item_start = item_end - blk_count * n_col
    n_used = item_end[-1]
    blocks_used = jnp.sum(blk_count)
    i = jnp.arange(n_items, dtype=jnp.int32)
    ok = i < n_used
    ic = jnp.minimum(i, n_used - 1)
    e = jnp.searchsorted(item_end, ic, side="right").astype(jnp.int32)
    local = ic - item_start[e]
    cnt = jnp.maximum(blk_count[e], 1)
    wcol = local // cnt
    xblk = blk_start[e] + local % cnt
    spare = i - n_used
    oblk = jnp.where(ok, xblk, blocks_used + spare // n_col)
    ocol = jnp.where(ok, wcol, spare % n_col)
    key = e * n_col + wcol
    new = jnp.concatenate([jnp.ones((1,), bool), key[1:] != key[:-1]]) & ok
    as_i32 = lambda a: a.astype(jnp.int32)
    return tuple(map(as_i32, (xblk, oblk, wcol, ocol, e, new, ok)))


def _expert_up(items, xs, w_gate, b_gate, w_up, b_up, tm, tf):
    rows, d = xs.shape
    n_e, _, f = w_gate.shape
    n_items = items[0].shape[0]
    wspec = pl.BlockSpec((1, d, tf), lambda i, xb, ob, wc, oc, e, new, ok: (e[i], 0, wc[i]))
    bspec = pl.BlockSpec((1, 1, tf), lambda i, xb, ob, wc, oc, e, new, ok: (e[i], 0, wc[i]))
    grid_spec = pltpu.PrefetchScalarGridSpec(
        num_scalar_prefetch=len(items),
        grid=(n_items,),
        in_specs=[pl.BlockSpec((tm, d), lambda i, xb, ob, wc, oc, e, new, ok: (xb[i], 0)),
                  wspec, wspec, bspec, bspec],
        out_specs=pl.BlockSpec((tm, tf), lambda i, xb, ob, wc, oc, e, new, ok: (ob[i], oc[i])),
        scratch_shapes=[pltpu.VMEM((d, tf), BF16), pltpu.VMEM((d, tf), BF16)],
    )
    return pl.pallas_call(
        _up_kernel,
        grid_spec=grid_spec,
        out_shape=jax.ShapeDtypeStruct((rows, f), BF16),
        compiler_params=_params("arbitrary"),
        name="moe_up",
    )(*items, xs, w_gate, w_up, b_gate.reshape(n_e, 1, f), b_up.reshape(n_e, 1, f))


def _expert_down(items, hid, w_down, b_down, tm, tn):
    rows, f = hid.shape
    n_e, _, d = w_down.shape
    n_items = items[0].shape[0]
    grid_spec = pltpu.PrefetchScalarGridSpec(
        num_scalar_prefetch=len(items),
        grid=(n_items,),
        in_specs=[pl.BlockSpec((tm, f), lambda i, xb, ob, wc, oc, e, new, ok: (xb[i], 0)),
                  pl.BlockSpec((1, f, tn), lambda i, xb, ob, wc, oc, e, new, ok: (e[i], 0, wc[i])),
                  pl.BlockSpec((1, 1, tn), lambda i, xb, ob, wc, oc, e, new, ok: (e[i], 0, wc[i]))],
        out_specs=pl.BlockSpec((tm, tn), lambda i, xb, ob, wc, oc, e, new, ok: (ob[i], oc[i])),
        scratch_shapes=[pltpu.VMEM((f, tn), BF16)],
    )
    return pl.pallas_call(
        _down_kernel,
        grid_spec=grid_spec,
        out_shape=jax.ShapeDtypeStruct((rows, d), F32),
        compiler_params=_params("arbitrary"),
        name="moe_down",
    )(*items, hid, w_down, b_down.reshape(n_e, 1, d))


def _combine_kernel(dest_ref, h1_ref, w_ref, g_ref, ys_ref, o_ref, buf_ref, sem):
    tm = h1_ref.shape[0]

    def copy(k, r):
        return pltpu.make_async_copy(ys_ref.at[pl.ds(dest_ref[k, r], 1), :], buf_ref.at[k, pl.ds(r, 1), :], sem)

    def start(r, carry):
        for k in range(TOP_K):
            copy(k, r).start()
        return carry

    def wait(r, carry):
        for k in range(TOP_K):
            copy(k, r).wait()
        return carry

    lax.fori_loop(0, tm, start, 0)
    lax.fori_loop(0, tm, wait, 0)
    h2 = h1_ref[...]
    for k in range(TOP_K):
        h2 = h2 + w_ref[:, k:k + 1] * buf_ref[k]
    ms = jnp.mean(h2 * h2, axis=-1, keepdims=True)
    o_ref[...] = h2 * lax.rsqrt(ms + NORM_EPS) * g_ref[...]


def _combine(dest, h1, w_tk, g, ys, tm):
    t, d = h1.shape
    return pl.pallas_call(
        _combine_kernel,
        grid=(t // tm,),
        in_specs=[pl.BlockSpec((TOP_K, tm), lambda i: (0, i), memory_space=pltpu.SMEM),
                  pl.BlockSpec((tm, d), lambda i: (i, 0)),
                  pl.BlockSpec((tm, TOP_K), lambda i: (i, 0)),
                  pl.BlockSpec((1, d), lambda i: (0, 0)),
                  pl.BlockSpec(memory_space=pl.ANY)],
        out_specs=pl.BlockSpec((tm, d), lambda i: (i, 0)),
        out_shape=jax.ShapeDtypeStruct((t, d), F32),
        scratch_shapes=[pltpu.VMEM((TOP_K, tm, d), F32), pltpu.SemaphoreType.DMA(())],
        compiler_params=_params("arbitrary"),
        name="moe_combine",
    )(dest, h1, w_tk, g.reshape(1, d), ys)


def _rope_tables(seq):
    half = ROPE_DIM // 2
    inv_freq = jnp.exp(-math.log(ROPE_THETA) * jnp.arange(half, dtype=F32) / half)
    ang = jnp.arange(seq, dtype=jnp.int32).astype(F32)[:, None] * inv_freq[None, :]
    cos, sin = jnp.cos(ang), jnp.sin(ang)
    zeros = jnp.zeros((seq, HEAD_DIM - ROPE_DIM), F32)
    zh = jnp.zeros((seq, half), F32)
    c = jnp.concatenate([cos, cos, jnp.ones_like(zeros)], axis=1)
    sa = jnp.concatenate([-sin, zh, zeros], axis=1)
    sb = jnp.concatenate([zh, sin, zeros], axis=1)
    return c, sa, sb


def _routing_tables(top_e, n_e, tm, n_blocks):
    k, t = top_e.shape
    flat_e = top_e.reshape(k * t)
    onehot = (flat_e[:, None] == jnp.arange(n_e, dtype=jnp.int32)[None, :]).astype(jnp.int32)
    cum = jnp.cumsum(onehot, axis=0)
    rank = jnp.sum(onehot * cum, axis=1) - 1
    counts = cum[-1]
    blk_count = (counts + tm - 1) // tm
    blk_end = jnp.cumsum(blk_count)
    blk_start = blk_end - blk_count
    dest = (blk_start[flat_e] * tm + rank).astype(jnp.int32).reshape(k, t)
    return dest, blk_start.astype(jnp.int32), blk_count.astype(jnp.int32)


def kernel(x, norm_mix, w_in, conv_w, w_conv_out, w_attn_out, w_o, norm_ffn, w_router, b_router,
           w_gate, b_gate, w_up, b_up, w_down, b_down, norm_final):
    batch, seq, d = x.shape
    depth = w_in.shape[0]
    cw = w_conv_out.shape[1]
    aw = w_attn_out.shape[1]
    n_heads = aw // HEAD_DIM
    n_e = w_router.shape[-1]
    f = w_gate.shape[-1]
    t = batch * seq
    assert cw == d and aw == d and seq % MOBA_BLOCK == 0

    tm_norm = min(512, t)
    tm_in, tn_in = min(512, t), min(1024, w_in.shape[-1])
    tm_mid = min(256, seq)
    tm_moe = min(256, t)
    tf_up = min(1024, f)
    tn_down = min(1024, d)
    tm_disp = min(128, t)
    tm_comb = min(128, t)
    n_blocks = (t * TOP_K + n_e * (tm_moe - 1)) // tm_moe
    moe_rows = n_blocks * tm_moe

    rope_c, rope_sa, rope_sb = _rope_tables(seq)
    h = x.reshape(t, d)
    for l in range(depth):
        xn = _rmsnorm(h, norm_mix[l], tm_norm, BF16)
        proj = _inproj(xn, w_in[l], tm_in, tn_in)
        col = lambda width_off: width_off // HEAD_DIM
        attn = _attention(proj, rope_c, rope_sa, rope_sb, batch, seq, n_heads,
                          col(3 * cw), col(3 * cw + aw), col(3 * cw + 2 * aw))
        merged = _merge(proj, attn, conv_w[l], w_conv_out[l].astype(BF16), w_attn_out[l].astype(BF16),
                        seq, cw, aw, d, tm_mid)
        h1, xn2, top_e, top_w = _oproj(merged, h, w_o[l].astype(BF16), norm_ffn[l],
                                       w_router[l].T, b_router[l], tm_mid)
        dest, blk_start, blk_count = _routing_tables(top_e, n_e, tm_moe, n_blocks)
        xs = _dispatch(dest, xn2, jnp.zeros((moe_rows, d), F32), tm_disp)
        up_items = _expert_items(blk_start, blk_count, f // tf_up, n_blocks)
        hid = _expert_up(up_items, xs, w_gate[l], b_gate[l], w_up[l], b_up[l], tm_moe, tf_up)
        down_items = _expert_items(blk_start, blk_count, d // tn_down, n_blocks)
        ys = _expert_down(down_items, hid, w_down[l], b_down[l], tm_moe, tn_down)
        last = l == depth - 1
        assert last, "depth > 1 needs an un-normalised combine"
        h = _combine(dest, h1, top_w.T, norm_final, ys, tm_comb)
    return h.reshape(batch, seq, d)
```

```python
import functools
import math

import jax
import jax.numpy as jnp
from jax import lax
from jax.experimental import pallas as pl
from jax.experimental.pallas import tpu as pltpu

HEAD_DIM = 128
ROPE_DIM = HEAD_DIM // 4
ROPE_THETA = 500000.0
MOBA_BLOCK = 256
MOBA_TOPK = 3
CONV_K = 3
TOP_K = 4
SWIGLU_LIMIT = 7.0
SWIGLU_ALPHA = 1.702
NORM_EPS = 1e-5
NEG = -1e30
ATTN_HEADS_PER_STEP = 4

V7X_LANES = 128
V7X_BF16_SUBLANES = 16
V7X_VMEM_BYTES = 64 * 1024 * 1024
VMEM_LIMIT = V7X_VMEM_BYTES - 8 * 1024 * 1024

F32 = jnp.float32
BF16 = jnp.bfloat16
NT_DIMS = (((1,), (1,)), ((), ()))


def _params(*sem):
    return pltpu.CompilerParams(dimension_semantics=sem, vmem_limit_bytes=VMEM_LIMIT)


def _rmsnorm_kernel(x_ref, g_ref, o_ref):
    x = x_ref[...]
    ms = jnp.mean(x * x, axis=-1, keepdims=True)
    o_ref[...] = (x * lax.rsqrt(ms + NORM_EPS) * g_ref[...]).astype(o_ref.dtype)


def _rmsnorm(x, g, tm, out_dtype):
    t, d = x.shape
    return pl.pallas_call(
        _rmsnorm_kernel,
        grid=(t // tm,),
        in_specs=[pl.BlockSpec((tm, d), lambda i: (i, 0)), pl.BlockSpec((1, d), lambda i: (0, 0))],
        out_specs=pl.BlockSpec((tm, d), lambda i: (i, 0)),
        out_shape=jax.ShapeDtypeStruct((t, d), out_dtype),
        compiler_params=_params("arbitrary"),
        name="rmsnorm",
    )(x, g.reshape(1, d))


def _inproj_kernel(x_ref, w_ref, o_ref, wbf_ref):
    @pl.when(pl.program_id(1) == 0)
    def _():
        wbf_ref[...] = w_ref[...].astype(BF16)

    o_ref[...] = jnp.dot(x_ref[...], wbf_ref[...], preferred_element_type=F32).astype(o_ref.dtype)


def _inproj(xn, w, tm, tn):
    t, d = xn.shape
    n = w.shape[1]
    return pl.pallas_call(
        _inproj_kernel,
        grid=(n // tn, t // tm),
        in_specs=[pl.BlockSpec((tm, d), lambda j, i: (i, 0)), pl.BlockSpec((d, tn), lambda j, i: (0, j))],
        out_specs=pl.BlockSpec((tm, tn), lambda j, i: (i, j)),
        out_shape=jax.ShapeDtypeStruct((t, n), BF16),
        scratch_shapes=[pltpu.VMEM((d, tn), BF16)],
        compiler_params=_params("arbitrary", "arbitrary"),
        name="inproj",
    )(xn, w)


def _rope(t, c, sa, sb):
    half = ROPE_DIM // 2
    return t * c + pltpu.roll(t, HEAD_DIM - half, 1) * sa + pltpu.roll(t, half, 1) * sb


def _attn_kernel(q_ref, k_ref, v_ref, c_ref, sa_ref, sb_ref, o_ref,
                 kaug_ref, vt_ref, kmean_ref, m_ref, l_ref, acc_ref, *, n_kb, kb_pad, n_hd, exp_scale):
    blk = MOBA_BLOCK
    qi = pl.program_id(1)
    heads = range(n_hd)
    hcols = lambda hd: slice(hd * HEAD_DIM, (hd + 1) * HEAD_DIM)

    @pl.when(qi == 0)
    def _prep():
        kmean_ref[...] = jnp.zeros_like(kmean_ref)
        lane = lax.broadcasted_iota(jnp.int32, (blk, HEAD_DIM), 1)

        def body(j, carry):
            rows = pl.ds(pl.multiple_of(j * blk, blk), blk)
            onehot = jnp.where(lane == j, 1.0, 0.0).astype(BF16)
            for hd in heads:
                kr = _rope(k_ref[rows, hcols(hd)].astype(F32), c_ref[rows, :], sa_ref[rows, :], sb_ref[rows, :])
                kaug_ref[hd, rows, 0:HEAD_DIM] = kr.astype(BF16)
                kaug_ref[hd, rows, HEAD_DIM:2 * HEAD_DIM] = onehot
                kmean_ref[hd, pl.ds(j, 1), :] = jnp.mean(kr, axis=0, keepdims=True)
                vt_ref[hd, j] = v_ref[rows, hcols(hd)].astype(F32).T.astype(BF16)
            return carry

        lax.fori_loop(0, n_kb, body, 0)

    rows = pl.ds(pl.multiple_of(qi * blk, blk), blk)
    sub = lax.broadcasted_iota(jnp.int32, (kb_pad, blk), 0)
    valid = sub < qi
    key_i = lax.broadcasted_iota(jnp.int32, (blk, blk), 0)
    qry_i = lax.broadcasted_iota(jnp.int32, (blk, blk), 1)
    q_augs = []
    for hd in heads:
        qr = _rope(q_ref[:, hcols(hd)].astype(F32), c_ref[rows, :], sa_ref[rows, :], sb_ref[rows, :])
        qrt = qr.T

        gate = jnp.dot(kmean_ref[hd], qrt, precision=lax.Precision.HIGHEST, preferred_element_type=F32)
        g = jnp.where(valid, gate, -jnp.inf)
        bias = jnp.full((kb_pad, blk), NEG, F32)
        for _ in range(MOBA_TOPK):
            top = jnp.max(g, axis=0, keepdims=True)
            idx = jnp.min(jnp.where(g == top, sub, kb_pad), axis=0, keepdims=True)
            pick = sub == idx
            bias = jnp.where(pick, jnp.where(valid, 0.0, bias), bias)
            g = jnp.where(pick, -jnp.inf, g)

        qrt_bf = (qrt * exp_scale).astype(BF16)
        q_augs.append(jnp.concatenate(
            [qrt_bf, bias.astype(BF16), jnp.zeros((HEAD_DIM - kb_pad, blk), BF16)], axis=0))

        s = jnp.dot(kaug_ref[hd, rows, 0:HEAD_DIM], qrt_bf, preferred_element_type=F32)
        s = jnp.where(key_i <= qry_i, s, NEG)
        m0 = jnp.max(s, axis=0, keepdims=True)
        p = jnp.exp2(s - m0)
        m_ref[hd] = m0
        l_ref[hd] = jnp.sum(p, axis=0, keepdims=True)
        acc_ref[hd] = jnp.dot(vt_ref[hd, qi], p.astype(BF16), preferred_element_type=F32)

    def body(pair, carry):
        j0 = 2 * pair
        keys = pl.ds(pl.multiple_of(j0 * blk, 2 * blk), 2 * blk)
        sjs = [jnp.dot(kaug_ref[hd, keys, :], q_augs[hd], preferred_element_type=F32) for hd in heads]
        pbs, alphas = [], []
        for hd in heads:
            m_prev = m_ref[hd]
            m_new = jnp.maximum(m_prev, jnp.max(sjs[hd], axis=0, keepdims=True))
            alpha = jnp.exp2(m_prev - m_new)
            pj = jnp.exp2(sjs[hd] - m_new)
            l_ref[hd] = alpha * l_ref[hd] + jnp.sum(pj, axis=0, keepdims=True)
            m_ref[hd] = m_new
            pbs.append(pj.astype(BF16))
            alphas.append(alpha)
        for hd in heads:
            pv = (jnp.dot(vt_ref[hd, j0], pbs[hd][0:blk], preferred_element_type=F32)
                  + jnp.dot(vt_ref[hd, j0 + 1], pbs[hd][blk:2 * blk], preferred_element_type=F32))
            acc_ref[hd] = alphas[hd] * acc_ref[hd] + pv
        return carry

    lax.fori_loop(0, (qi + 1) // 2, body, 0)
    for hd in heads:
        o_ref[:, hcols(hd)] = (acc_ref[hd] / l_ref[hd]).T.astype(o_ref.dtype)


def _attention(proj, rope_c, rope_sa, rope_sb, batch, seq, n_heads, q_col, k_col, v_col):
    t = proj.shape[0]
    blk = MOBA_BLOCK
    n_kb = seq // blk
    kb_pad = -(-n_kb // V7X_BF16_SUBLANES) * V7X_BF16_SUBLANES
    n_hd = ATTN_HEADS_PER_STEP
    wide = n_hd * HEAD_DIM
    assert n_kb % 2 == 0 and kb_pad <= HEAD_DIM and n_heads % n_hd == 0
    assert q_col % n_hd == 0 and k_col % n_hd == 0 and v_col % n_hd == 0
    groups = n_heads // n_hd
    kernel = functools.partial(_attn_kernel, n_kb=n_kb, kb_pad=kb_pad, n_hd=n_hd,
                               exp_scale=HEAD_DIM ** -0.5 * math.log2(math.e))
    tab = pl.BlockSpec((seq, HEAD_DIM), lambda bg, qi: (0, 0))
    return pl.pallas_call(
        kernel,
        grid=(batch * groups, n_kb),
        in_specs=[
            pl.BlockSpec((blk, wide), lambda bg, qi: ((bg // groups) * n_kb + qi, q_col // n_hd + bg % groups)),
            pl.BlockSpec((seq, wide), lambda bg, qi: (bg // groups, k_col // n_hd + bg % groups)),
            pl.BlockSpec((seq, wide), lambda bg, qi: (bg // groups, v_col // n_hd + bg % groups)),
            tab, tab, tab,
        ],
        out_specs=pl.BlockSpec((blk, wide), lambda bg, qi: ((bg // groups) * n_kb + qi, bg % groups)),
        out_shape=jax.ShapeDtypeStruct((t, n_heads * HEAD_DIM), BF16),
        scratch_shapes=[
            pltpu.VMEM((n_hd, seq, 2 * HEAD_DIM), BF16),
            pltpu.VMEM((n_hd, n_kb, HEAD_DIM, blk), BF16),
            pltpu.VMEM((n_hd, kb_pad, HEAD_DIM), F32),
            pltpu.VMEM((n_hd, 1, blk), F32),
            pltpu.VMEM((n_hd, 1, blk), F32),
            pltpu.VMEM((n_hd, HEAD_DIM, blk), F32),
        ],
        compiler_params=_params("arbitrary", "arbitrary"),
        name="moba_attention",
    )(proj, proj, proj, rope_c, rope_sa, rope_sb)


def _merge_kernel(b_ref, c_ref, h_ref, cp_ref, hp_ref, gc_ref, ga_ref, at_ref, cw_ref, wc_ref, wa_ref,
                  o_ref, *, tiles_per_seq):
    hal = V7X_BF16_SUBLANES
    u = c_ref[...].astype(F32) * h_ref[...].astype(F32)
    up = cp_ref[...].astype(F32) * hp_ref[...].astype(F32)
    up = jnp.where(pl.program_id(0) % tiles_per_seq == 0, 0.0, up)
    row = lax.broadcasted_iota(jnp.int32, u.shape, 0)
    u1 = jnp.where(row == 0, up[hal - 1:hal, :], pltpu.roll(u, 1, 0))
    u2 = jnp.where(row == 0, up[hal - 2:hal - 1, :],
                   jnp.where(row == 1, up[hal - 1:hal, :], pltpu.roll(u, 2, 0)))
    y = cw_ref[0:1, :] * u2 + cw_ref[1:2, :] * u1 + cw_ref[2:3, :] * u
    cm = (b_ref[...].astype(F32) * y).astype(BF16)
    yc = jnp.dot(cm, wc_ref[...], preferred_element_type=F32)
    ya = jnp.dot(at_ref[...], wa_ref[...], preferred_element_type=F32)
    merged = jax.nn.sigmoid(gc_ref[...].astype(F32)) * yc + jax.nn.sigmoid(ga_ref[...].astype(F32)) * ya
    o_ref[...] = merged.astype(o_ref.dtype)


def _merge(proj, attn, conv_w, wc_bf, wa_bf, seq, cw, aw, d, tm):
    t = proj.shape[0]
    hal = V7X_BF16_SUBLANES
    kernel = functools.partial(_merge_kernel, tiles_per_seq=seq // tm)
    col = lambda cidx: pl.BlockSpec((tm, cw), lambda i: (i, cidx))
    halo = lambda cidx: pl.BlockSpec((hal, cw), lambda i: (jnp.maximum(i * (tm // hal) - 1, 0), cidx))
    const = lambda shape: pl.BlockSpec(shape, lambda i: (0, 0))
    return pl.pallas_call(
        kernel,
        grid=(t // tm,),
        in_specs=[col(0), col(1), col(2), halo(1), halo(2), col(6), col(7),
                  pl.BlockSpec((tm, aw), lambda i: (i, 0)),
                  const((CONV_K, cw)), const((cw, d)), const((aw, d))],
        out_specs=pl.BlockSpec((tm, d), lambda i: (i, 0)),
        out_shape=jax.ShapeDtypeStruct((t, d), BF16),
        compiler_params=_params("arbitrary"),
        name="merge_branches",
    )(proj, proj, proj, proj, proj, proj, proj, attn, conv_w, wc_bf, wa_bf)


def _oproj_kernel(mg_ref, x_ref, wo_ref, g_ref, wrt_ref, br_ref,
                  h1_ref, xn_ref, te_ref, tw_ref, rk_ref, cnt_ref, run_ref):
    h1 = x_ref[...] + jnp.dot(mg_ref[...], wo_ref[...], preferred_element_type=F32)
    h1_ref[...] = h1
    ms = jnp.mean(h1 * h1, axis=-1, keepdims=True)
    xn = h1 * lax.rsqrt(ms + NORM_EPS) * g_ref[...]
    xn_ref[...] = xn
    lg = lax.dot_general(wrt_ref[...], xn, NT_DIMS, precision=lax.Precision.HIGHEST,
                         preferred_element_type=F32) + br_ref[...]
    n_e, tm = lg.shape
    sub = lax.broadcasted_iota(jnp.int32, lg.shape, 0)
    vals, idxs = [], []
    for _ in range(TOP_K):
        top = jnp.max(lg, axis=0, keepdims=True)
        idx = jnp.min(jnp.where(lg == top, sub, n_e), axis=0, keepdims=True)
        vals.append(top)
        idxs.append(idx)
        lg = jnp.where(sub == idx, -jnp.inf, lg)
    v = jnp.concatenate(vals, axis=0)
    ex = jnp.exp(v - vals[0])
    te_ref[...] = jnp.concatenate(idxs, axis=0)
    tw_ref[...] = ex / jnp.sum(ex, axis=0, keepdims=True)

    @pl.when(pl.program_id(0) == 0)
    def _():
        run_ref[...] = jnp.zeros_like(run_ref)

    upper = jnp.where(lax.broadcasted_iota(jnp.int32, (tm, tm), 0) <= lax.broadcasted_iota(jnp.int32, (tm, tm), 1),
                      1.0, 0.0).astype(BF16)
    run = run_ref[...]
    ranks = []
    for k in range(TOP_K):
        onehot = jnp.where(sub == idxs[k], 1.0, 0.0)
        incl = jnp.dot(onehot.astype(BF16), upper, preferred_element_type=F32)
        ranks.append(jnp.sum(onehot * (incl - 1.0 + run[:, 0:1]), axis=0, keepdims=True))
        run = run + incl[:, tm - 1:tm]
    run_ref[...] = run
    rk_ref[...] = jnp.concatenate(ranks, axis=0).astype(jnp.int32)
    cnt_ref[...] = run.astype(jnp.int32)


def _oproj(merged, x2, wo_bf, g, wr_t, br, tm):
    t, d = x2.shape
    n_e = wr_t.shape[0]
    const = lambda shape: pl.BlockSpec(shape, lambda i: (0, 0))
    row = pl.BlockSpec((tm, d), lambda i: (i, 0))
    top = pl.BlockSpec((TOP_K, tm), lambda i: (0, i))
    return pl.pallas_call(
        _oproj_kernel,
        grid=(t // tm,),
        in_specs=[row, row, const((d, d)), const((1, d)), const((n_e, d)), const((n_e, 1))],
        out_specs=[row, row, top, top, top, const((n_e, V7X_LANES))],
        out_shape=[jax.ShapeDtypeStruct((t, d), F32), jax.ShapeDtypeStruct((t, d), F32),
                   jax.ShapeDtypeStruct((TOP_K, t), jnp.int32), jax.ShapeDtypeStruct((TOP_K, t), F32),
                   jax.ShapeDtypeStruct((TOP_K, t), jnp.int32), jax.ShapeDtypeStruct((n_e, V7X_LANES), jnp.int32)],
        scratch_shapes=[pltpu.VMEM((n_e, V7X_LANES), F32)],
        compiler_params=_params("arbitrary"),
        name="oproj_router",
    )(merged, x2, wo_bf, g.reshape(1, d), wr_t, br.reshape(n_e, 1))


def _dispatch_kernel(dest_ref, fill_ref, x_ref, xs_ref, zero_ref, sem, zsem, *, n_e, n_blocks, tm_moe):
    tm = x_ref.shape[0]

    @pl.when(pl.program_id(0) == 0)
    def _():
        zero_ref[...] = jnp.zeros_like(zero_ref)

        def zcopy(b):
            return pltpu.make_async_copy(
                zero_ref, xs_ref.at[pl.ds(pl.multiple_of(b * tm_moe, tm_moe), tm_moe), :], zsem)

        def zstart(b, carry):
            zcopy(b).start()
            return carry

        def zwait(b, carry):
            zcopy(b).wait()
            return carry

        for e in range(n_e):
            @pl.when(fill_ref[e] >= 0)
            def _():
                zcopy(fill_ref[e]).start()
        lax.fori_loop(fill_ref[n_e], n_blocks, zstart, 0)
        for e in range(n_e):
            @pl.when(fill_ref[e] >= 0)
            def _():
                zcopy(fill_ref[e]).wait()
        lax.fori_loop(fill_ref[n_e], n_blocks, zwait, 0)

    def copy(k, r):
        return pltpu.make_async_copy(x_ref.at[pl.ds(r, 1), :], xs_ref.at[pl.ds(dest_ref[k, r], 1), :], sem)

    def start(r, carry):
        for k in range(TOP_K):
            copy(k, r).start()
        return carry

    def wait(r, carry):
        for k in range(TOP_K):
            copy(k, r).wait()
        return carry

    lax.fori_loop(0, tm, start, 0)
    lax.fori_loop(0, tm, wait, 0)


def _dispatch(dest, fill, xn, n_blocks, tm_moe, tm):
    t, d = xn.shape
    n_e = fill.shape[0] - 1
    kernel = functools.partial(_dispatch_kernel, n_e=n_e, n_blocks=n_blocks, tm_moe=tm_moe)
    return pl.pallas_call(
        kernel,
        grid=(t // tm,),
        in_specs=[pl.BlockSpec((TOP_K, tm), lambda i: (0, i), memory_space=pltpu.SMEM),
                  pl.BlockSpec(memory_space=pltpu.SMEM),
                  pl.BlockSpec((tm, d), lambda i: (i, 0))],
        out_specs=pl.BlockSpec(memory_space=pl.ANY),
        out_shape=jax.ShapeDtypeStruct((n_blocks * tm_moe, d), F32),
        scratch_shapes=[pltpu.VMEM((tm_moe, d), F32), pltpu.SemaphoreType.DMA(()), pltpu.SemaphoreType.DMA(())],
        compiler_params=_params("arbitrary"),
        name="moe_dispatch",
    )(dest, fill, xn)


N_ITEM_ARRAYS = 11


def _expert_items(blk_start, blk_count, n_col, n_blocks):
    n_e = blk_count.shape[0]
    n_items = n_blocks * n_col
    item_end = jnp.cumsum(blk_count * n_col)
    item_start = item_end - blk_count * n_col
    n_used = item_end[-1]
    blocks_used = jnp.sum(blk_count)
    i = jnp.arange(n_items, dtype=jnp.int32)
    ok = i < n_used

    def at(idx):
        ic = jnp.minimum(idx, n_used - 1)
        e = jnp.sum(ic[:, None] >= item_end[None, :], axis=1).astype(jnp.int32)
        local = ic - item_start[e]
        cnt = jnp.maximum(blk_count[e], 1)
        return e, local // cnt, blk_start[e] + local % cnt, cnt

    e, wcol, xblk, cnt = at(i)
    spare = i - n_used
    oblk = jnp.where(ok, xblk, blocks_used + spare // n_col)
    ocol = jnp.where(ok, wcol, spare % n_col)
    new = ok & ((xblk == blk_start[e]))
    nonempty_before = jnp.cumsum(blk_count > 0) - (blk_count > 0)
    slot = (nonempty_before[e] * n_col + wcol) % 2
    nxt = i + cnt
    nxt_e, nxt_col, _, _ = at(nxt)
    has_nxt = new & (nxt < n_used)
    as_i32 = lambda a: a.astype(jnp.int32)
    items = tuple(map(as_i32, (xblk, oblk, wcol, ocol, e, new, ok, slot, nxt_e, nxt_col, has_nxt)))
    assert len(items) == N_ITEM_ARRAYS and n_e == item_end.shape[0]
    return items


def _weight_tile_copy(w_hbm, stage_ref, sem, e, col, slot, which, tn):
    cols = pl.ds(pl.multiple_of(col * tn, tn), tn)
    return pltpu.make_async_copy(w_hbm.at[e, :, cols], stage_ref.at[slot, which], sem.at[slot, which])


def _stream_weight_tiles(i, items, w_hbms, stage_ref, sem, bf_refs):
    _, _, wcol, _, exp, new, _, slot, nxt_e, nxt_col, has_nxt = items
    tn = bf_refs[0].shape[1]

    @pl.when(new[i] == 1)
    def _():
        s = slot[i]

        @pl.when(i == 0)
        def _():
            for which, w in enumerate(w_hbms):
                _weight_tile_copy(w, stage_ref, sem, exp[i], wcol[i], s, which, tn).start()

        for which, w in enumerate(w_hbms):
            _weight_tile_copy(w, stage_ref, sem, exp[i], wcol[i], s, which, tn).wait()

        @pl.when(has_nxt[i] == 1)
        def _():
            for which, w in enumerate(w_hbms):
                _weight_tile_copy(w, stage_ref, sem, nxt_e[i], nxt_col[i], 1 - s, which, tn).start()

        for which, bf in enumerate(bf_refs):
            bf[...] = stage_ref[s, which].astype(BF16)


def _up_kernel(*refs):
    items = refs[:N_ITEM_ARRAYS]
    x_ref, wg_hbm, wu_hbm, bg_ref, bu_ref, o_ref, stage_ref, wgb_ref, wub_ref, sem = refs[N_ITEM_ARRAYS:]
    ok = items[6]
    i = pl.program_id(0)
    _stream_weight_tiles(i, items, (wg_hbm, wu_hbm), stage_ref, sem, (wgb_ref, wub_ref))

    @pl.when(ok[i] == 1)
    def _():
        x = x_ref[...].astype(BF16)
        g = jnp.dot(x, wgb_ref[...], preferred_element_type=F32) + bg_ref[0]
        u = jnp.dot(x, wub_ref[...], preferred_element_type=F32) + bu_ref[0]
        g = jnp.minimum(g, SWIGLU_LIMIT)
        u = jnp.clip(u, -SWIGLU_LIMIT, SWIGLU_LIMIT)
        o_ref[...] = ((u + 1.0) * (g * jax.nn.sigmoid(SWIGLU_ALPHA * g))).astype(o_ref.dtype)

    @pl.when(ok[i] == 0)
    def _():
        o_ref[...] = jnp.zeros_like(o_ref)


def _down_kernel(*refs):
    items = refs[:N_ITEM_ARRAYS]
    h_ref, wd_hbm, bd_ref, o_ref, stage_ref, wdb_ref, sem = refs[N_ITEM_ARRAYS:]
    ok = items[6]
    i = pl.program_id(0)
    _stream_weight_tiles(i, items, (wd_hbm,), stage_ref, sem, (wdb_ref,))

    @pl.when(ok[i] == 1)
    def _():
        o_ref[...] = jnp.dot(h_ref[...], wdb_ref[...], preferred_element_type=F32) + bd_ref[0]

    @pl.when(ok[i] == 0)
    def _():
        o_ref[...] = jnp.zeros_like(o_ref)


def _item_map(fn):
    return lambda i, *items: fn(i, *items[:5])


def _expert_up(items, xs, w_gate, b_gate, w_up, b_up, tm, tf):
    rows, d = xs.shape
    n_e, _, f = w_gate.shape
    n_items = items[0].shape[0]
    bspec = pl.BlockSpec((1, 1, tf), _item_map(lambda i, xb, ob, wc, oc, e: (e[i], 0, wc[i])))
    hbm = pl.BlockSpec(memory_space=pl.ANY)
    grid_spec = pltpu.PrefetchScalarGridSpec(
        num_scalar_prefetch=N_ITEM_ARRAYS,
        grid=(n_items,),
        in_specs=[pl.BlockSpec((tm, d), _item_map(lambda i, xb, ob, wc, oc, e: (xb[i], 0))),
                  hbm, hbm, bspec, bspec],
        out_specs=pl.BlockSpec((tm, tf), _item_map(lambda i, xb, ob, wc, oc, e: (ob[i], oc[i]))),
        scratch_shapes=[pltpu.VMEM((2, 2, d, tf), F32), pltpu.VMEM((d, tf), BF16), pltpu.VMEM((d, tf), BF16),
                        pltpu.SemaphoreType.DMA((2, 2))],
    )
    return pl.pallas_call(
        _up_kernel,
        grid_spec=grid_spec,
        out_shape=jax.ShapeDtypeStruct((rows, f), BF16),
        compiler_params=_params("arbitrary"),
        name="moe_up",
    )(*items, xs, w_gate, w_up, b_gate.reshape(n_e, 1, f), b_up.reshape(n_e, 1, f))


def _expert_down(items, hid, w_down, b_down, tm, tn):
    rows, f = hid.shape
    n_e, _, d = w_down.shape
    n_items = items[0].shape[0]
    grid_spec = pltpu.PrefetchScalarGridSpec(
        num_scalar_prefetch=N_ITEM_ARRAYS,
        grid=(n_items,),
        in_specs=[pl.BlockSpec((tm, f), _item_map(lambda i, xb, ob, wc, oc, e: (xb[i], 0))),
                  pl.BlockSpec(memory_space=pl.ANY),
                  pl.BlockSpec((1, 1, tn), _item_map(lambda i, xb, ob, wc, oc, e: (e[i], 0, wc[i])))],
        out_specs=pl.BlockSpec((tm, tn), _item_map(lambda i, xb, ob, wc, oc, e: (ob[i], oc[i]))),
        scratch_shapes=[pltpu.VMEM((2, 1, f, tn), F32), pltpu.VMEM((f, tn), BF16),
                        pltpu.SemaphoreType.DMA((2, 1))],
    )
    return pl.pallas_call(
        _down_kernel,
        grid_spec=grid_spec,
        out_shape=jax.ShapeDtypeStruct((rows, d), F32),
        compiler_params=_params("arbitrary"),
        name="moe_down",
    )(*items, hid, w_down, b_down.reshape(n_e, 1, d))


def _combine_kernel(dest_ref, h1_ref, w_ref, g_ref, ys_ref, o_ref, buf_ref, sem):
    tm = h1_ref.shape[0]

    def copy(k, r):
        return pltpu.make_async_copy(ys_ref.at[pl.ds(dest_ref[k, r], 1), :], buf_ref.at[k, pl.ds(r, 1), :], sem)

    def start(r, carry):
        for k in range(TOP_K):
            copy(k, r).start()
        return carry

    def wait(r, carry):
        for k in range(TOP_K):
            copy(k, r).wait()
        return carry

    lax.fori_loop(0, tm, start, 0)
    lax.fori_loop(0, tm, wait, 0)
    h2 = h1_ref[...]
    for k in range(TOP_K):
        h2 = h2 + w_ref[:, k:k + 1] * buf_ref[k]
    ms = jnp.mean(h2 * h2, axis=-1, keepdims=True)
    o_ref[...] = h2 * lax.rsqrt(ms + NORM_EPS) * g_ref[...]


def _combine(dest, h1, w_tk, g, ys, tm):
    t, d = h1.shape
    return pl.pallas_call(
        _combine_kernel,
        grid=(t // tm,),
        in_specs=[pl.BlockSpec((TOP_K, tm), lambda i: (0, i), memory_space=pltpu.SMEM),
                  pl.BlockSpec((tm, d), lambda i: (i, 0)),
                  pl.BlockSpec((tm, TOP_K), lambda i: (i, 0)),
                  pl.BlockSpec((1, d), lambda i: (0, 0)),
                  pl.BlockSpec(memory_space=pl.ANY)],
        out_specs=pl.BlockSpec((tm, d), lambda i: (i, 0)),
        out_shape=jax.ShapeDtypeStruct((t, d), F32),
        scratch_shapes=[pltpu.VMEM((TOP_K, tm, d), F32), pltpu.SemaphoreType.DMA(())],
        compiler_params=_params("arbitrary"),
        name="moe_combine",
    )(dest, h1, w_tk, g.reshape(1, d), ys)


def _rope_tables(seq):
    half = ROPE_DIM // 2
    inv_freq = jnp.exp(-math.log(ROPE_THETA) * jnp.arange(half, dtype=F32) / half)
    ang = jnp.arange(seq, dtype=jnp.int32).astype(F32)[:, None] * inv_freq[None, :]
    cos, sin = jnp.cos(ang), jnp.sin(ang)
    zeros = jnp.zeros((seq, HEAD_DIM - ROPE_DIM), F32)
    zh = jnp.zeros((seq, half), F32)
    c = jnp.concatenate([cos, cos, jnp.ones_like(zeros)], axis=1)
    sa = jnp.concatenate([-sin, zh, zeros], axis=1)
    sb = jnp.concatenate([zh, sin, zeros], axis=1)
    return c, sa, sb


def _routing_tables(top_e, rank, counts, tm):
    n_e = counts.shape[0]
    blk_count = (counts + tm - 1) // tm
    blk_end = jnp.cumsum(blk_count)
    blk_start = blk_end - blk_count
    onehot = top_e[:, :, None] == jnp.arange(n_e, dtype=jnp.int32)[None, None, :]
    dest = rank + jnp.sum(jnp.where(onehot, (blk_start * tm)[None, None, :], 0), axis=-1)
    fill = jnp.concatenate([jnp.where(blk_count > 0, blk_end - 1, -1), blk_end[-1:]])
    return dest.astype(jnp.int32), blk_start.astype(jnp.int32), blk_count.astype(jnp.int32), fill.astype(jnp.int32)


def kernel(x, norm_mix, w_in, conv_w, w_conv_out, w_attn_out, w_o, norm_ffn, w_router, b_router,
           w_gate, b_gate, w_up, b_up, w_down, b_down, norm_final):
    batch, seq, d = x.shape
    cw = w_conv_out.shape[1]
    aw = w_attn_out.shape[1]
    n_heads = aw // HEAD_DIM
    n_e = w_router.shape[-1]
    f = w_gate.shape[-1]
    t = batch * seq
    assert w_in.shape[0] == 1, "single layer: the final rmsnorm is fused into the MoE combine"
    assert cw == d and aw == d and seq % MOBA_BLOCK == 0

    tm_norm = min(512, t)
    tm_in, tn_in = min(512, t), min(1024, w_in.shape[-1])
    tm_mid = min(256, seq)
    tm_moe = min(256, t)
    tf_up = min(1024, f)
    tn_down = min(1024, d)
    tm_disp = min(128, t)
    tm_comb = min(128, t)
    n_blocks = (t * TOP_K + n_e * (tm_moe - 1)) // tm_moe
    col = lambda width_off: width_off // HEAD_DIM

    rope_c, rope_sa, rope_sb = _rope_tables(seq)
    h = x.reshape(t, d)
    xn = _rmsnorm(h, norm_mix[0], tm_norm, BF16)
    proj = _inproj(xn, w_in[0], tm_in, tn_in)
    attn = _attention(proj, rope_c, rope_sa, rope_sb, batch, seq, n_heads,
                      col(3 * cw), col(3 * cw + aw), col(3 * cw + 2 * aw))
    merged = _merge(proj, attn, conv_w[0], w_conv_out[0].astype(BF16), w_attn_out[0].astype(BF16),
                    seq, cw, aw, d, tm_mid)
    h1, xn2, top_e, top_w, rank, counts = _oproj(merged, h, w_o[0].astype(BF16), norm_ffn[0],
                                                 w_router[0].T, b_router[0], tm_mid)
    dest, blk_start, blk_count, fill = _routing_tables(top_e, rank, counts[:, 0], tm_moe)
    xs = _dispatch(dest, fill, xn2, n_blocks, tm_moe, tm_disp)
    up_items = _expert_items(blk_start, blk_count, f // tf_up, n_blocks)
    hid = _expert_up(up_items, xs, w_gate[0], b_gate[0], w_up[0], b_up[0], tm_moe, tf_up)
    down_items = _expert_items(blk_start, blk_count, d // tn_down, n_blocks)
    ys = _expert_down(down_items, hid, w_down[0], b_down[0], tm_moe, tn_down)
    out = _combine(dest, h1, top_w.T, norm_final, ys, tm_comb)
    return out.reshape(batch, seq, d)
```

```python
import functools
import math

import jax
import jax.numpy as jnp
from jax import lax
from jax.experimental import pallas as pl
from jax.experimental.pallas import tpu as pltpu

HEAD_DIM = 128
ROPE_DIM = HEAD_DIM // 4
ROPE_THETA = 500000.0
MOBA_BLOCK = 256
MOBA_TOPK = 3
CONV_K = 3
TOP_K = 4
SWIGLU_LIMIT = 7.0
SWIGLU_ALPHA = 1.702
NORM_EPS = 1e-5
NEG = -1e30
ATTN_HEADS_PER_STEP = 4

V7X_LANES = 128
V7X_BF16_SUBLANES = 16
V7X_VMEM_BYTES = 64 * 1024 * 1024
VMEM_LIMIT = V7X_VMEM_BYTES - 8 * 1024 * 1024

F32 = jnp.float32
BF16 = jnp.bfloat16
NT_DIMS = (((1,), (1,)), ((), ()))
NN_DIMS = (((1,), (0,)), ((), ()))


def _params(*sem):
    return pltpu.CompilerParams(dimension_semantics=sem, vmem_limit_bytes=VMEM_LIMIT)


def _split_bf16(a):
    hi = a.astype(BF16)
    return hi, (a - hi.astype(F32)).astype(BF16)


def _dot3(a, b, dims):
    ah, al = _split_bf16(a)
    bh, bl = _split_bf16(b)
    dot = lambda u, v: lax.dot_general(u, v, dims, preferred_element_type=F32)
    return dot(ah, bh) + (dot(al, bh) + dot(ah, bl))


def _rmsnorm_kernel(x_ref, g_ref, o_ref):
    x = x_ref[...]
    ms = jnp.mean(x * x, axis=-1, keepdims=True)
    o_ref[...] = (x * lax.rsqrt(ms + NORM_EPS) * g_ref[...]).astype(o_ref.dtype)


def _rmsnorm(x, g, tm, out_dtype):
    t, d = x.shape
    return pl.pallas_call(
        _rmsnorm_kernel,
        grid=(t // tm,),
        in_specs=[pl.BlockSpec((tm, d), lambda i: (i, 0)), pl.BlockSpec((1, d), lambda i: (0, 0))],
        out_specs=pl.BlockSpec((tm, d), lambda i: (i, 0)),
        out_shape=jax.ShapeDtypeStruct((t, d), out_dtype),
        compiler_params=_params("arbitrary"),
        name="rmsnorm",
    )(x, g.reshape(1, d))


def _inproj_kernel(x_ref, w_ref, o_ref, wbf_ref):
    @pl.when(pl.program_id(1) == 0)
    def _():
        wbf_ref[...] = w_ref[...].astype(BF16)

    o_ref[...] = jnp.dot(x_ref[...], wbf_ref[...], preferred_element_type=F32).astype(o_ref.dtype)


def _inproj(xn, w, tm, tn):
    t, d = xn.shape
    n = w.shape[1]
    return pl.pallas_call(
        _inproj_kernel,
        grid=(n // tn, t // tm),
        in_specs=[pl.BlockSpec((tm, d), lambda j, i: (i, 0)), pl.BlockSpec((d, tn), lambda j, i: (0, j))],
        out_specs=pl.BlockSpec((tm, tn), lambda j, i: (i, j)),
        out_shape=jax.ShapeDtypeStruct((t, n), BF16),
        scratch_shapes=[pltpu.VMEM((d, tn), BF16)],
        compiler_params=_params("arbitrary", "arbitrary"),
        name="inproj",
    )(xn, w)


def _rope(t, c, sa, sb):
    half = ROPE_DIM // 2
    return t * c + pltpu.roll(t, HEAD_DIM - half, 1) * sa + pltpu.roll(t, half, 1) * sb


def _attn_kernel(q_ref, k_ref, v_ref, c_ref, sa_ref, sb_ref, o_ref,
                 kaug_ref, vt_ref, kmean_ref, m_ref, l_ref, acc_ref, *, n_kb, kb_pad, n_hd, exp_scale):
    blk = MOBA_BLOCK
    qi = pl.program_id(1)
    heads = range(n_hd)
    hcols = lambda hd: slice(hd * HEAD_DIM, (hd + 1) * HEAD_DIM)

    @pl.when(qi == 0)
    def _prep():
        kmean_ref[...] = jnp.zeros_like(kmean_ref)
        lane = lax.broadcasted_iota(jnp.int32, (blk, HEAD_DIM), 1)

        def body(j, carry):
            rows = pl.ds(pl.multiple_of(j * blk, blk), blk)
            onehot = jnp.where(lane == j, 1.0, 0.0).astype(BF16)
            for hd in heads:
                kr = _rope(k_ref[rows, hcols(hd)].astype(F32), c_ref[rows, :], sa_ref[rows, :], sb_ref[rows, :])
                kaug_ref[hd, rows, 0:HEAD_DIM] = kr.astype(BF16)
                kaug_ref[hd, rows, HEAD_DIM:2 * HEAD_DIM] = onehot
                kmean_ref[hd, pl.ds(j, 1), :] = jnp.mean(kr, axis=0, keepdims=True)
                vt_ref[hd, j] = v_ref[rows, hcols(hd)].astype(F32).T.astype(BF16)
            return carry

        lax.fori_loop(0, n_kb, body, 0)

    rows = pl.ds(pl.multiple_of(qi * blk, blk), blk)
    sub = lax.broadcasted_iota(jnp.int32, (kb_pad, blk), 0)
    valid = sub < qi
    key_i = lax.broadcasted_iota(jnp.int32, (blk, blk), 0)
    qry_i = lax.broadcasted_iota(jnp.int32, (blk, blk), 1)
    q_augs = []
    for hd in heads:
        qr = _rope(q_ref[:, hcols(hd)].astype(F32), c_ref[rows, :], sa_ref[rows, :], sb_ref[rows, :])
        qrt = qr.T

        gate = _dot3(kmean_ref[hd], qrt, NN_DIMS)
        g = jnp.where(valid, gate, -jnp.inf)
        bias = jnp.full((kb_pad, blk), NEG, F32)
        for _ in range(MOBA_TOPK):
            top = jnp.max(g, axis=0, keepdims=True)
            idx = jnp.min(jnp.where(g == top, sub, kb_pad), axis=0, keepdims=True)
            pick = sub == idx
            bias = jnp.where(pick, jnp.where(valid, 0.0, bias), bias)
            g = jnp.where(pick, -jnp.inf, g)

        qrt_bf = (qrt * exp_scale).astype(BF16)
        q_augs.append(jnp.concatenate(
            [qrt_bf, bias.astype(BF16), jnp.zeros((HEAD_DIM - kb_pad, blk), BF16)], axis=0))

        s = jnp.dot(kaug_ref[hd, rows, 0:HEAD_DIM], qrt_bf, preferred_element_type=F32)
        s = jnp.where(key_i <= qry_i, s, NEG)
        m0 = jnp.max(s, axis=0, keepdims=True)
        p = jnp.exp2(s - m0)
        m_ref[hd] = m0
        l_ref[hd] = jnp.sum(p, axis=0, keepdims=True)
        acc_ref[hd] = jnp.dot(vt_ref[hd, qi], p.astype(BF16), preferred_element_type=F32)

    def body(pair, carry):
        j0 = 2 * pair
        keys = pl.ds(pl.multiple_of(j0 * blk, 2 * blk), 2 * blk)
        sjs = [jnp.dot(kaug_ref[hd, keys, :], q_augs[hd], preferred_element_type=F32) for hd in heads]
        pbs, alphas = [], []
        for hd in heads:
            m_prev = m_ref[hd]
            m_new = jnp.maximum(m_prev, jnp.max(sjs[hd], axis=0, keepdims=True))
            alpha = jnp.exp2(m_prev - m_new)
            pj = jnp.exp2(sjs[hd] - m_new)
            l_ref[hd] = alpha * l_ref[hd] + jnp.sum(pj, axis=0, keepdims=True)
            m_ref[hd] = m_new
            pbs.append(pj.astype(BF16))
            alphas.append(alpha)
        for hd in heads:
            pv = (jnp.dot(vt_ref[hd, j0], pbs[hd][0:blk], preferred_element_type=F32)
                  + jnp.dot(vt_ref[hd, j0 + 1], pbs[hd][blk:2 * blk], preferred_element_type=F32))
            acc_ref[hd] = alphas[hd] * acc_ref[hd] + pv
        return carry

    lax.fori_loop(0, (qi + 1) // 2, body, 0)
    for hd in heads:
        o_ref[:, hcols(hd)] = (acc_ref[hd] / l_ref[hd]).T.astype(o_ref.dtype)


def _attention(proj, rope_c, rope_sa, rope_sb, batch, seq, n_heads, q_col, k_col, v_col):
    t = proj.shape[0]
    blk = MOBA_BLOCK
    n_kb = seq // blk
    kb_pad = -(-n_kb // V7X_BF16_SUBLANES) * V7X_BF16_SUBLANES
    n_hd = ATTN_HEADS_PER_STEP
    wide = n_hd * HEAD_DIM
    assert n_kb % 2 == 0 and kb_pad <= HEAD_DIM and n_heads % n_hd == 0
    assert q_col % n_hd == 0 and k_col % n_hd == 0 and v_col % n_hd == 0
    groups = n_heads // n_hd
    kernel = functools.partial(_attn_kernel, n_kb=n_kb, kb_pad=kb_pad, n_hd=n_hd,
                               exp_scale=HEAD_DIM ** -0.5 * math.log2(math.e))
    tab = pl.BlockSpec((seq, HEAD_DIM), lambda bg, qi: (0, 0))
    return pl.pallas_call(
        kernel,
        grid=(batch * groups, n_kb),
        in_specs=[
            pl.BlockSpec((blk, wide), lambda bg, qi: ((bg // groups) * n_kb + qi, q_col // n_hd + bg % groups)),
            pl.BlockSpec((seq, wide), lambda bg, qi: (bg // groups, k_col // n_hd + bg % groups)),
            pl.BlockSpec((seq, wide), lambda bg, qi: (bg // groups, v_col // n_hd + bg % groups)),
            tab, tab, tab,
        ],
        out_specs=pl.BlockSpec((blk, wide), lambda bg, qi: ((bg // groups) * n_kb + qi, bg % groups)),
        out_shape=jax.ShapeDtypeStruct((t, n_heads * HEAD_DIM), BF16),
        scratch_shapes=[
            pltpu.VMEM((n_hd, seq, 2 * HEAD_DIM), BF16),
            pltpu.VMEM((n_hd, n_kb, HEAD_DIM, blk), BF16),
            pltpu.VMEM((n_hd, kb_pad, HEAD_DIM), F32),
            pltpu.VMEM((n_hd, 1, blk), F32),
            pltpu.VMEM((n_hd, 1, blk), F32),
            pltpu.VMEM((n_hd, HEAD_DIM, blk), F32),
        ],
        compiler_params=_params("arbitrary", "arbitrary"),
        name="moba_attention",
    )(proj, proj, proj, rope_c, rope_sa, rope_sb)


def _merge_kernel(b_ref, c_ref, h_ref, cp_ref, hp_ref, gc_ref, ga_ref, at_ref, cw_ref, wc_ref, wa_ref,
                  o_ref, *, tiles_per_seq):
    hal = V7X_BF16_SUBLANES
    u = c_ref[...].astype(F32) * h_ref[...].astype(F32)
    up = cp_ref[...].astype(F32) * hp_ref[...].astype(F32)
    up = jnp.where(pl.program_id(0) % tiles_per_seq == 0, 0.0, up)
    row = lax.broadcasted_iota(jnp.int32, u.shape, 0)
    u1 = jnp.where(row == 0, up[hal - 1:hal, :], pltpu.roll(u, 1, 0))
    u2 = jnp.where(row == 0, up[hal - 2:hal - 1, :],
                   jnp.where(row == 1, up[hal - 1:hal, :], pltpu.roll(u, 2, 0)))
    y = cw_ref[0:1, :] * u2 + cw_ref[1:2, :] * u1 + cw_ref[2:3, :] * u
    cm = (b_ref[...].astype(F32) * y).astype(BF16)
    yc = jnp.dot(cm, wc_ref[...], preferred_element_type=F32)
    ya = jnp.dot(at_ref[...], wa_ref[...], preferred_element_type=F32)
    merged = jax.nn.sigmoid(gc_ref[...].astype(F32)) * yc + jax.nn.sigmoid(ga_ref[...].astype(F32)) * ya
    o_ref[...] = merged.astype(o_ref.dtype)


def _merge(proj, attn, conv_w, wc_bf, wa_bf, seq, cw, aw, d, tm):
    t = proj.shape[0]
    hal = V7X_BF16_SUBLANES
    kernel = functools.partial(_merge_kernel, tiles_per_seq=seq // tm)
    col = lambda cidx: pl.BlockSpec((tm, cw), lambda i: (i, cidx))
    halo = lambda cidx: pl.BlockSpec((hal, cw), lambda i: (jnp.maximum(i * (tm // hal) - 1, 0), cidx))
    const = lambda shape: pl.BlockSpec(shape, lambda i: (0, 0))
    return pl.pallas_call(
        kernel,
        grid=(t // tm,),
        in_specs=[col(0), col(1), col(2), halo(1), halo(2), col(6), col(7),
                  pl.BlockSpec((tm, aw), lambda i: (i, 0)),
                  const((CONV_K, cw)), const((cw, d)), const((aw, d))],
        out_specs=pl.BlockSpec((tm, d), lambda i: (i, 0)),
        out_shape=jax.ShapeDtypeStruct((t, d), BF16),
        compiler_params=_params("arbitrary"),
        name="merge_branches",
    )(proj, proj, proj, proj, proj, proj, proj, attn, conv_w, wc_bf, wa_bf)


def _oproj_kernel(mg_ref, x_ref, wo_ref, g_ref, wrt_ref, br_ref,
                  h1_ref, xn_ref, te_ref, tw_ref, rk_ref, cnt_ref, run_ref):
    h1 = x_ref[...] + jnp.dot(mg_ref[...], wo_ref[...], preferred_element_type=F32)
    h1_ref[...] = h1
    ms = jnp.mean(h1 * h1, axis=-1, keepdims=True)
    xn = h1 * lax.rsqrt(ms + NORM_EPS) * g_ref[...]
    xn_ref[...] = xn
    lg = _dot3(wrt_ref[...], xn, NT_DIMS) + br_ref[...]
    n_e, tm = lg.shape
    sub = lax.broadcasted_iota(jnp.int32, lg.shape, 0)
    vals, idxs = [], []
    for _ in range(TOP_K):
        top = jnp.max(lg, axis=0, keepdims=True)
        idx = jnp.min(jnp.where(lg == top, sub, n_e), axis=0, keepdims=True)
        vals.append(top)
        idxs.append(idx)
        lg = jnp.where(sub == idx, -jnp.inf, lg)
    v = jnp.concatenate(vals, axis=0)
    ex = jnp.exp(v - vals[0])
    te_ref[...] = jnp.concatenate(idxs, axis=0)
    tw_ref[...] = ex / jnp.sum(ex, axis=0, keepdims=True)

    @pl.when(pl.program_id(0) == 0)
    def _():
        run_ref[...] = jnp.zeros_like(run_ref)

    upper = jnp.where(lax.broadcasted_iota(jnp.int32, (tm, tm), 0) <= lax.broadcasted_iota(jnp.int32, (tm, tm), 1),
                      1.0, 0.0).astype(BF16)
    run = run_ref[...]
    ranks = []
    for k in range(TOP_K):
        onehot = jnp.where(sub == idxs[k], 1.0, 0.0)
        incl = jnp.dot(onehot.astype(BF16), upper, preferred_element_type=F32)
        ranks.append(jnp.sum(onehot * (incl - 1.0 + run[:, 0:1]), axis=0, keepdims=True))
        run = run + incl[:, tm - 1:tm]
    run_ref[...] = run
    rk_ref[...] = jnp.concatenate(ranks, axis=0).astype(jnp.int32)
    cnt_ref[...] = run.astype(jnp.int32)


def _oproj(merged, x2, wo_bf, g, wr_t, br, tm):
    t, d = x2.shape
    n_e = wr_t.shape[0]
    const = lambda shape: pl.BlockSpec(shape, lambda i: (0, 0))
    row = pl.BlockSpec((tm, d), lambda i: (i, 0))
    top = pl.BlockSpec((TOP_K, tm), lambda i: (0, i))
    return pl.pallas_call(
        _oproj_kernel,
        grid=(t // tm,),
        in_specs=[row, row, const((d, d)), const((1, d)), const((n_e, d)), const((n_e, 1))],
        out_specs=[row, row, top, top, top, const((n_e, V7X_LANES))],
        out_shape=[jax.ShapeDtypeStruct((t, d), F32), jax.ShapeDtypeStruct((t, d), F32),
                   jax.ShapeDtypeStruct((TOP_K, t), jnp.int32), jax.ShapeDtypeStruct((TOP_K, t), F32),
                   jax.ShapeDtypeStruct((TOP_K, t), jnp.int32), jax.ShapeDtypeStruct((n_e, V7X_LANES), jnp.int32)],
        scratch_shapes=[pltpu.VMEM((n_e, V7X_LANES), F32)],
        compiler_params=_params("arbitrary"),
        name="oproj_router",
    )(merged, x2, wo_bf, g.reshape(1, d), wr_t, br.reshape(n_e, 1))


def _dispatch_kernel(dest_ref, fill_ref, x_ref, xs_ref, zero_ref, sem, zsem, *, n_e, n_blocks, tm_moe):
    tm = x_ref.shape[0]

    @pl.when(pl.program_id(0) == 0)
    def _():
        zero_ref[...] = jnp.zeros_like(zero_ref)

        def zcopy(b):
            return pltpu.make_async_copy(
                zero_ref, xs_ref.at[pl.ds(pl.multiple_of(b * tm_moe, tm_moe), tm_moe), :], zsem)

        def zstart(b, carry):
            zcopy(b).start()
            return carry

        def zwait(b, carry):
            zcopy(b).wait()
            return carry

        for e in range(n_e):
            @pl.when(fill_ref[e] >= 0)
            def _():
                zcopy(fill_ref[e]).start()
        lax.fori_loop(fill_ref[n_e], n_blocks, zstart, 0)
        for e in range(n_e):
            @pl.when(fill_ref[e] >= 0)
            def _():
                zcopy(fill_ref[e]).wait()
        lax.fori_loop(fill_ref[n_e], n_blocks, zwait, 0)

    def copy(k, r):
        return pltpu.make_async_copy(x_ref.at[pl.ds(r, 1), :], xs_ref.at[pl.ds(dest_ref[k, r], 1), :], sem)

    _for_each_row_copy(tm, copy)


def _for_each_row_copy(tm, copy):
    group = 8

    def start(g, carry):
        r0 = pl.multiple_of(g * group, group)
        for j in range(group):
            for k in range(TOP_K):
                copy(k, r0 + j).start(priority=(j * TOP_K + k) % 2)
        return carry

    def wait(g, carry):
        r0 = pl.multiple_of(g * group, group)
        for j in range(group):
            for k in range(TOP_K):
                copy(k, r0 + j).wait()
        return carry

    lax.fori_loop(0, tm // group, start, 0)
    lax.fori_loop(0, tm // group, wait, 0)


def _dispatch(dest, fill, xn, n_blocks, tm_moe, tm):
    t, d = xn.shape
    n_e = fill.shape[0] - 1
    kernel = functools.partial(_dispatch_kernel, n_e=n_e, n_blocks=n_blocks, tm_moe=tm_moe)
    return pl.pallas_call(
        kernel,
        grid=(t // tm,),
        in_specs=[pl.BlockSpec((TOP_K, tm), lambda i: (0, i), memory_space=pltpu.SMEM),
                  pl.BlockSpec(memory_space=pltpu.SMEM),
                  pl.BlockSpec((tm, d), lambda i: (i, 0))],
        out_specs=pl.BlockSpec(memory_space=pl.ANY),
        out_shape=jax.ShapeDtypeStruct((n_blocks * tm_moe, d), F32),
        scratch_shapes=[pltpu.VMEM((tm_moe, d), F32), pltpu.SemaphoreType.DMA(()), pltpu.SemaphoreType.DMA(())],
        compiler_params=_params("arbitrary"),
        name="moe_dispatch",
    )(dest, fill, xn)


N_ITEM_ARRAYS = 11


def _expert_items(blk_start, blk_count, n_col, n_blocks):
    n_e = blk_count.shape[0]
    n_items = n_blocks * n_col
    item_end = jnp.cumsum(blk_count * n_col)
    item_start = item_end - blk_count * n_col
    n_used = item_end[-1]
    blocks_used = jnp.sum(blk_count)
    i = jnp.arange(n_items, dtype=jnp.int32)
    ok = i < n_used

    def at(idx):
        ic = jnp.clip(idx, 0, jnp.maximum(n_used - 1, 0))
        e = jnp.sum(ic[:, None] >= item_end[None, :], axis=1).astype(jnp.int32)
        local = ic - item_start[e]
        cnt = jnp.maximum(blk_count[e], 1)
        return e, local // cnt, blk_start[e] + local % cnt, cnt

    e, wcol, xblk, cnt = at(i)
    spare = i - n_used
    oblk = jnp.where(ok, xblk, blocks_used + spare // n_col)
    ocol = jnp.where(ok, wcol, spare % n_col)
    new = ok & ((xblk == blk_start[e]))
    nonempty_before = jnp.cumsum(blk_count > 0) - (blk_count > 0)
    slot = (nonempty_before[e] * n_col + wcol) % 2
    nxt = i + cnt
    nxt_e, nxt_col, _, _ = at(nxt)
    has_nxt = new & (nxt < n_used)
    as_i32 = lambda a: a.astype(jnp.int32)
    items = tuple(map(as_i32, (xblk, oblk, wcol, ocol, e, new, ok, slot, nxt_e, nxt_col, has_nxt)))
    assert len(items) == N_ITEM_ARRAYS and n_e == item_end.shape[0]
    return items


def _weight_tile_copy(w_hbm, stage_ref, sem, e, col, slot, which, tn):
    cols = pl.ds(pl.multiple_of(col * tn, tn), tn)
    return pltpu.make_async_copy(w_hbm.at[e, :, cols], stage_ref.at[slot, which], sem.at[slot, which])


def _stream_weight_tiles(i, items, w_hbms, stage_ref, sem, bf_refs):
    _, _, wcol, _, exp, new, _, slot, nxt_e, nxt_col, has_nxt = items
    tn = bf_refs[0].shape[1]

    @pl.when(new[i] == 1)
    def _():
        s = slot[i]

        @pl.when(i == 0)
        def _():
            for which, w in enumerate(w_hbms):
                _weight_tile_copy(w, stage_ref, sem, exp[i], wcol[i], s, which, tn).start()

        for which, w in enumerate(w_hbms):
            _weight_tile_copy(w, stage_ref, sem, exp[i], wcol[i], s, which, tn).wait()

        @pl.when(has_nxt[i] == 1)
        def _():
            for which, w in enumerate(w_hbms):
                _weight_tile_copy(w, stage_ref, sem, nxt_e[i], nxt_col[i], 1 - s, which, tn).start()

        for which, bf in enumerate(bf_refs):
            bf[...] = stage_ref[s, which].astype(BF16)


def _up_kernel(*refs):
    items = refs[:N_ITEM_ARRAYS]
    x_ref, wg_hbm, wu_hbm, bg_ref, bu_ref, o_ref, stage_ref, wgb_ref, wub_ref, sem = refs[N_ITEM_ARRAYS:]
    ok = items[6]
    i = pl.program_id(0)
    _stream_weight_tiles(i, items, (wg_hbm, wu_hbm), stage_ref, sem, (wgb_ref, wub_ref))

    @pl.when(ok[i] == 1)
    def _():
        x = x_ref[...].astype(BF16)
        g = jnp.dot(x, wgb_ref[...], preferred_element_type=F32) + bg_ref[0]
        u = jnp.dot(x, wub_ref[...], preferred_element_type=F32) + bu_ref[0]
        g = jnp.minimum(g, SWIGLU_LIMIT)
        u = jnp.clip(u, -SWIGLU_LIMIT, SWIGLU_LIMIT)
        o_ref[...] = ((u + 1.0) * (g * jax.nn.sigmoid(SWIGLU_ALPHA * g))).astype(o_ref.dtype)

    @pl.when(ok[i] == 0)
    def _():
        o_ref[...] = jnp.zeros_like(o_ref)


def _down_kernel(*refs):
    items = refs[:N_ITEM_ARRAYS]
    h_ref, wd_hbm, bd_ref, o_ref, stage_ref, wdb_ref, sem = refs[N_ITEM_ARRAYS:]
    ok = items[6]
    i = pl.program_id(0)
    _stream_weight_tiles(i, items, (wd_hbm,), stage_ref, sem, (wdb_ref,))

    @pl.when(ok[i] == 1)
    def _():
        o_ref[...] = jnp.dot(h_ref[...], wdb_ref[...], preferred_element_type=F32) + bd_ref[0]

    @pl.when(ok[i] == 0)
    def _():
        o_ref[...] = jnp.zeros_like(o_ref)


def _item_map(fn):
    return lambda i, *items: fn(i, *items[:5])


def _expert_up(items, xs, w_gate, b_gate, w_up, b_up, tm, tf):
    rows, d = xs.shape
    n_e, _, f = w_gate.shape
    n_items = items[0].shape[0]
    bspec = pl.BlockSpec((1, 1, tf), _item_map(lambda i, xb, ob, wc, oc, e: (e[i], 0, wc[i])))
    hbm = pl.BlockSpec(memory_space=pl.ANY)
    grid_spec = pltpu.PrefetchScalarGridSpec(
        num_scalar_prefetch=N_ITEM_ARRAYS,
        grid=(n_items,),
        in_specs=[pl.BlockSpec((tm, d), _item_map(lambda i, xb, ob, wc, oc, e: (xb[i], 0))),
                  hbm, hbm, bspec, bspec],
        out_specs=pl.BlockSpec((tm, tf), _item_map(lambda i, xb, ob, wc, oc, e: (ob[i], oc[i]))),
        scratch_shapes=[pltpu.VMEM((2, 2, d, tf), F32), pltpu.VMEM((d, tf), BF16), pltpu.VMEM((d, tf), BF16),
                        pltpu.SemaphoreType.DMA((2, 2))],
    )
    return pl.pallas_call(
        _up_kernel,
        grid_spec=grid_spec,
        out_shape=jax.ShapeDtypeStruct((rows, f), BF16),
        compiler_params=_params("arbitrary"),
        name="moe_up",
    )(*items, xs, w_gate, w_up, b_gate.reshape(n_e, 1, f), b_up.reshape(n_e, 1, f))


def _expert_down(items, hid, w_down, b_down, tm, tn):
    rows, f = hid.shape
    n_e, _, d = w_down.shape
    n_items = items[0].shape[0]
    grid_spec = pltpu.PrefetchScalarGridSpec(
        num_scalar_prefetch=N_ITEM_ARRAYS,
        grid=(n_items,),
        in_specs=[pl.BlockSpec((tm, f), _item_map(lambda i, xb, ob, wc, oc, e: (xb[i], 0))),
                  pl.BlockSpec(memory_space=pl.ANY),
                  pl.BlockSpec((1, 1, tn), _item_map(lambda i, xb, ob, wc, oc, e: (e[i], 0, wc[i])))],
        out_specs=pl.BlockSpec((tm, tn), _item_map(lambda i, xb, ob, wc, oc, e: (ob[i], oc[i]))),
        scratch_shapes=[pltpu.VMEM((2, 1, f, tn), F32), pltpu.VMEM((f, tn), BF16),
                        pltpu.SemaphoreType.DMA((2, 1))],
    )
    return pl.pallas_call(
        _down_kernel,
        grid_spec=grid_spec,
        out_shape=jax.ShapeDtypeStruct((rows, d), F32),
        compiler_params=_params("arbitrary"),
        name="moe_down",
    )(*items, hid, w_down, b_down.reshape(n_e, 1, d))


def _combine_kernel(dest_ref, h1_ref, w_ref, g_ref, ys_ref, o_ref, buf_ref, sem):
    tm = h1_ref.shape[0]

    def copy(k, r):
        return pltpu.make_async_copy(ys_ref.at[pl.ds(dest_ref[k, r], 1), :], buf_ref.at[k, pl.ds(r, 1), :], sem)

    _for_each_row_copy(tm, copy)
    h2 = h1_ref[...]
    for k in range(TOP_K):
        h2 = h2 + w_ref[:, k:k + 1] * buf_ref[k]
    ms = jnp.mean(h2 * h2, axis=-1, keepdims=True)
    o_ref[...] = h2 * lax.rsqrt(ms + NORM_EPS) * g_ref[...]


def _combine(dest, h1, w_tk, g, ys, tm):
    t, d = h1.shape
    return pl.pallas_call(
        _combine_kernel,
        grid=(t // tm,),
        in_specs=[pl.BlockSpec((TOP_K, tm), lambda i: (0, i), memory_space=pltpu.SMEM),
                  pl.BlockSpec((tm, d), lambda i: (i, 0)),
                  pl.BlockSpec((tm, TOP_K), lambda i: (i, 0)),
                  pl.BlockSpec((1, d), lambda i: (0, 0)),
                  pl.BlockSpec(memory_space=pl.ANY)],
        out_specs=pl.BlockSpec((tm, d), lambda i: (i, 0)),
        out_shape=jax.ShapeDtypeStruct((t, d), F32),
        scratch_shapes=[pltpu.VMEM((TOP_K, tm, d), F32), pltpu.SemaphoreType.DMA(())],
        compiler_params=_params("arbitrary"),
        name="moe_combine",
    )(dest, h1, w_tk, g.reshape(1, d), ys)


def _rope_tables(seq):
    half = ROPE_DIM // 2
    inv_freq = jnp.exp(-math.log(ROPE_THETA) * jnp.arange(half, dtype=F32) / half)
    ang = jnp.arange(seq, dtype=jnp.int32).astype(F32)[:, None] * inv_freq[None, :]
    cos, sin = jnp.cos(ang), jnp.sin(ang)
    zeros = jnp.zeros((seq, HEAD_DIM - ROPE_DIM), F32)
    zh = jnp.zeros((seq, half), F32)
    c = jnp.concatenate([cos, cos, jnp.ones_like(zeros)], axis=1)
    sa = jnp.concatenate([-sin, zh, zeros], axis=1)
    sb = jnp.concatenate([zh, sin, zeros], axis=1)
    return c, sa, sb


def _dest_kernel(base_ref, te_ref, rk_ref, o_ref):
    te = te_ref[...]
    rows = rk_ref[...]
    for e in range(base_ref.shape[0]):
        rows = rows + jnp.where(te == e, base_ref[e], 0)
    o_ref[...] = rows


def _dest_rows(base, top_e, rank, tl):
    k, t = top_e.shape
    blk = pl.BlockSpec((k, tl), lambda i: (0, i))
    return pl.pallas_call(
        _dest_kernel,
        grid=(t // tl,),
        in_specs=[pl.BlockSpec(memory_space=pltpu.SMEM), blk, blk],
        out_specs=blk,
        out_shape=jax.ShapeDtypeStruct((k, t), jnp.int32),
        compiler_params=_params("arbitrary"),
        name="moe_dest_rows",
    )(base, top_e, rank)


def _routing_tables(counts, tm):
    blk_count = (counts + tm - 1) // tm
    blk_end = jnp.cumsum(blk_count)
    blk_start = blk_end - blk_count
    fill = jnp.concatenate([jnp.where(blk_count > 0, blk_end - 1, -1), blk_end[-1:]])
    as_i32 = lambda a: a.astype(jnp.int32)
    return as_i32(blk_start), as_i32(blk_count), as_i32(blk_start * tm), as_i32(fill)


def kernel(x, norm_mix, w_in, conv_w, w_conv_out, w_attn_out, w_o, norm_ffn, w_router, b_router,
           w_gate, b_gate, w_up, b_up, w_down, b_down, norm_final):
    batch, seq, d = x.shape
    cw = w_conv_out.shape[1]
    aw = w_attn_out.shape[1]
    n_heads = aw // HEAD_DIM
    n_e = w_router.shape[-1]
    f = w_gate.shape[-1]
    t = batch * seq
    assert w_in.shape[0] == 1, "single layer: the final rmsnorm is fused into the MoE combine"
    assert cw == d and aw == d and seq % MOBA_BLOCK == 0

    tm_norm = min(512, t)
    tm_in, tn_in = min(1024, t), min(1024, w_in.shape[-1])
    tm_mid = min(256, seq)
    tm_moe = min(256, t)
    tf_up = min(1024, f)
    tn_down = min(2048, d)
    tm_disp = min(128, t)
    tm_comb = min(128, t)
    n_blocks = (t * TOP_K + n_e * (tm_moe - 1)) // tm_moe
    col = lambda width_off: width_off // HEAD_DIM

    rope_c, rope_sa, rope_sb = _rope_tables(seq)
    h = x.reshape(t, d)
    xn = _rmsnorm(h, norm_mix[0], tm_norm, BF16)
    proj = _inproj(xn, w_in[0], tm_in, tn_in)
    attn = _attention(proj, rope_c, rope_sa, rope_sb, batch, seq, n_heads,
                      col(3 * cw), col(3 * cw + aw), col(3 * cw + 2 * aw))
    merged = _merge(proj, attn, conv_w[0], w_conv_out[0].astype(BF16), w_attn_out[0].astype(BF16),
                    seq, cw, aw, d, tm_mid)
    h1, xn2, top_e, top_w, rank, counts = _oproj(merged, h, w_o[0].astype(BF16), norm_ffn[0],
                                                 w_router[0].T, b_router[0], tm_mid)
    blk_start, blk_count, base, fill = _routing_tables(counts[:, 0], tm_moe)
    dest = _dest_rows(base, top_e, rank, min(2048, t))
    xs = _dispatch(dest, fill, xn2, n_blocks, tm_moe, tm_disp)
    up_items = _expert_items(blk_start, blk_count, f // tf_up, n_blocks)
    hid = _expert_up(up_items, xs, w_gate[0], b_gate[0], w_up[0], b_up[0], tm_moe, tf_up)
    same_tiling = f // tf_up == d // tn_down
    down_items = up_items if same_tiling else _expert_items(blk_start, blk_count, d // tn_down, n_blocks)
    ys = _expert_down(down_items, hid, w_down[0], b_down[0], tm_moe, tn_down)
    out = _combine(dest, h1, top_w.T, norm_final, ys, tm_comb)
    return out.reshape(batch, seq, d)
```

```python
import functools
import math

import jax
import jax.numpy as jnp
from jax import lax
from jax.experimental import pallas as pl
from jax.experimental.pallas import tpu as pltpu

HEAD_DIM = 128
ROPE_DIM = HEAD_DIM // 4
ROPE_THETA = 500000.0
MOBA_BLOCK = 256
MOBA_TOPK = 3
CONV_K = 3
TOP_K = 4
SWIGLU_LIMIT = 7.0
SWIGLU_ALPHA = 1.702
NORM_EPS = 1e-5
NEG = -1e30
ATTN_HEADS_PER_STEP = 4

V7X_LANES = 128
V7X_BF16_SUBLANES = 16
V7X_VMEM_BYTES = 64 * 1024 * 1024
VMEM_LIMIT = V7X_VMEM_BYTES - 8 * 1024 * 1024

F32 = jnp.float32
BF16 = jnp.bfloat16
NT_DIMS = (((1,), (1,)), ((), ()))
NN_DIMS = (((1,), (0,)), ((), ()))


def _params(*sem):
    return pltpu.CompilerParams(dimension_semantics=sem, vmem_limit_bytes=VMEM_LIMIT)


def _split_bf16(a):
    hi = a.astype(BF16)
    return hi, (a - hi.astype(F32)).astype(BF16)


def _dot3(a, b, dims):
    ah, al = _split_bf16(a)
    bh, bl = _split_bf16(b)
    dot = lambda u, v: lax.dot_general(u, v, dims, preferred_element_type=F32)
    return dot(ah, bh) + (dot(al, bh) + dot(ah, bl))


def _rmsnorm_kernel(x_ref, g_ref, o_ref):
    x = x_ref[...]
    ms = jnp.mean(x * x, axis=-1, keepdims=True)
    o_ref[...] = (x * lax.rsqrt(ms + NORM_EPS) * g_ref[...]).astype(o_ref.dtype)


def _rmsnorm(x, g, tm, out_dtype):
    t, d = x.shape
    return pl.pallas_call(
        _rmsnorm_kernel,
        grid=(t // tm,),
        in_specs=[pl.BlockSpec((tm, d), lambda i: (i, 0)), pl.BlockSpec((1, d), lambda i: (0, 0))],
        out_specs=pl.BlockSpec((tm, d), lambda i: (i, 0)),
        out_shape=jax.ShapeDtypeStruct((t, d), out_dtype),
        compiler_params=_params("arbitrary"),
        name="rmsnorm",
    )(x, g.reshape(1, d))


def _inproj_kernel(x_ref, w_ref, o_ref, wbf_ref):
    @pl.when(pl.program_id(1) == 0)
    def _():
        wbf_ref[...] = w_ref[...].astype(BF16)

    o_ref[...] = jnp.dot(x_ref[...], wbf_ref[...], preferred_element_type=F32).astype(o_ref.dtype)


def _inproj(xn, w, tm, tn):
    t, d = xn.shape
    n = w.shape[1]
    return pl.pallas_call(
        _inproj_kernel,
        grid=(n // tn, t // tm),
        in_specs=[pl.BlockSpec((tm, d), lambda j, i: (i, 0)), pl.BlockSpec((d, tn), lambda j, i: (0, j))],
        out_specs=pl.BlockSpec((tm, tn), lambda j, i: (i, j)),
        out_shape=jax.ShapeDtypeStruct((t, n), BF16),
        scratch_shapes=[pltpu.VMEM((d, tn), BF16)],
        compiler_params=_params("arbitrary", "arbitrary"),
        name="inproj",
    )(xn, w)


def _rope(t, c, sa, sb):
    half = ROPE_DIM // 2
    return t * c + pltpu.roll(t, HEAD_DIM - half, 1) * sa + pltpu.roll(t, half, 1) * sb


def _attn_kernel(q_ref, k_ref, v_ref, c_ref, sa_ref, sb_ref, o_ref,
                 kaug_ref, vt_ref, kmean_ref, m_ref, l_ref, acc_ref, qaug_ref, s0_ref, s1_ref,
                 *, n_kb, kb_pad, n_hd, exp_scale):
    blk = MOBA_BLOCK
    qi = pl.program_id(1)
    heads = range(n_hd)
    hcols = lambda hd: slice(hd * HEAD_DIM, (hd + 1) * HEAD_DIM)

    @pl.when(qi == 0)
    def _prep():
        kmean_ref[...] = jnp.zeros_like(kmean_ref)
        qaug_ref[...] = jnp.zeros_like(qaug_ref)
        lane = lax.broadcasted_iota(jnp.int32, (blk, HEAD_DIM), 1)

        def body(j, carry):
            rows = pl.ds(pl.multiple_of(j * blk, blk), blk)
            onehot = jnp.where(lane == j, 1.0, 0.0).astype(BF16)
            for hd in heads:
                kr = _rope(k_ref[rows, hcols(hd)].astype(F32), c_ref[rows, :], sa_ref[rows, :], sb_ref[rows, :])
                kaug_ref[hd, rows, 0:HEAD_DIM] = kr.astype(BF16)
                kaug_ref[hd, rows, HEAD_DIM:2 * HEAD_DIM] = onehot
                kmean_ref[hd, pl.ds(j, 1), :] = jnp.mean(kr, axis=0, keepdims=True)
                vt_ref[hd, j] = v_ref[rows, hcols(hd)].astype(F32).T.astype(BF16)
            return carry

        lax.fori_loop(0, n_kb, body, 0)

    rows = pl.ds(pl.multiple_of(qi * blk, blk), blk)
    sub = lax.broadcasted_iota(jnp.int32, (kb_pad, blk), 0)
    valid = sub < qi
    key_i = lax.broadcasted_iota(jnp.int32, (blk, blk), 0)
    qry_i = lax.broadcasted_iota(jnp.int32, (blk, blk), 1)
    own_scores = []
    for hd in heads:
        qr = _rope(q_ref[:, hcols(hd)].astype(F32), c_ref[rows, :], sa_ref[rows, :], sb_ref[rows, :])
        qrt = qr.T
        qrt_bf = (qrt * exp_scale).astype(BF16)
        qaug_ref[hd, 0:HEAD_DIM, :] = qrt_bf
        own_scores.append(jnp.dot(kaug_ref[hd, rows, 0:HEAD_DIM], qrt_bf, preferred_element_type=F32))

        gate = _dot3(kmean_ref[hd], qrt, NN_DIMS)
        g = jnp.where(valid, gate, -jnp.inf)
        bias = jnp.full((kb_pad, blk), NEG, F32)
        for _ in range(MOBA_TOPK):
            top = jnp.max(g, axis=0, keepdims=True)
            idx = jnp.min(jnp.where(g == top, sub, kb_pad), axis=0, keepdims=True)
            pick = sub == idx
            bias = jnp.where(pick, jnp.where(valid, 0.0, bias), bias)
            g = jnp.where(pick, -jnp.inf, g)
        qaug_ref[hd, HEAD_DIM:HEAD_DIM + kb_pad, :] = bias.astype(BF16)

    def scores(trip, s_ref):
        keys = pl.ds(pl.multiple_of(trip * 2 * blk, 2 * blk), 2 * blk)
        for hd in heads:
            s_ref[hd] = jnp.dot(kaug_ref[hd, keys, :], qaug_ref[hd], preferred_element_type=F32)

    def consume(trip, s_ref):
        j0 = 2 * trip
        pbs, alphas = [], []
        for hd in heads:
            sj = s_ref[hd]
            m_prev = m_ref[hd]
            m_new = jnp.maximum(m_prev, jnp.max(sj, axis=0, keepdims=True))
            alpha = jnp.exp2(m_prev - m_new)
            pj = jnp.exp2(sj - m_new)
            l_ref[hd] = alpha * l_ref[hd] + jnp.sum(pj, axis=0, keepdims=True)
            m_ref[hd] = m_new
            pbs.append(pj.astype(BF16))
            alphas.append(alpha)
        for hd in heads:
            pv = (jnp.dot(vt_ref[hd, j0], pbs[hd][0:blk], preferred_element_type=F32)
                  + jnp.dot(vt_ref[hd, j0 + 1], pbs[hd][blk:2 * blk], preferred_element_type=F32))
            acc_ref[hd] = alphas[hd] * acc_ref[hd] + pv

    def trip_ahead(trip, cur_ref, nxt_ref):
        scores(trip + 1, nxt_ref)
        consume(trip, cur_ref)

    scores(0, s0_ref)
    own_ps = []
    for hd in heads:
        s = jnp.where(key_i <= qry_i, own_scores[hd], NEG)
        m0 = jnp.max(s, axis=0, keepdims=True)
        p = jnp.exp2(s - m0)
        m_ref[hd] = m0
        l_ref[hd] = jnp.sum(p, axis=0, keepdims=True)
        own_ps.append(p.astype(BF16))
    for hd in heads:
        acc_ref[hd] = jnp.dot(vt_ref[hd, qi], own_ps[hd], preferred_element_type=F32)

    n_trips = (qi + 1) // 2

    @pl.when(n_trips > 0)
    def _():
        def two_trips(r, carry):
            trip_ahead(2 * r, s0_ref, s1_ref)
            trip_ahead(2 * r + 1, s1_ref, s0_ref)
            return carry

        lax.fori_loop(0, (n_trips - 1) // 2, two_trips, 0)
        last = n_trips - 1

        @pl.when(n_trips % 2 == 1)
        def _():
            consume(last, s0_ref)

        @pl.when(n_trips % 2 == 0)
        def _():
            trip_ahead(last - 1, s0_ref, s1_ref)
            consume(last, s1_ref)

    for hd in heads:
        o_ref[:, hcols(hd)] = (acc_ref[hd] / l_ref[hd]).T.astype(o_ref.dtype)


def _attention(proj, rope_c, rope_sa, rope_sb, batch, seq, n_heads, q_col, k_col, v_col):
    t = proj.shape[0]
    blk = MOBA_BLOCK
    n_kb = seq // blk
    kb_pad = -(-n_kb // V7X_BF16_SUBLANES) * V7X_BF16_SUBLANES
    n_hd = ATTN_HEADS_PER_STEP
    wide = n_hd * HEAD_DIM
    assert n_kb % 2 == 0 and kb_pad <= HEAD_DIM and n_heads % n_hd == 0
    assert q_col % n_hd == 0 and k_col % n_hd == 0 and v_col % n_hd == 0
    groups = n_heads // n_hd
    kernel = functools.partial(_attn_kernel, n_kb=n_kb, kb_pad=kb_pad, n_hd=n_hd,
                               exp_scale=HEAD_DIM ** -0.5 * math.log2(math.e))
    tab = pl.BlockSpec((seq, HEAD_DIM), lambda bg, qi: (0, 0))
    return pl.pallas_call(
        kernel,
        grid=(batch * groups, n_kb),
        in_specs=[
            pl.BlockSpec((blk, wide), lambda bg, qi: ((bg // groups) * n_kb + qi, q_col // n_hd + bg % groups)),
            pl.BlockSpec((seq, wide), lambda bg, qi: (bg // groups, k_col // n_hd + bg % groups)),
            pl.BlockSpec((seq, wide), lambda bg, qi: (bg // groups, v_col // n_hd + bg % groups)),
            tab, tab, tab,
        ],
        out_specs=pl.BlockSpec((blk, wide), lambda bg, qi: ((bg // groups) * n_kb + qi, bg % groups)),
        out_shape=jax.ShapeDtypeStruct((t, n_heads * HEAD_DIM), BF16),
        scratch_shapes=[
            pltpu.VMEM((n_hd, seq, 2 * HEAD_DIM), BF16),
            pltpu.VMEM((n_hd, n_kb, HEAD_DIM, blk), BF16),
            pltpu.VMEM((n_hd, kb_pad, HEAD_DIM), F32),
            pltpu.VMEM((n_hd, 1, blk), F32),
            pltpu.VMEM((n_hd, 1, blk), F32),
            pltpu.VMEM((n_hd, HEAD_DIM, blk), F32),
            pltpu.VMEM((n_hd, 2 * HEAD_DIM, blk), BF16),
            pltpu.VMEM((n_hd, 2 * blk, blk), F32),
            pltpu.VMEM((n_hd, 2 * blk, blk), F32),
        ],
        compiler_params=_params("arbitrary", "arbitrary"),
        name="moba_attention",
    )(proj, proj, proj, rope_c, rope_sa, rope_sb)


def _merge_kernel(b_ref, c_ref, h_ref, cp_ref, hp_ref, gc_ref, ga_ref, at_ref, cw_ref, wc_ref, wa_ref,
                  o_ref, *, tiles_per_seq):
    hal = V7X_BF16_SUBLANES
    u = c_ref[...].astype(F32) * h_ref[...].astype(F32)
    up = cp_ref[...].astype(F32) * hp_ref[...].astype(F32)
    up = jnp.where(pl.program_id(0) % tiles_per_seq == 0, 0.0, up)
    row = lax.broadcasted_iota(jnp.int32, u.shape, 0)
    u1 = jnp.where(row == 0, up[hal - 1:hal, :], pltpu.roll(u, 1, 0))
    u2 = jnp.where(row == 0, up[hal - 2:hal - 1, :],
                   jnp.where(row == 1, up[hal - 1:hal, :], pltpu.roll(u, 2, 0)))
    y = cw_ref[0:1, :] * u2 + cw_ref[1:2, :] * u1 + cw_ref[2:3, :] * u
    cm = (b_ref[...].astype(F32) * y).astype(BF16)
    yc = jnp.dot(cm, wc_ref[...], preferred_element_type=F32)
    ya = jnp.dot(at_ref[...], wa_ref[...], preferred_element_type=F32)
    merged = jax.nn.sigmoid(gc_ref[...].astype(F32)) * yc + jax.nn.sigmoid(ga_ref[...].astype(F32)) * ya
    o_ref[...] = merged.astype(o_ref.dtype)


def _merge(proj, attn, conv_w, wc_bf, wa_bf, seq, cw, aw, d, tm):
    t = proj.shape[0]
    hal = V7X_BF16_SUBLANES
    kernel = functools.partial(_merge_kernel, tiles_per_seq=seq // tm)
    col = lambda cidx: pl.BlockSpec((tm, cw), lambda i: (i, cidx))
    halo = lambda cidx: pl.BlockSpec((hal, cw), lambda i: (jnp.maximum(i * (tm // hal) - 1, 0), cidx))
    const = lambda shape: pl.BlockSpec(shape, lambda i: (0, 0))
    return pl.pallas_call(
        kernel,
        grid=(t // tm,),
        in_specs=[col(0), col(1), col(2), halo(1), halo(2), col(6), col(7),
                  pl.BlockSpec((tm, aw), lambda i: (i, 0)),
                  const((CONV_K, cw)), const((cw, d)), const((aw, d))],
        out_specs=pl.BlockSpec((tm, d), lambda i: (i, 0)),
        out_shape=jax.ShapeDtypeStruct((t, d), BF16),
        compiler_params=_params("arbitrary"),
        name="merge_branches",
    )(proj, proj, proj, proj, proj, proj, proj, attn, conv_w, wc_bf, wa_bf)


def _oproj_kernel(mg_ref, x_ref, wo_ref, g_ref, wrt_ref, br_ref,
                  h1_ref, xn_ref, te_ref, tw_ref, rk_ref, cnt_ref, run_ref):
    h1 = x_ref[...] + jnp.dot(mg_ref[...], wo_ref[...], preferred_element_type=F32)
    h1_ref[...] = h1
    ms = jnp.mean(h1 * h1, axis=-1, keepdims=True)
    xn = h1 * lax.rsqrt(ms + NORM_EPS) * g_ref[...]
    xn_ref[...] = xn
    lg = _dot3(wrt_ref[...], xn, NT_DIMS) + br_ref[...]
    n_e, tm = lg.shape
    sub = lax.broadcasted_iota(jnp.int32, lg.shape, 0)
    vals, idxs = [], []
    for _ in range(TOP_K):
        top = jnp.max(lg, axis=0, keepdims=True)
        idx = jnp.min(jnp.where(lg == top, sub, n_e), axis=0, keepdims=True)
        vals.append(top)
        idxs.append(idx)
        lg = jnp.where(sub == idx, -jnp.inf, lg)
    v = jnp.concatenate(vals, axis=0)
    ex = jnp.exp(v - vals[0])
    te_ref[...] = jnp.concatenate(idxs, axis=0)
    tw_ref[...] = ex / jnp.sum(ex, axis=0, keepdims=True)

    @pl.when(pl.program_id(0) == 0)
    def _():
        run_ref[...] = jnp.zeros_like(run_ref)

    upper = jnp.where(lax.broadcasted_iota(jnp.int32, (tm, tm), 0) <= lax.broadcasted_iota(jnp.int32, (tm, tm), 1),
                      1.0, 0.0).astype(BF16)
    run = run_ref[...]
    ranks = []
    for k in range(TOP_K):
        onehot = jnp.where(sub == idxs[k], 1.0, 0.0)
        incl = jnp.dot(onehot.astype(BF16), upper, preferred_element_type=F32)
        ranks.append(jnp.sum(onehot * (incl - 1.0 + run[:, 0:1]), axis=0, keepdims=True))
        run = run + incl[:, tm - 1:tm]
    run_ref[...] = run
    rk_ref[...] = jnp.concatenate(ranks, axis=0).astype(jnp.int32)
    cnt_ref[...] = run.astype(jnp.int32)


def _oproj(merged, x2, wo_bf, g, wr_t, br, tm):
    t, d = x2.shape
    n_e = wr_t.shape[0]
    const = lambda shape: pl.BlockSpec(shape, lambda i: (0, 0))
    row = pl.BlockSpec((tm, d), lambda i: (i, 0))
    top = pl.BlockSpec((TOP_K, tm), lambda i: (0, i))
    return pl.pallas_call(
        _oproj_kernel,
        grid=(t // tm,),
        in_specs=[row, row, const((d, d)), const((1, d)), const((n_e, d)), const((n_e, 1))],
        out_specs=[row, row, top, top, top, const((n_e, V7X_LANES))],
        out_shape=[jax.ShapeDtypeStruct((t, d), F32), jax.ShapeDtypeStruct((t, d), F32),
                   jax.ShapeDtypeStruct((TOP_K, t), jnp.int32), jax.ShapeDtypeStruct((TOP_K, t), F32),
                   jax.ShapeDtypeStruct((TOP_K, t), jnp.int32), jax.ShapeDtypeStruct((n_e, V7X_LANES), jnp.int32)],
        scratch_shapes=[pltpu.VMEM((n_e, V7X_LANES), F32)],
        compiler_params=_params("arbitrary"),
        name="oproj_router",
    )(merged, x2, wo_bf, g.reshape(1, d), wr_t, br.reshape(n_e, 1))


def _dispatch_kernel(dest_ref, fill_ref, x_ref, xs_ref, zero_ref, sem, zsem, *, n_e, n_blocks, tm_moe):
    tm = x_ref.shape[0]

    @pl.when(pl.program_id(0) == 0)
    def _():
        zero_ref[...] = jnp.zeros_like(zero_ref)

        def zcopy(b):
            return pltpu.make_async_copy(
                zero_ref, xs_ref.at[pl.ds(pl.multiple_of(b * tm_moe, tm_moe), tm_moe), :], zsem)

        def zstart(b, carry):
            zcopy(b).start()
            return carry

        def zwait(b, carry):
            zcopy(b).wait()
            return carry

        for e in range(n_e):
            @pl.when(fill_ref[e] >= 0)
            def _():
                zcopy(fill_ref[e]).start()
        lax.fori_loop(fill_ref[n_e], n_blocks, zstart, 0)
        for e in range(n_e):
            @pl.when(fill_ref[e] >= 0)
            def _():
                zcopy(fill_ref[e]).wait()
        lax.fori_loop(fill_ref[n_e], n_blocks, zwait, 0)

    def copy(k, r):
        return pltpu.make_async_copy(x_ref.at[pl.ds(r, 1), :], xs_ref.at[pl.ds(dest_ref[k, r], 1), :], sem)

    _for_each_row_copy(tm, copy)


def _for_each_row_copy(tm, copy):
    group = 8

    def start(g, carry):
        r0 = pl.multiple_of(g * group, group)
        for j in range(group):
            for k in range(TOP_K):
                copy(k, r0 + j).start(priority=(j * TOP_K + k) % 2)
        return carry

    def wait(g, carry):
        r0 = pl.multiple_of(g * group, group)
        for j in range(group):
            for k in range(TOP_K):
                copy(k, r0 + j).wait()
        return carry

    lax.fori_loop(0, tm // group, start, 0)
    lax.fori_loop(0, tm // group, wait, 0)


def _dispatch(dest, fill, xn, n_blocks, tm_moe, tm):
    t, d = xn.shape
    n_e = fill.shape[0] - 1
    kernel = functools.partial(_dispatch_kernel, n_e=n_e, n_blocks=n_blocks, tm_moe=tm_moe)
    return pl.pallas_call(
        kernel,
        grid=(t // tm,),
        in_specs=[pl.BlockSpec((TOP_K, tm), lambda i: (0, i), memory_space=pltpu.SMEM),
                  pl.BlockSpec(memory_space=pltpu.SMEM),
                  pl.BlockSpec((tm, d), lambda i: (i, 0))],
        out_specs=pl.BlockSpec(memory_space=pl.ANY),
        out_shape=jax.ShapeDtypeStruct((n_blocks * tm_moe, d), F32),
        scratch_shapes=[pltpu.VMEM((tm_moe, d), F32), pltpu.SemaphoreType.DMA(()), pltpu.SemaphoreType.DMA(())],
        compiler_params=_params("arbitrary"),
        name="moe_dispatch",
    )(dest, fill, xn)


N_ITEM_ARRAYS = 11


def _expert_items(blk_start, blk_count, n_col, n_blocks):
    n_e = blk_count.shape[0]
    n_items = n_blocks * n_col
    item_end = jnp.cumsum(blk_count * n_col)
    item_start = item_end - blk_count * n_col
    n_used = item_end[-1]
    blocks_used = jnp.sum(blk_count)
    i = jnp.arange(n_items, dtype=jnp.int32)
    ok = i < n_used
    nonempty_before = jnp.cumsum(blk_count > 0) - (blk_count > 0)
    experts = jnp.arange(n_e, dtype=jnp.int32)

    def at(idx):
        ic = jnp.clip(idx, 0, jnp.maximum(n_used - 1, 0))
        e = jnp.sum(ic[:, None] >= item_end[None, :], axis=1).astype(jnp.int32)
        mine = e[:, None] == experts[None, :]
        pick = lambda table: jnp.sum(jnp.where(mine, table[None, :], 0), axis=1)
        local = ic - pick(item_start)
        cnt = jnp.maximum(pick(blk_count), 1)
        first_blk = pick(blk_start)
        return e, local // cnt, first_blk + local % cnt, cnt, first_blk, pick(nonempty_before)

    e, wcol, xblk, cnt, first_blk, groups_before = at(i)
    spare = i - n_used
    oblk = jnp.where(ok, xblk, blocks_used + spare // n_col)
    ocol = jnp.where(ok, wcol, spare % n_col)
    new = ok & (xblk == first_blk)
    slot = (groups_before * n_col + wcol) % 2
    nxt = i + cnt
    nxt_e, nxt_col = at(nxt)[:2]
    has_nxt = new & (nxt < n_used)
    as_i32 = lambda a: a.astype(jnp.int32)
    items = tuple(map(as_i32, (xblk, oblk, wcol, ocol, e, new, ok, slot, nxt_e, nxt_col, has_nxt)))
    assert len(items) == N_ITEM_ARRAYS and n_e == item_end.shape[0]
    return items


def _weight_tile_copy(w_hbm, stage_ref, sem, e, col, slot, which, tn):
    cols = pl.ds(pl.multiple_of(col * tn, tn), tn)
    return pltpu.make_async_copy(w_hbm.at[e, :, cols], stage_ref.at[slot, which], sem.at[slot, which])


def _stream_weight_tiles(i, items, w_hbms, stage_ref, sem, bf_refs):
    _, _, wcol, _, exp, new, _, slot, nxt_e, nxt_col, has_nxt = items
    tn = bf_refs[0].shape[1]

    @pl.when(new[i] == 1)
    def _():
        s = slot[i]

        @pl.when(i == 0)
        def _():
            for which, w in enumerate(w_hbms):
                _weight_tile_copy(w, stage_ref, sem, exp[i], wcol[i], s, which, tn).start()

        for which, w in enumerate(w_hbms):
            _weight_tile_copy(w, stage_ref, sem, exp[i], wcol[i], s, which, tn).wait()

        @pl.when(has_nxt[i] == 1)
        def _():
            for which, w in enumerate(w_hbms):
                _weight_tile_copy(w, stage_ref, sem, nxt_e[i], nxt_col[i], 1 - s, which, tn).start()

        for which, bf in enumerate(bf_refs):
            bf[...] = stage_ref[s, which].astype(BF16)


def _up_kernel(*refs):
    items = refs[:N_ITEM_ARRAYS]
    x_ref, wg_hbm, wu_hbm, bg_ref, bu_ref, o_ref, stage_ref, wgb_ref, wub_ref, sem = refs[N_ITEM_ARRAYS:]
    ok = items[6]
    i = pl.program_id(0)
    _stream_weight_tiles(i, items, (wg_hbm, wu_hbm), stage_ref, sem, (wgb_ref, wub_ref))

    @pl.when(ok[i] == 1)
    def _():
        x = x_ref[...].astype(BF16)
        g = jnp.dot(x, wgb_ref[...], preferred_element_type=F32) + bg_ref[0]
        u = jnp.dot(x, wub_ref[...], preferred_element_type=F32) + bu_ref[0]
        g = jnp.minimum(g, SWIGLU_LIMIT)
        u = jnp.clip(u, -SWIGLU_LIMIT, SWIGLU_LIMIT)
        o_ref[...] = ((u + 1.0) * (g * jax.nn.sigmoid(SWIGLU_ALPHA * g))).astype(o_ref.dtype)

    @pl.when(ok[i] == 0)
    def _():
        o_ref[...] = jnp.zeros_like(o_ref)


def _down_kernel(*refs):
    items = refs[:N_ITEM_ARRAYS]
    h_ref, wd_hbm, bd_ref, o_ref, stage_ref, wdb_ref, sem = refs[N_ITEM_ARRAYS:]
    ok = items[6]
    i = pl.program_id(0)
    _stream_weight_tiles(i, items, (wd_hbm,), stage_ref, sem, (wdb_ref,))

    @pl.when(ok[i] == 1)
    def _():
        o_ref[...] = jnp.dot(h_ref[...], wdb_ref[...], preferred_element_type=F32) + bd_ref[0]

    @pl.when(ok[i] == 0)
    def _():
        o_ref[...] = jnp.zeros_like(o_ref)


def _item_map(fn):
    return lambda i, *items: fn(i, *items[:5])


def _expert_up(items, xs, w_gate, b_gate, w_up, b_up, tm, tf):
    rows, d = xs.shape
    n_e, _, f = w_gate.shape
    n_items = items[0].shape[0]
    bspec = pl.BlockSpec((1, 1, tf), _item_map(lambda i, xb, ob, wc, oc, e: (e[i], 0, wc[i])))
    hbm = pl.BlockSpec(memory_space=pl.ANY)
    grid_spec = pltpu.PrefetchScalarGridSpec(
        num_scalar_prefetch=N_ITEM_ARRAYS,
        grid=(n_items,),
        in_specs=[pl.BlockSpec((tm, d), _item_map(lambda i, xb, ob, wc, oc, e: (xb[i], 0))),
                  hbm, hbm, bspec, bspec],
        out_specs=pl.BlockSpec((tm, tf), _item_map(lambda i, xb, ob, wc, oc, e: (ob[i], oc[i]))),
        scratch_shapes=[pltpu.VMEM((2, 2, d, tf), F32), pltpu.VMEM((d, tf), BF16), pltpu.VMEM((d, tf), BF16),
                        pltpu.SemaphoreType.DMA((2, 2))],
    )
    return pl.pallas_call(
        _up_kernel,
        grid_spec=grid_spec,
        out_shape=jax.ShapeDtypeStruct((rows, f), BF16),
        compiler_params=_params("arbitrary"),
        name="moe_up",
    )(*items, xs, w_gate, w_up, b_gate.reshape(n_e, 1, f), b_up.reshape(n_e, 1, f))


def _expert_down(items, hid, w_down, b_down, tm, tn):
    rows, f = hid.shape
    n_e, _, d = w_down.shape
    n_items = items[0].shape[0]
    grid_spec = pltpu.PrefetchScalarGridSpec(
        num_scalar_prefetch=N_ITEM_ARRAYS,
        grid=(n_items,),
        in_specs=[pl.BlockSpec((tm, f), _item_map(lambda i, xb, ob, wc, oc, e: (xb[i], 0))),
                  pl.BlockSpec(memory_space=pl.ANY),
                  pl.BlockSpec((1, 1, tn), _item_map(lambda i, xb, ob, wc, oc, e: (e[i], 0, wc[i])))],
        out_specs=pl.BlockSpec((tm, tn), _item_map(lambda i, xb, ob, wc, oc, e: (ob[i], oc[i]))),
        scratch_shapes=[pltpu.VMEM((2, 1, f, tn), F32), pltpu.VMEM((f, tn), BF16),
                        pltpu.SemaphoreType.DMA((2, 1))],
    )
    return pl.pallas_call(
        _down_kernel,
        grid_spec=grid_spec,
        out_shape=jax.ShapeDtypeStruct((rows, d), F32),
        compiler_params=_params("arbitrary"),
        name="moe_down",
    )(*items, hid, w_down, b_down.reshape(n_e, 1, d))


def _combine_kernel(dest_ref, h1_ref, w_ref, g_ref, ys_ref, o_ref, buf_ref, sem):
    tm = h1_ref.shape[0]

    def copy(k, r):
        return pltpu.make_async_copy(ys_ref.at[pl.ds(dest_ref[k, r], 1), :], buf_ref.at[k, pl.ds(r, 1), :], sem)

    _for_each_row_copy(tm, copy)
    h2 = h1_ref[...]
    for k in range(TOP_K):
        h2 = h2 + w_ref[:, k:k + 1] * buf_ref[k]
    ms = jnp.mean(h2 * h2, axis=-1, keepdims=True)
    o_ref[...] = h2 * lax.rsqrt(ms + NORM_EPS) * g_ref[...]


def _combine(dest, h1, w_tk, g, ys, tm):
    t, d = h1.shape
    return pl.pallas_call(
        _combine_kernel,
        grid=(t // tm,),
        in_specs=[pl.BlockSpec((TOP_K, tm), lambda i: (0, i), memory_space=pltpu.SMEM),
                  pl.BlockSpec((tm, d), lambda i: (i, 0)),
                  pl.BlockSpec((tm, TOP_K), lambda i: (i, 0)),
                  pl.BlockSpec((1, d), lambda i: (0, 0)),
                  pl.BlockSpec(memory_space=pl.ANY)],
        out_specs=pl.BlockSpec((tm, d), lambda i: (i, 0)),
        out_shape=jax.ShapeDtypeStruct((t, d), F32),
        scratch_shapes=[pltpu.VMEM((TOP_K, tm, d), F32), pltpu.SemaphoreType.DMA(())],
        compiler_params=_params("arbitrary"),
        name="moe_combine",
    )(dest, h1, w_tk, g.reshape(1, d), ys)


def _rope_tables(seq):
    half = ROPE_DIM // 2
    inv_freq = jnp.exp(-math.log(ROPE_THETA) * jnp.arange(half, dtype=F32) / half)
    ang = jnp.arange(seq, dtype=jnp.int32).astype(F32)[:, None] * inv_freq[None, :]
    cos, sin = jnp.cos(ang), jnp.sin(ang)
    zeros = jnp.zeros((seq, HEAD_DIM - ROPE_DIM), F32)
    zh = jnp.zeros((seq, half), F32)
    c = jnp.concatenate([cos, cos, jnp.ones_like(zeros)], axis=1)
    sa = jnp.concatenate([-sin, zh, zeros], axis=1)
    sb = jnp.concatenate([zh, sin, zeros], axis=1)
    return c, sa, sb


def _dest_kernel(base_ref, te_ref, rk_ref, o_ref):
    te = te_ref[...]
    rows = rk_ref[...]
    for e in range(base_ref.shape[0]):
        rows = rows + jnp.where(te == e, base_ref[e], 0)
    o_ref[...] = rows


def _dest_rows(base, top_e, rank, tl):
    k, t = top_e.shape
    blk = pl.BlockSpec((k, tl), lambda i: (0, i))
    return pl.pallas_call(
        _dest_kernel,
        grid=(t // tl,),
        in_specs=[pl.BlockSpec(memory_space=pltpu.SMEM), blk, blk],
        out_specs=blk,
        out_shape=jax.ShapeDtypeStruct((k, t), jnp.int32),
        compiler_params=_params("arbitrary"),
        name="moe_dest_rows",
    )(base, top_e, rank)


def _routing_tables(counts, tm):
    blk_count = (counts + tm - 1) // tm
    blk_end = jnp.cumsum(blk_count)
    blk_start = blk_end - blk_count
    fill = jnp.concatenate([jnp.where(blk_count > 0, blk_end - 1, -1), blk_end[-1:]])
    as_i32 = lambda a: a.astype(jnp.int32)
    return as_i32(blk_start), as_i32(blk_count), as_i32(blk_start * tm), as_i32(fill)


def kernel(x, norm_mix, w_in, conv_w, w_conv_out, w_attn_out, w_o, norm_ffn, w_router, b_router,
           w_gate, b_gate, w_up, b_up, w_down, b_down, norm_final):
    batch, seq, d = x.shape
    cw = w_conv_out.shape[1]
    aw = w_attn_out.shape[1]
    n_heads = aw // HEAD_DIM
    n_e = w_router.shape[-1]
    f = w_gate.shape[-1]
    t = batch * seq
    assert w_in.shape[0] == 1, "single layer: the final rmsnorm is fused into the MoE combine"
    assert cw == d and aw == d and seq % MOBA_BLOCK == 0

    tm_norm = min(512, t)
    tm_in, tn_in = min(1024, t), min(1024, w_in.shape[-1])
    tm_mid = min(256, seq)
    tm_moe = min(256, t)
    tf_up = min(1024, f)
    tn_down = min(2048, d)
    tm_disp = min(128, t)
    tm_comb = min(128, t)
    n_blocks = (t * TOP_K + n_e * (tm_moe - 1)) // tm_moe
    col = lambda width_off: width_off // HEAD_DIM

    rope_c, rope_sa, rope_sb = _rope_tables(seq)
    h = x.reshape(t, d)
    xn = _rmsnorm(h, norm_mix[0], tm_norm, BF16)
    proj = _inproj(xn, w_in[0], tm_in, tn_in)
    attn = _attention(proj, rope_c, rope_sa, rope_sb, batch, seq, n_heads,
                      col(3 * cw), col(3 * cw + aw), col(3 * cw + 2 * aw))
    merged = _merge(proj, attn, conv_w[0], w_conv_out[0].astype(BF16), w_attn_out[0].astype(BF16),
                    seq, cw, aw, d, tm_mid)
    h1, xn2, top_e, top_w, rank, counts = _oproj(merged, h, w_o[0].astype(BF16), norm_ffn[0],
                                                 w_router[0].T, b_router[0], tm_mid)
    blk_start, blk_count, base, fill = _routing_tables(counts[:, 0], tm_moe)
    dest = _dest_rows(base, top_e, rank, min(2048, t))
    xs = _dispatch(dest, fill, xn2, n_blocks, tm_moe, tm_disp)
    up_items = _expert_items(blk_start, blk_count, f // tf_up, n_blocks)
    hid = _expert_up(up_items, xs, w_gate[0], b_gate[0], w_up[0], b_up[0], tm_moe, tf_up)
    same_tiling = f // tf_up == d // tn_down
    down_items = up_items if same_tiling else _expert_items(blk_start, blk_count, d // tn_down, n_blocks)
    ys = _expert_down(down_items, hid, w_down[0], b_down[0], tm_moe, tn_down)
    out = _combine(dest, h1, top_w.T, norm_final, ys, tm_comb)
    return out.reshape(batch, seq, d)
```

```python
import functools
import math

import jax
import jax.numpy as jnp
from jax import lax
from jax.experimental import pallas as pl
from jax.experimental.pallas import tpu as pltpu

HEAD_DIM = 128
ROPE_DIM = HEAD_DIM // 4
ROPE_THETA = 500000.0
MOBA_BLOCK = 256
MOBA_TOPK = 3
CONV_K = 3
TOP_K = 4
SWIGLU_LIMIT = 7.0
SWIGLU_ALPHA = 1.702
NORM_EPS = 1e-5
NEG = -1e30
ATTN_HEADS_PER_STEP = 4

V7X_LANES = 128
V7X_BF16_SUBLANES = 16
V7X_VMEM_BYTES = 64 * 1024 * 1024
VMEM_LIMIT = V7X_VMEM_BYTES - 8 * 1024 * 1024

F32 = jnp.float32
BF16 = jnp.bfloat16
NT_DIMS = (((1,), (1,)), ((), ()))
NN_DIMS = (((1,), (0,)), ((), ()))


def _params(*sem):
    return pltpu.CompilerParams(dimension_semantics=sem, vmem_limit_bytes=VMEM_LIMIT)


def _split_bf16(a):
    hi = a.astype(BF16)
    return hi, (a - hi.astype(F32)).astype(BF16)


def _dot3(a, b, dims):
    ah, al = _split_bf16(a)
    bh, bl = _split_bf16(b)
    dot = lambda u, v: lax.dot_general(u, v, dims, preferred_element_type=F32)
    return dot(ah, bh) + (dot(al, bh) + dot(ah, bl))


def _rmsnorm_kernel(x_ref, g_ref, o_ref):
    x = x_ref[...]
    ms = jnp.mean(x * x, axis=-1, keepdims=True)
    o_ref[...] = (x * lax.rsqrt(ms + NORM_EPS) * g_ref[...]).astype(o_ref.dtype)


def _rmsnorm(x, g, tm, out_dtype):
    t, d = x.shape
    return pl.pallas_call(
        _rmsnorm_kernel,
        grid=(t // tm,),
        in_specs=[pl.BlockSpec((tm, d), lambda i: (i, 0)), pl.BlockSpec((1, d), lambda i: (0, 0))],
        out_specs=pl.BlockSpec((tm, d), lambda i: (i, 0)),
        out_shape=jax.ShapeDtypeStruct((t, d), out_dtype),
        compiler_params=_params("arbitrary"),
        name="rmsnorm",
    )(x, g.reshape(1, d))


def _inproj_kernel(x_ref, w_ref, o_ref, wbf_ref):
    @pl.when(pl.program_id(1) == 0)
    def _():
        wbf_ref[...] = w_ref[...].astype(BF16)

    o_ref[...] = jnp.dot(x_ref[...], wbf_ref[...], preferred_element_type=F32).astype(o_ref.dtype)


def _inproj(xn, w, tm, tn):
    t, d = xn.shape
    n = w.shape[1]
    return pl.pallas_call(
        _inproj_kernel,
        grid=(n // tn, t // tm),
        in_specs=[pl.BlockSpec((tm, d), lambda j, i: (i, 0)), pl.BlockSpec((d, tn), lambda j, i: (0, j))],
        out_specs=pl.BlockSpec((tm, tn), lambda j, i: (i, j)),
        out_shape=jax.ShapeDtypeStruct((t, n), BF16),
        scratch_shapes=[pltpu.VMEM((d, tn), BF16)],
        compiler_params=_params("arbitrary", "arbitrary"),
        name="inproj",
    )(xn, w)


def _rope(t, c, sa, sb):
    half = ROPE_DIM // 2
    return t * c + pltpu.roll(t, HEAD_DIM - half, 1) * sa + pltpu.roll(t, half, 1) * sb


def _attn_kernel(q_ref, k_ref, v_ref, c_ref, sa_ref, sb_ref, o_ref,
                 kaug_ref, vt_ref, kmean_ref, m_ref, l_ref, acc_ref, qaug_ref, s0_ref, s1_ref,
                 *, n_kb, kb_pad, n_hd, exp_scale):
    blk = MOBA_BLOCK
    qi = pl.program_id(1)
    heads = range(n_hd)
    hcols = lambda hd: slice(hd * HEAD_DIM, (hd + 1) * HEAD_DIM)

    @pl.when(qi == 0)
    def _prep():
        kmean_ref[...] = jnp.zeros_like(kmean_ref)
        qaug_ref[...] = jnp.zeros_like(qaug_ref)
        lane = lax.broadcasted_iota(jnp.int32, (blk, HEAD_DIM), 1)

        def body(j, carry):
            rows = pl.ds(pl.multiple_of(j * blk, blk), blk)
            onehot = jnp.where(lane == j, 1.0, 0.0).astype(BF16)
            for hd in heads:
                kr = _rope(k_ref[rows, hcols(hd)].astype(F32), c_ref[rows, :], sa_ref[rows, :], sb_ref[rows, :])
                kaug_ref[hd, rows, 0:HEAD_DIM] = kr.astype(BF16)
                kaug_ref[hd, rows, HEAD_DIM:2 * HEAD_DIM] = onehot
                kmean_ref[hd, pl.ds(j, 1), :] = jnp.mean(kr, axis=0, keepdims=True)
                vt_ref[hd, j] = v_ref[rows, hcols(hd)].astype(F32).T.astype(BF16)
            return carry

        lax.fori_loop(0, n_kb, body, 0)

    rows = pl.ds(pl.multiple_of(qi * blk, blk), blk)
    sub = lax.broadcasted_iota(jnp.int32, (kb_pad, blk), 0)
    valid = sub < qi
    key_i = lax.broadcasted_iota(jnp.int32, (blk, blk), 0)
    qry_i = lax.broadcasted_iota(jnp.int32, (blk, blk), 1)
    own_scores = []
    for hd in heads:
        qr = _rope(q_ref[:, hcols(hd)].astype(F32), c_ref[rows, :], sa_ref[rows, :], sb_ref[rows, :])
        qrt = qr.T
        qrt_bf = (qrt * exp_scale).astype(BF16)
        qaug_ref[hd, 0:HEAD_DIM, :] = qrt_bf
        own_scores.append(jnp.dot(kaug_ref[hd, rows, 0:HEAD_DIM], qrt_bf, preferred_element_type=F32))

        gate = _dot3(kmean_ref[hd], qrt, NN_DIMS)
        g = jnp.where(valid, gate, -jnp.inf)
        bias = jnp.full((kb_pad, blk), NEG, F32)
        for _ in range(MOBA_TOPK):
            top = jnp.max(g, axis=0, keepdims=True)
            idx = jnp.min(jnp.where(g == top, sub, kb_pad), axis=0, keepdims=True)
            pick = sub == idx
            bias = jnp.where(pick, jnp.where(valid, 0.0, bias), bias)
            g = jnp.where(pick, -jnp.inf, g)
        qaug_ref[hd, HEAD_DIM:HEAD_DIM + kb_pad, :] = bias.astype(BF16)

    def scores(trip, s_ref):
        keys = pl.ds(pl.multiple_of(trip * 2 * blk, 2 * blk), 2 * blk)
        for hd in heads:
            s_ref[hd] = jnp.dot(kaug_ref[hd, keys, :], qaug_ref[hd], preferred_element_type=F32)

    def consume(trip, s_ref):
        j0 = 2 * trip
        pbs, alphas = [], []
        for hd in heads:
            sj = s_ref[hd]
            m_prev = m_ref[hd]
            m_new = jnp.maximum(m_prev, jnp.max(sj, axis=0, keepdims=True))
            alpha = jnp.exp2(m_prev - m_new)
            pj = jnp.exp2(sj - m_new)
            l_ref[hd] = alpha * l_ref[hd] + jnp.sum(pj, axis=0, keepdims=True)
            m_ref[hd] = m_new
            pbs.append(pj.astype(BF16))
            alphas.append(alpha)
        for hd in heads:
            pv = (jnp.dot(vt_ref[hd, j0], pbs[hd][0:blk], preferred_element_type=F32)
                  + jnp.dot(vt_ref[hd, j0 + 1], pbs[hd][blk:2 * blk], preferred_element_type=F32))
            acc_ref[hd] = alphas[hd] * acc_ref[hd] + pv

    def trip_ahead(trip, cur_ref, nxt_ref):
        scores(trip + 1, nxt_ref)
        consume(trip, cur_ref)

    scores(0, s0_ref)
    own_ps = []
    for hd in heads:
        s = jnp.where(key_i <= qry_i, own_scores[hd], NEG)
        m0 = jnp.max(s, axis=0, keepdims=True)
        p = jnp.exp2(s - m0)
        m_ref[hd] = m0
        l_ref[hd] = jnp.sum(p, axis=0, keepdims=True)
        own_ps.append(p.astype(BF16))
    for hd in heads:
        acc_ref[hd] = jnp.dot(vt_ref[hd, qi], own_ps[hd], preferred_element_type=F32)

    n_trips = (qi + 1) // 2

    @pl.when(n_trips > 0)
    def _():
        def two_trips(r, carry):
            trip_ahead(2 * r, s0_ref, s1_ref)
            trip_ahead(2 * r + 1, s1_ref, s0_ref)
            return carry

        lax.fori_loop(0, (n_trips - 1) // 2, two_trips, 0)
        last = n_trips - 1

        @pl.when(n_trips % 2 == 1)
        def _():
            consume(last, s0_ref)

        @pl.when(n_trips % 2 == 0)
        def _():
            trip_ahead(last - 1, s0_ref, s1_ref)
            consume(last, s1_ref)

    for hd in heads:
        o_ref[:, hcols(hd)] = (acc_ref[hd] / l_ref[hd]).T.astype(o_ref.dtype)


def _attention(proj, rope_c, rope_sa, rope_sb, batch, seq, n_heads, q_col, k_col, v_col):
    t = proj.shape[0]
    blk = MOBA_BLOCK
    n_kb = seq // blk
    kb_pad = -(-n_kb // V7X_BF16_SUBLANES) * V7X_BF16_SUBLANES
    n_hd = ATTN_HEADS_PER_STEP
    wide = n_hd * HEAD_DIM
    assert n_kb % 2 == 0 and kb_pad <= HEAD_DIM and n_heads % n_hd == 0
    assert q_col % n_hd == 0 and k_col % n_hd == 0 and v_col % n_hd == 0
    groups = n_heads // n_hd
    kernel = functools.partial(_attn_kernel, n_kb=n_kb, kb_pad=kb_pad, n_hd=n_hd,
                               exp_scale=HEAD_DIM ** -0.5 * math.log2(math.e))
    tab = pl.BlockSpec((seq, HEAD_DIM), lambda bg, qi: (0, 0))
    return pl.pallas_call(
        kernel,
        grid=(batch * groups, n_kb),
        in_specs=[
            pl.BlockSpec((blk, wide), lambda bg, qi: ((bg // groups) * n_kb + qi, q_col // n_hd + bg % groups)),
            pl.BlockSpec((seq, wide), lambda bg, qi: (bg // groups, k_col // n_hd + bg % groups)),
            pl.BlockSpec((seq, wide), lambda bg, qi: (bg // groups, v_col // n_hd + bg % groups)),
            tab, tab, tab,
        ],
        out_specs=pl.BlockSpec((blk, wide), lambda bg, qi: ((bg // groups) * n_kb + qi, bg % groups)),
        out_shape=jax.ShapeDtypeStruct((t, n_heads * HEAD_DIM), BF16),
        scratch_shapes=[
            pltpu.VMEM((n_hd, seq, 2 * HEAD_DIM), BF16),
            pltpu.VMEM((n_hd, n_kb, HEAD_DIM, blk), BF16),
            pltpu.VMEM((n_hd, kb_pad, HEAD_DIM), F32),
            pltpu.VMEM((n_hd, 1, blk), F32),
            pltpu.VMEM((n_hd, 1, blk), F32),
            pltpu.VMEM((n_hd, HEAD_DIM, blk), F32),
            pltpu.VMEM((n_hd, 2 * HEAD_DIM, blk), BF16),
            pltpu.VMEM((n_hd, 2 * blk, blk), F32),
            pltpu.VMEM((n_hd, 2 * blk, blk), F32),
        ],
        compiler_params=_params("arbitrary", "arbitrary"),
        name="moba_attention",
    )(proj, proj, proj, rope_c, rope_sa, rope_sb)


def _merge_kernel(b_ref, c_ref, h_ref, cp_ref, hp_ref, gc_ref, ga_ref, at_ref, cw_ref, wc_ref, wa_ref,
                  o_ref, *, tiles_per_seq):
    hal = V7X_BF16_SUBLANES
    u = c_ref[...].astype(F32) * h_ref[...].astype(F32)
    up = cp_ref[...].astype(F32) * hp_ref[...].astype(F32)
    up = jnp.where(pl.program_id(0) % tiles_per_seq == 0, 0.0, up)
    row = lax.broadcasted_iota(jnp.int32, u.shape, 0)
    u1 = jnp.where(row == 0, up[hal - 1:hal, :], pltpu.roll(u, 1, 0))
    u2 = jnp.where(row == 0, up[hal - 2:hal - 1, :],
                   jnp.where(row == 1, up[hal - 1:hal, :], pltpu.roll(u, 2, 0)))
    y = cw_ref[0:1, :] * u2 + cw_ref[1:2, :] * u1 + cw_ref[2:3, :] * u
    cm = (b_ref[...].astype(F32) * y).astype(BF16)
    yc = jnp.dot(cm, wc_ref[...], preferred_element_type=F32)
    ya = jnp.dot(at_ref[...], wa_ref[...], preferred_element_type=F32)
    merged = jax.nn.sigmoid(gc_ref[...].astype(F32)) * yc + jax.nn.sigmoid(ga_ref[...].astype(F32)) * ya
    o_ref[...] = merged.astype(o_ref.dtype)


def _merge(proj, attn, conv_w, wc_bf, wa_bf, seq, cw, aw, d, tm):
    t = proj.shape[0]
    hal = V7X_BF16_SUBLANES
    kernel = functools.partial(_merge_kernel, tiles_per_seq=seq // tm)
    col = lambda cidx: pl.BlockSpec((tm, cw), lambda i: (i, cidx))
    halo = lambda cidx: pl.BlockSpec((hal, cw), lambda i: (jnp.maximum(i * (tm // hal) - 1, 0), cidx))
    const = lambda shape: pl.BlockSpec(shape, lambda i: (0, 0))
    return pl.pallas_call(
        kernel,
        grid=(t // tm,),
        in_specs=[col(0), col(1), col(2), halo(1), halo(2), col(6), col(7),
                  pl.BlockSpec((tm, aw), lambda i: (i, 0)),
                  const((CONV_K, cw)), const((cw, d)), const((aw, d))],
        out_specs=pl.BlockSpec((tm, d), lambda i: (i, 0)),
        out_shape=jax.ShapeDtypeStruct((t, d), BF16),
        compiler_params=_params("arbitrary"),
        name="merge_branches",
    )(proj, proj, proj, proj, proj, proj, proj, attn, conv_w, wc_bf, wa_bf)


def _oproj_kernel(mg_ref, x_ref, wo_ref, g_ref, wrt_ref, br_ref,
                  h1_ref, xn_ref, te_ref, tw_ref, rk_ref, cnt_ref, run_ref):
    h1 = x_ref[...] + jnp.dot(mg_ref[...], wo_ref[...], preferred_element_type=F32)
    h1_ref[...] = h1
    ms = jnp.mean(h1 * h1, axis=-1, keepdims=True)
    xn = h1 * lax.rsqrt(ms + NORM_EPS) * g_ref[...]
    xn_ref[...] = xn
    lg = _dot3(wrt_ref[...], xn, NT_DIMS) + br_ref[...]
    n_e, tm = lg.shape
    sub = lax.broadcasted_iota(jnp.int32, lg.shape, 0)
    vals, idxs = [], []
    for _ in range(TOP_K):
        top = jnp.max(lg, axis=0, keepdims=True)
        idx = jnp.min(jnp.where(lg == top, sub, n_e), axis=0, keepdims=True)
        vals.append(top)
        idxs.append(idx)
        lg = jnp.where(sub == idx, -jnp.inf, lg)
    v = jnp.concatenate(vals, axis=0)
    ex = jnp.exp(v - vals[0])
    te_ref[...] = jnp.concatenate(idxs, axis=0)
    tw_ref[...] = ex / jnp.sum(ex, axis=0, keepdims=True)

    @pl.when(pl.program_id(0) == 0)
    def _():
        run_ref[...] = jnp.zeros_like(run_ref)

    upper = jnp.where(lax.broadcasted_iota(jnp.int32, (tm, tm), 0) <= lax.broadcasted_iota(jnp.int32, (tm, tm), 1),
                      1.0, 0.0).astype(BF16)
    run = run_ref[...]
    ranks = []
    for k in range(TOP_K):
        onehot = jnp.where(sub == idxs[k], 1.0, 0.0)
        incl = jnp.dot(onehot.astype(BF16), upper, preferred_element_type=F32)
        ranks.append(jnp.sum(onehot * (incl - 1.0 + run[:, 0:1]), axis=0, keepdims=True))
        run = run + incl[:, tm - 1:tm]
    run_ref[...] = run
    rk_ref[...] = jnp.concatenate(ranks, axis=0).astype(jnp.int32)
    cnt_ref[...] = run.astype(jnp.int32)


def _oproj(merged, x2, wo_bf, g, wr_t, br, tm):
    t, d = x2.shape
    n_e = wr_t.shape[0]
    const = lambda shape: pl.BlockSpec(shape, lambda i: (0, 0))
    row = pl.BlockSpec((tm, d), lambda i: (i, 0))
    top = pl.BlockSpec((TOP_K, tm), lambda i: (0, i))
    return pl.pallas_call(
        _oproj_kernel,
        grid=(t // tm,),
        in_specs=[row, row, const((d, d)), const((1, d)), const((n_e, d)), const((n_e, 1))],
        out_specs=[row, row, top, top, top, const((n_e, V7X_LANES))],
        out_shape=[jax.ShapeDtypeStruct((t, d), F32), jax.ShapeDtypeStruct((t, d), F32),
                   jax.ShapeDtypeStruct((TOP_K, t), jnp.int32), jax.ShapeDtypeStruct((TOP_K, t), F32),
                   jax.ShapeDtypeStruct((TOP_K, t), jnp.int32), jax.ShapeDtypeStruct((n_e, V7X_LANES), jnp.int32)],
        scratch_shapes=[pltpu.VMEM((n_e, V7X_LANES), F32)],
        compiler_params=_params("arbitrary"),
        name="oproj_router",
    )(merged, x2, wo_bf, g.reshape(1, d), wr_t, br.reshape(n_e, 1))


def _dispatch_kernel(dest_ref, fill_ref, x_ref, xs_ref, zero_ref, sem, zsem, *, n_e, n_blocks, tm_moe, tm):
    @pl.when(pl.program_id(0) == 0)
    def _():
        zero_ref[...] = jnp.zeros_like(zero_ref)

        def zcopy(b):
            return pltpu.make_async_copy(
                zero_ref, xs_ref.at[pl.ds(pl.multiple_of(b * tm_moe, tm_moe), tm_moe), :], zsem)

        def zstart(b, carry):
            zcopy(b).start()
            return carry

        def zwait(b, carry):
            zcopy(b).wait()
            return carry

        for e in range(n_e):
            @pl.when(fill_ref[e] >= 0)
            def _():
                zcopy(fill_ref[e]).start()
        lax.fori_loop(fill_ref[n_e], n_blocks, zstart, 0)
        for e in range(n_e):
            @pl.when(fill_ref[e] >= 0)
            def _():
                zcopy(fill_ref[e]).wait()
        lax.fori_loop(fill_ref[n_e], n_blocks, zwait, 0)

    def copy(k, g, j):
        row = dest_ref[k * tm + g * ROW_GROUP + j]
        return pltpu.make_async_copy(x_ref.at[g, pl.ds(j, 1), :], xs_ref.at[pl.ds(row, 1), :], sem)

    _for_each_row_copy(tm, copy)


ROW_GROUP = 8


def _for_each_row_copy(tm, copy):
    def start(g, carry):
        for j in range(ROW_GROUP):
            for k in range(TOP_K):
                copy(k, g, j).start(priority=(j * TOP_K + k) % 2)
        return carry

    def wait(g, carry):
        for j in range(ROW_GROUP):
            for k in range(TOP_K):
                copy(k, g, j).wait()
        return carry

    lax.fori_loop(0, tm // ROW_GROUP, start, 0)
    lax.fori_loop(0, tm // ROW_GROUP, wait, 0)


def _tile_major(dest, tm):
    k, t = dest.shape
    return dest.reshape(k, t // tm, tm).transpose(1, 0, 2).reshape(-1)


def _dispatch(dest_tiles, fill, xn, n_blocks, tm_moe, tm):
    t, d = xn.shape
    n_e = fill.shape[0] - 1
    kernel = functools.partial(_dispatch_kernel, n_e=n_e, n_blocks=n_blocks, tm_moe=tm_moe, tm=tm)
    return pl.pallas_call(
        kernel,
        grid=(t // tm,),
        in_specs=[pl.BlockSpec((TOP_K * tm,), lambda i: (i,), memory_space=pltpu.SMEM),
                  pl.BlockSpec(memory_space=pltpu.SMEM),
                  pl.BlockSpec((tm // ROW_GROUP, ROW_GROUP, d), lambda i: (i, 0, 0))],
        out_specs=pl.BlockSpec(memory_space=pl.ANY),
        out_shape=jax.ShapeDtypeStruct((n_blocks * tm_moe, d), F32),
        scratch_shapes=[pltpu.VMEM((tm_moe, d), F32), pltpu.SemaphoreType.DMA(()), pltpu.SemaphoreType.DMA(())],
        compiler_params=_params("arbitrary"),
        name="moe_dispatch",
    )(dest_tiles, fill, xn.reshape(t // ROW_GROUP, ROW_GROUP, d))


N_ITEM_ARRAYS = 11


def _expert_items(blk_start, blk_count, n_col, n_blocks):
    n_e = blk_count.shape[0]
    n_items = n_blocks * n_col
    item_end = jnp.cumsum(blk_count * n_col)
    item_start = item_end - blk_count * n_col
    n_used = item_end[-1]
    blocks_used = jnp.sum(blk_count)
    i = jnp.arange(n_items, dtype=jnp.int32)
    ok = i < n_used
    nonempty_before = jnp.cumsum(blk_count > 0) - (blk_count > 0)
    experts = jnp.arange(n_e, dtype=jnp.int32)

    def at(idx):
        ic = jnp.clip(idx, 0, jnp.maximum(n_used - 1, 0))
        e = jnp.sum(ic[:, None] >= item_end[None, :], axis=1).astype(jnp.int32)
        mine = e[:, None] == experts[None, :]
        pick = lambda table: jnp.sum(jnp.where(mine, table[None, :], 0), axis=1)
        local = ic - pick(item_start)
        cnt = jnp.maximum(pick(blk_count), 1)
        first_blk = pick(blk_start)
        return e, local // cnt, first_blk + local % cnt, cnt, first_blk, pick(nonempty_before)

    e, wcol, xblk, cnt, first_blk, groups_before = at(i)
    spare = i - n_used
    oblk = jnp.where(ok, xblk, blocks_used + spare // n_col)
    ocol = jnp.where(ok, wcol, spare % n_col)
    new = ok & (xblk == first_blk)
    slot = (groups_before * n_col + wcol) % 2
    nxt = i + cnt
    nxt_e, nxt_col = at(nxt)[:2]
    has_nxt = new & (nxt < n_used)
    as_i32 = lambda a: a.astype(jnp.int32)
    items = tuple(map(as_i32, (xblk, oblk, wcol, ocol, e, new, ok, slot, nxt_e, nxt_col, has_nxt)))
    assert len(items) == N_ITEM_ARRAYS and n_e == item_end.shape[0]
    return items


def _weight_tile_copy(w_hbm, stage_ref, sem, e, col, slot, which, tn):
    cols = pl.ds(pl.multiple_of(col * tn, tn), tn)
    return pltpu.make_async_copy(w_hbm.at[e, :, cols], stage_ref.at[slot, which], sem.at[slot, which])


def _stream_weight_tiles(i, items, w_hbms, stage_ref, sem, bf_refs):
    _, _, wcol, _, exp, new, _, slot, nxt_e, nxt_col, has_nxt = items
    tn = bf_refs[0].shape[1]

    @pl.when(new[i] == 1)
    def _():
        s = slot[i]

        @pl.when(i == 0)
        def _():
            for which, w in enumerate(w_hbms):
                _weight_tile_copy(w, stage_ref, sem, exp[i], wcol[i], s, which, tn).start(priority=1)

        for which, w in enumerate(w_hbms):
            _weight_tile_copy(w, stage_ref, sem, exp[i], wcol[i], s, which, tn).wait()

        @pl.when(has_nxt[i] == 1)
        def _():
            for which, w in enumerate(w_hbms):
                _weight_tile_copy(w, stage_ref, sem, nxt_e[i], nxt_col[i], 1 - s, which, tn).start(priority=1)

        for which, bf in enumerate(bf_refs):
            bf[...] = stage_ref[s, which].astype(BF16)


def _up_kernel(*refs):
    items = refs[:N_ITEM_ARRAYS]
    x_ref, wg_hbm, wu_hbm, bg_ref, bu_ref, o_ref, stage_ref, wgb_ref, wub_ref, sem = refs[N_ITEM_ARRAYS:]
    ok = items[6]
    i = pl.program_id(0)
    _stream_weight_tiles(i, items, (wg_hbm, wu_hbm), stage_ref, sem, (wgb_ref, wub_ref))

    @pl.when(ok[i] == 1)
    def _():
        x = x_ref[...].astype(BF16)
        g = jnp.dot(x, wgb_ref[...], preferred_element_type=F32) + bg_ref[0]
        u = jnp.dot(x, wub_ref[...], preferred_element_type=F32) + bu_ref[0]
        g = jnp.minimum(g, SWIGLU_LIMIT)
        u = jnp.clip(u, -SWIGLU_LIMIT, SWIGLU_LIMIT)
        o_ref[...] = ((u + 1.0) * (g * jax.nn.sigmoid(SWIGLU_ALPHA * g))).astype(o_ref.dtype)

    @pl.when(ok[i] == 0)
    def _():
        o_ref[...] = jnp.zeros_like(o_ref)


def _down_kernel(*refs):
    items = refs[:N_ITEM_ARRAYS]
    h_ref, wd_hbm, bd_ref, o_ref, stage_ref, wdb_ref, sem = refs[N_ITEM_ARRAYS:]
    ok = items[6]
    i = pl.program_id(0)
    _stream_weight_tiles(i, items, (wd_hbm,), stage_ref, sem, (wdb_ref,))

    @pl.when(ok[i] == 1)
    def _():
        o_ref[...] = jnp.dot(h_ref[...], wdb_ref[...], preferred_element_type=F32) + bd_ref[0]

    @pl.when(ok[i] == 0)
    def _():
        o_ref[...] = jnp.zeros_like(o_ref)


def _item_map(fn):
    return lambda i, *items: fn(i, *items[:5])


def _expert_up(items, xs, w_gate, b_gate, w_up, b_up, tm, tf):
    rows, d = xs.shape
    n_e, _, f = w_gate.shape
    n_items = items[0].shape[0]
    bspec = pl.BlockSpec((1, 1, tf), _item_map(lambda i, xb, ob, wc, oc, e: (e[i], 0, wc[i])))
    hbm = pl.BlockSpec(memory_space=pl.ANY)
    grid_spec = pltpu.PrefetchScalarGridSpec(
        num_scalar_prefetch=N_ITEM_ARRAYS,
        grid=(n_items,),
        in_specs=[pl.BlockSpec((tm, d), _item_map(lambda i, xb, ob, wc, oc, e: (xb[i], 0))),
                  hbm, hbm, bspec, bspec],
        out_specs=pl.BlockSpec((tm, tf), _item_map(lambda i, xb, ob, wc, oc, e: (ob[i], oc[i]))),
        scratch_shapes=[pltpu.VMEM((2, 2, d, tf), F32), pltpu.VMEM((d, tf), BF16), pltpu.VMEM((d, tf), BF16),
                        pltpu.SemaphoreType.DMA((2, 2))],
    )
    return pl.pallas_call(
        _up_kernel,
        grid_spec=grid_spec,
        out_shape=jax.ShapeDtypeStruct((rows, f), BF16),
        compiler_params=_params("arbitrary"),
        name="moe_up",
    )(*items, xs, w_gate, w_up, b_gate.reshape(n_e, 1, f), b_up.reshape(n_e, 1, f))


def _expert_down(items, hid, w_down, b_down, tm, tn):
    rows, f = hid.shape
    n_e, _, d = w_down.shape
    n_items = items[0].shape[0]
    grid_spec = pltpu.PrefetchScalarGridSpec(
        num_scalar_prefetch=N_ITEM_ARRAYS,
        grid=(n_items,),
        in_specs=[pl.BlockSpec((tm, f), _item_map(lambda i, xb, ob, wc, oc, e: (xb[i], 0))),
                  pl.BlockSpec(memory_space=pl.ANY),
                  pl.BlockSpec((1, 1, tn), _item_map(lambda i, xb, ob, wc, oc, e: (e[i], 0, wc[i])))],
        out_specs=pl.BlockSpec((tm, tn), _item_map(lambda i, xb, ob, wc, oc, e: (ob[i], oc[i]))),
        scratch_shapes=[pltpu.VMEM((2, 1, f, tn), F32), pltpu.VMEM((f, tn), BF16),
                        pltpu.SemaphoreType.DMA((2, 1))],
    )
    return pl.pallas_call(
        _down_kernel,
        grid_spec=grid_spec,
        out_shape=jax.ShapeDtypeStruct((rows, d), F32),
        compiler_params=_params("arbitrary"),
        name="moe_down",
    )(*items, hid, w_down, b_down.reshape(n_e, 1, d))


def _combine_kernel(dest_ref, h1_ref, w_ref, g_ref, ys_ref, o_ref, buf_ref, sem):
    tm, d = h1_ref.shape

    def copy(k, g, j):
        row = dest_ref[k * tm + g * ROW_GROUP + j]
        return pltpu.make_async_copy(ys_ref.at[pl.ds(row, 1), :], buf_ref.at[k, g, pl.ds(j, 1), :], sem)

    _for_each_row_copy(tm, copy)
    h2 = h1_ref[...]
    for k in range(TOP_K):
        h2 = h2 + w_ref[:, k:k + 1] * buf_ref[k].reshape(tm, d)
    ms = jnp.mean(h2 * h2, axis=-1, keepdims=True)
    o_ref[...] = h2 * lax.rsqrt(ms + NORM_EPS) * g_ref[...]


def _combine(dest_tiles, h1, w_tk, g, ys, tm):
    t, d = h1.shape
    return pl.pallas_call(
        _combine_kernel,
        grid=(t // tm,),
        in_specs=[pl.BlockSpec((TOP_K * tm,), lambda i: (i,), memory_space=pltpu.SMEM),
                  pl.BlockSpec((tm, d), lambda i: (i, 0)),
                  pl.BlockSpec((tm, TOP_K), lambda i: (i, 0)),
                  pl.BlockSpec((1, d), lambda i: (0, 0)),
                  pl.BlockSpec(memory_space=pl.ANY)],
        out_specs=pl.BlockSpec((tm, d), lambda i: (i, 0)),
        out_shape=jax.ShapeDtypeStruct((t, d), F32),
        scratch_shapes=[pltpu.VMEM((TOP_K, tm // ROW_GROUP, ROW_GROUP, d), F32), pltpu.SemaphoreType.DMA(())],
        compiler_params=_params("arbitrary"),
        name="moe_combine",
    )(dest_tiles, h1, w_tk, g.reshape(1, d), ys)


def _rope_tables(seq):
    half = ROPE_DIM // 2
    inv_freq = jnp.exp(-math.log(ROPE_THETA) * jnp.arange(half, dtype=F32) / half)
    ang = jnp.arange(seq, dtype=jnp.int32).astype(F32)[:, None] * inv_freq[None, :]
    cos, sin = jnp.cos(ang), jnp.sin(ang)
    zeros = jnp.zeros((seq, HEAD_DIM - ROPE_DIM), F32)
    zh = jnp.zeros((seq, half), F32)
    c = jnp.concatenate([cos, cos, jnp.ones_like(zeros)], axis=1)
    sa = jnp.concatenate([-sin, zh, zeros], axis=1)
    sb = jnp.concatenate([zh, sin, zeros], axis=1)
    return c, sa, sb


def _dest_kernel(base_ref, te_ref, rk_ref, o_ref):
    te = te_ref[...]
    rows = rk_ref[...]
    for e in range(base_ref.shape[0]):
        rows = rows + jnp.where(te == e, base_ref[e], 0)
    o_ref[...] = rows


def _dest_rows(base, top_e, rank, tl):
    k, t = top_e.shape
    blk = pl.BlockSpec((k, tl), lambda i: (0, i))
    return pl.pallas_call(
        _dest_kernel,
        grid=(t // tl,),
        in_specs=[pl.BlockSpec(memory_space=pltpu.SMEM), blk, blk],
        out_specs=blk,
        out_shape=jax.ShapeDtypeStruct((k, t), jnp.int32),
        compiler_params=_params("arbitrary"),
        name="moe_dest_rows",
    )(base, top_e, rank)


def _routing_tables(counts, tm):
    blk_count = (counts + tm - 1) // tm
    blk_end = jnp.cumsum(blk_count)
    blk_start = blk_end - blk_count
    fill = jnp.concatenate([jnp.where(blk_count > 0, blk_end - 1, -1), blk_end[-1:]])
    as_i32 = lambda a: a.astype(jnp.int32)
    return as_i32(blk_start), as_i32(blk_count), as_i32(blk_start * tm), as_i32(fill)


def kernel(x, norm_mix, w_in, conv_w, w_conv_out, w_attn_out, w_o, norm_ffn, w_router, b_router,
           w_gate, b_gate, w_up, b_up, w_down, b_down, norm_final):
    batch, seq, d = x.shape
    cw = w_conv_out.shape[1]
    aw = w_attn_out.shape[1]
    n_heads = aw // HEAD_DIM
    n_e = w_router.shape[-1]
    f = w_gate.shape[-1]
    t = batch * seq
    assert w_in.shape[0] == 1, "single layer: the final rmsnorm is fused into the MoE combine"
    assert cw == d and aw == d and seq % MOBA_BLOCK == 0

    tm_norm = min(512, t)
    tm_in, tn_in = min(1024, t), min(1024, w_in.shape[-1])
    tm_mid = min(256, seq)
    tm_moe = min(256, t)
    tf_up = min(1024, f)
    tn_down = min(2048, d)
    tm_disp = min(128, t)
    tm_comb = min(128, t)
    n_blocks = (t * TOP_K + n_e * (tm_moe - 1)) // tm_moe
    col = lambda width_off: width_off // HEAD_DIM

    rope_c, rope_sa, rope_sb = _rope_tables(seq)
    h = x.reshape(t, d)
    xn = _rmsnorm(h, norm_mix[0], tm_norm, BF16)
    proj = _inproj(xn, w_in[0], tm_in, tn_in)
    attn = _attention(proj, rope_c, rope_sa, rope_sb, batch, seq, n_heads,
                      col(3 * cw), col(3 * cw + aw), col(3 * cw + 2 * aw))
    merged = _merge(proj, attn, conv_w[0], w_conv_out[0].astype(BF16), w_attn_out[0].astype(BF16),
                    seq, cw, aw, d, tm_mid)
    h1, xn2, top_e, top_w, rank, counts = _oproj(merged, h, w_o[0].astype(BF16), norm_ffn[0],
                                                 w_router[0].T, b_router[0], tm_mid)
    blk_start, blk_count, base, fill = _routing_tables(counts[:, 0], tm_moe)
    dest = _dest_rows(base, top_e, rank, min(2048, t))
    xs = _dispatch(_tile_major(dest, tm_disp), fill, xn2, n_blocks, tm_moe, tm_disp)
    up_items = _expert_items(blk_start, blk_count, f // tf_up, n_blocks)
    hid = _expert_up(up_items, xs, w_gate[0], b_gate[0], w_up[0], b_up[0], tm_moe, tf_up)
    same_tiling = f // tf_up == d // tn_down
    down_items = up_items if same_tiling else _expert_items(blk_start, blk_count, d // tn_down, n_blocks)
    ys = _expert_down(down_items, hid, w_down[0], b_down[0], tm_moe, tn_down)
    out = _combine(_tile_major(dest, tm_comb), h1, top_w.T, norm_final, ys, tm_comb)
    return out.reshape(batch, seq, d)
```

```python
import functools
import math

import jax
import jax.numpy as jnp
from jax import lax
from jax.experimental import pallas as pl
from jax.experimental.pallas import tpu as pltpu

HEAD_DIM = 128
ROPE_DIM = HEAD_DIM // 4
ROPE_THETA = 500000.0
MOBA_BLOCK = 256
MOBA_TOPK = 3
CONV_K = 3
TOP_K = 4
SWIGLU_LIMIT = 7.0
SWIGLU_ALPHA = 1.702
NORM_EPS = 1e-5
NEG = -1e30
ATTN_HEADS_PER_STEP = 4

V7X_LANES = 128
V7X_BF16_SUBLANES = 16
V7X_VMEM_BYTES = 64 * 1024 * 1024
VMEM_LIMIT = V7X_VMEM_BYTES - 8 * 1024 * 1024

F32 = jnp.float32
BF16 = jnp.bfloat16
NT_DIMS = (((1,), (1,)), ((), ()))
NN_DIMS = (((1,), (0,)), ((), ()))


def _params(*sem):
    return pltpu.CompilerParams(dimension_semantics=sem, vmem_limit_bytes=VMEM_LIMIT)


def _split_bf16(a):
    hi = a.astype(BF16)
    return hi, (a - hi.astype(F32)).astype(BF16)


def _dot3(a, b, dims):
    ah, al = _split_bf16(a)
    bh, bl = _split_bf16(b)
    rows = a.shape[0]
    dot = lambda u, v: lax.dot_general(u, v, dims, preferred_element_type=F32)
    both = dot(jnp.concatenate([ah, al], axis=0), bh)
    return both[:rows] + (both[rows:] + dot(ah, bl))


def _rmsnorm_kernel(x_ref, g_ref, o_ref):
    x = x_ref[...]
    ms = jnp.mean(x * x, axis=-1, keepdims=True)
    o_ref[...] = (x * lax.rsqrt(ms + NORM_EPS) * g_ref[...]).astype(o_ref.dtype)


def _rmsnorm(x, g, tm, out_dtype):
    t, d = x.shape
    return pl.pallas_call(
        _rmsnorm_kernel,
        grid=(t // tm,),
        in_specs=[pl.BlockSpec((tm, d), lambda i: (i, 0)), pl.BlockSpec((1, d), lambda i: (0, 0))],
        out_specs=pl.BlockSpec((tm, d), lambda i: (i, 0)),
        out_shape=jax.ShapeDtypeStruct((t, d), out_dtype),
        compiler_params=_params("arbitrary"),
        name="rmsnorm",
    )(x, g.reshape(1, d))


def _inproj_kernel(x_ref, w_ref, o_ref, wbf_ref):
    @pl.when(pl.program_id(1) == 0)
    def _():
        wbf_ref[...] = w_ref[...].astype(BF16)

    o_ref[...] = jnp.dot(x_ref[...], wbf_ref[...], preferred_element_type=F32).astype(o_ref.dtype)


def _inproj(xn, w, tm, tn):
    t, d = xn.shape
    n = w.shape[1]
    return pl.pallas_call(
        _inproj_kernel,
        grid=(n // tn, t // tm),
        in_specs=[pl.BlockSpec((tm, d), lambda j, i: (i, 0)), pl.BlockSpec((d, tn), lambda j, i: (0, j))],
        out_specs=pl.BlockSpec((tm, tn), lambda j, i: (i, j)),
        out_shape=jax.ShapeDtypeStruct((t, n), BF16),
        scratch_shapes=[pltpu.VMEM((d, tn), BF16)],
        compiler_params=_params("arbitrary", "arbitrary"),
        name="inproj",
    )(xn, w)


def _rope(t, c, sa, sb):
    half = ROPE_DIM // 2
    return t * c + pltpu.roll(t, HEAD_DIM - half, 1) * sa + pltpu.roll(t, half, 1) * sb


def _attn_kernel(q_ref, k_ref, v_ref, c_ref, sa_ref, sb_ref, o_ref,
                 kaug_ref, vt_ref, kmean_ref, m_ref, l_ref, acc_ref, qaug_ref, s0_ref, s1_ref,
                 *, n_kb, kb_pad, n_hd, exp_scale):
    blk = MOBA_BLOCK
    qi = pl.program_id(1)
    heads = range(n_hd)
    hcols = lambda hd: slice(hd * HEAD_DIM, (hd + 1) * HEAD_DIM)

    @pl.when(qi == 0)
    def _prep():
        kmean_ref[...] = jnp.zeros_like(kmean_ref)
        qaug_ref[...] = jnp.zeros_like(qaug_ref)
        lane = lax.broadcasted_iota(jnp.int32, (blk, HEAD_DIM), 1)

        def body(j, carry):
            rows = pl.ds(pl.multiple_of(j * blk, blk), blk)
            onehot = jnp.where(lane == j, 1.0, 0.0).astype(BF16)
            for hd in heads:
                kr = _rope(k_ref[rows, hcols(hd)].astype(F32), c_ref[rows, :], sa_ref[rows, :], sb_ref[rows, :])
                kaug_ref[hd, rows, 0:HEAD_DIM] = kr.astype(BF16)
                kaug_ref[hd, rows, HEAD_DIM:2 * HEAD_DIM] = onehot
                kmean_ref[hd, pl.ds(j, 1), :] = jnp.mean(kr, axis=0, keepdims=True)
                vt_ref[hd, j] = v_ref[rows, hcols(hd)].astype(F32).T.astype(BF16)
            return carry

        lax.fori_loop(0, n_kb, body, 0)

    rows = pl.ds(pl.multiple_of(qi * blk, blk), blk)
    sub = lax.broadcasted_iota(jnp.int32, (kb_pad, blk), 0)
    valid = sub < qi
    key_i = lax.broadcasted_iota(jnp.int32, (blk, blk), 0)
    qry_i = lax.broadcasted_iota(jnp.int32, (blk, blk), 1)
    own_scores = []
    for hd in heads:
        qr = _rope(q_ref[:, hcols(hd)].astype(F32), c_ref[rows, :], sa_ref[rows, :], sb_ref[rows, :])
        qrt = qr.T
        qrt_bf = (qrt * exp_scale).astype(BF16)
        qaug_ref[hd, 0:HEAD_DIM, :] = qrt_bf
        own_scores.append(jnp.dot(kaug_ref[hd, rows, 0:HEAD_DIM], qrt_bf, preferred_element_type=F32))

        gate = _dot3(kmean_ref[hd], qrt, NN_DIMS)
        g = jnp.where(valid, gate, -jnp.inf)
        bias = jnp.full((kb_pad, blk), NEG, F32)
        for _ in range(MOBA_TOPK):
            top = jnp.max(g, axis=0, keepdims=True)
            idx = jnp.min(jnp.where(g == top, sub, kb_pad), axis=0, keepdims=True)
            pick = sub == idx
            bias = jnp.where(pick, jnp.where(valid, 0.0, bias), bias)
            g = jnp.where(pick, -jnp.inf, g)
        qaug_ref[hd, HEAD_DIM:HEAD_DIM + kb_pad, :] = bias.astype(BF16)

    def scores(trip, s_ref):
        keys = pl.ds(pl.multiple_of(trip * 2 * blk, 2 * blk), 2 * blk)
        for hd in heads:
            s_ref[hd] = jnp.dot(kaug_ref[hd, keys, :], qaug_ref[hd], preferred_element_type=F32)

    def consume(trip, s_ref):
        j0 = 2 * trip
        pbs, alphas = [], []
        for hd in heads:
            sj = s_ref[hd]
            m_prev = m_ref[hd]
            m_new = jnp.maximum(m_prev, jnp.max(sj, axis=0, keepdims=True))
            alpha = jnp.exp2(m_prev - m_new)
            pj = jnp.exp2(sj - m_new)
            l_ref[hd] = alpha * l_ref[hd] + jnp.sum(pj, axis=0, keepdims=True)
            m_ref[hd] = m_new
            pbs.append(pj.astype(BF16))
            alphas.append(alpha)
        for hd in heads:
            pv = (jnp.dot(vt_ref[hd, j0], pbs[hd][0:blk], preferred_element_type=F32)
                  + jnp.dot(vt_ref[hd, j0 + 1], pbs[hd][blk:2 * blk], preferred_element_type=F32))
            acc_ref[hd] = alphas[hd] * acc_ref[hd] + pv

    def trip_ahead(trip, cur_ref, nxt_ref):
        scores(trip + 1, nxt_ref)
        consume(trip, cur_ref)

    scores(0, s0_ref)
    own_ps = []
    for hd in heads:
        s = jnp.where(key_i <= qry_i, own_scores[hd], NEG)
        m0 = jnp.max(s, axis=0, keepdims=True)
        p = jnp.exp2(s - m0)
        m_ref[hd] = m0
        l_ref[hd] = jnp.sum(p, axis=0, keepdims=True)
        own_ps.append(p.astype(BF16))
    for hd in heads:
        acc_ref[hd] = jnp.dot(vt_ref[hd, qi], own_ps[hd], preferred_element_type=F32)

    n_trips = (qi + 1) // 2

    @pl.when(n_trips > 0)
    def _():
        def two_trips(r, carry):
            trip_ahead(2 * r, s0_ref, s1_ref)
            trip_ahead(2 * r + 1, s1_ref, s0_ref)
            return carry

        lax.fori_loop(0, (n_trips - 1) // 2, two_trips, 0)
        last = n_trips - 1

        @pl.when(n_trips % 2 == 1)
        def _():
            consume(last, s0_ref)

        @pl.when(n_trips % 2 == 0)
        def _():
            trip_ahead(last - 1, s0_ref, s1_ref)
            consume(last, s1_ref)

    for hd in heads:
        o_ref[:, hcols(hd)] = (acc_ref[hd] / l_ref[hd]).T.astype(o_ref.dtype)


def _attention(proj, rope_c, rope_sa, rope_sb, batch, seq, n_heads, q_col, k_col, v_col):
    t = proj.shape[0]
    blk = MOBA_BLOCK
    n_kb = seq // blk
    kb_pad = -(-n_kb // V7X_BF16_SUBLANES) * V7X_BF16_SUBLANES
    n_hd = ATTN_HEADS_PER_STEP
    wide = n_hd * HEAD_DIM
    assert n_kb % 2 == 0 and kb_pad <= HEAD_DIM and n_heads % n_hd == 0
    assert q_col % n_hd == 0 and k_col % n_hd == 0 and v_col % n_hd == 0
    groups = n_heads // n_hd
    kernel = functools.partial(_attn_kernel, n_kb=n_kb, kb_pad=kb_pad, n_hd=n_hd,
                               exp_scale=HEAD_DIM ** -0.5 * math.log2(math.e))
    tab = pl.BlockSpec((seq, HEAD_DIM), lambda bg, qi: (0, 0))
    return pl.pallas_call(
        kernel,
        grid=(batch * groups, n_kb),
        in_specs=[
            pl.BlockSpec((blk, wide), lambda bg, qi: ((bg // groups) * n_kb + qi, q_col // n_hd + bg % groups)),
            pl.BlockSpec((seq, wide), lambda bg, qi: (bg // groups, k_col // n_hd + bg % groups)),
            pl.BlockSpec((seq, wide), lambda bg, qi: (bg // groups, v_col // n_hd + bg % groups)),
            tab, tab, tab,
        ],
        out_specs=pl.BlockSpec((blk, wide), lambda bg, qi: ((bg // groups) * n_kb + qi, bg % groups)),
        out_shape=jax.ShapeDtypeStruct((t, n_heads * HEAD_DIM), BF16),
        scratch_shapes=[
            pltpu.VMEM((n_hd, seq, 2 * HEAD_DIM), BF16),
            pltpu.VMEM((n_hd, n_kb, HEAD_DIM, blk), BF16),
            pltpu.VMEM((n_hd, kb_pad, HEAD_DIM), F32),
            pltpu.VMEM((n_hd, 1, blk), F32),
            pltpu.VMEM((n_hd, 1, blk), F32),
            pltpu.VMEM((n_hd, HEAD_DIM, blk), F32),
            pltpu.VMEM((n_hd, 2 * HEAD_DIM, blk), BF16),
            pltpu.VMEM((n_hd, 2 * blk, blk), F32),
            pltpu.VMEM((n_hd, 2 * blk, blk), F32),
        ],
        compiler_params=_params("arbitrary", "arbitrary"),
        name="moba_attention",
    )(proj, proj, proj, rope_c, rope_sa, rope_sb)


def _merge_kernel(b_ref, c_ref, h_ref, cp_ref, hp_ref, gc_ref, ga_ref, at_ref, cw_ref, wc_ref, wa_ref,
                  o_ref, *, tiles_per_seq):
    hal = V7X_BF16_SUBLANES
    u = c_ref[...].astype(F32) * h_ref[...].astype(F32)
    up = cp_ref[...].astype(F32) * hp_ref[...].astype(F32)
    up = jnp.where(pl.program_id(0) % tiles_per_seq == 0, 0.0, up)
    row = lax.broadcasted_iota(jnp.int32, u.shape, 0)
    u1 = jnp.where(row == 0, up[hal - 1:hal, :], pltpu.roll(u, 1, 0))
    u2 = jnp.where(row == 0, up[hal - 2:hal - 1, :],
                   jnp.where(row == 1, up[hal - 1:hal, :], pltpu.roll(u, 2, 0)))
    y = cw_ref[0:1, :] * u2 + cw_ref[1:2, :] * u1 + cw_ref[2:3, :] * u
    cm = (b_ref[...].astype(F32) * y).astype(BF16)
    yc = jnp.dot(cm, wc_ref[...], preferred_element_type=F32)
    ya = jnp.dot(at_ref[...], wa_ref[...], preferred_element_type=F32)
    merged = jax.nn.sigmoid(gc_ref[...].astype(F32)) * yc + jax.nn.sigmoid(ga_ref[...].astype(F32)) * ya
    o_ref[...] = merged.astype(o_ref.dtype)


def _merge(proj, attn, conv_w, wc_bf, wa_bf, seq, cw, aw, d, tm):
    t = proj.shape[0]
    hal = V7X_BF16_SUBLANES
    kernel = functools.partial(_merge_kernel, tiles_per_seq=seq // tm)
    col = lambda cidx: pl.BlockSpec((tm, cw), lambda i: (i, cidx))
    halo = lambda cidx: pl.BlockSpec((hal, cw), lambda i: (jnp.maximum(i * (tm // hal) - 1, 0), cidx))
    const = lambda shape: pl.BlockSpec(shape, lambda i: (0, 0))
    return pl.pallas_call(
        kernel,
        grid=(t // tm,),
        in_specs=[col(0), col(1), col(2), halo(1), halo(2), col(6), col(7),
                  pl.BlockSpec((tm, aw), lambda i: (i, 0)),
                  const((CONV_K, cw)), const((cw, d)), const((aw, d))],
        out_specs=pl.BlockSpec((tm, d), lambda i: (i, 0)),
        out_shape=jax.ShapeDtypeStruct((t, d), BF16),
        compiler_params=_params("arbitrary"),
        name="merge_branches",
    )(proj, proj, proj, proj, proj, proj, proj, attn, conv_w, wc_bf, wa_bf)


def _oproj_kernel(mg_ref, x_ref, wo_ref, g_ref, wrt_ref, br_ref,
                  h1_ref, xn_ref, te_ref, tw_ref, rk_ref, cnt_ref, run_ref):
    h1 = x_ref[...] + jnp.dot(mg_ref[...], wo_ref[...], preferred_element_type=F32)
    h1_ref[...] = h1
    ms = jnp.mean(h1 * h1, axis=-1, keepdims=True)
    xn = h1 * lax.rsqrt(ms + NORM_EPS) * g_ref[...]
    xn_ref[...] = xn
    lg = _dot3(wrt_ref[...], xn, NT_DIMS) + br_ref[...]
    n_e, tm = lg.shape
    sub = lax.broadcasted_iota(jnp.int32, lg.shape, 0)
    vals, idxs = [], []
    for _ in range(TOP_K):
        top = jnp.max(lg, axis=0, keepdims=True)
        idx = jnp.min(jnp.where(lg == top, sub, n_e), axis=0, keepdims=True)
        vals.append(top)
        idxs.append(idx)
        lg = jnp.where(sub == idx, -jnp.inf, lg)
    v = jnp.concatenate(vals, axis=0)
    ex = jnp.exp(v - vals[0])
    te_ref[...] = jnp.concatenate(idxs, axis=0)
    tw_ref[...] = ex / jnp.sum(ex, axis=0, keepdims=True)

    @pl.when(pl.program_id(0) == 0)
    def _():
        run_ref[...] = jnp.zeros_like(run_ref)

    upper = jnp.where(lax.broadcasted_iota(jnp.int32, (tm, tm), 0) <= lax.broadcasted_iota(jnp.int32, (tm, tm), 1),
                      1.0, 0.0).astype(BF16)
    run = run_ref[...]
    ranks = []
    for k in range(TOP_K):
        onehot = jnp.where(sub == idxs[k], 1.0, 0.0)
        incl = jnp.dot(onehot.astype(BF16), upper, preferred_element_type=F32)
        ranks.append(jnp.sum(onehot * (incl - 1.0 + run[:, 0:1]), axis=0, keepdims=True))
        run = run + incl[:, tm - 1:tm]
    run_ref[...] = run
    rk_ref[...] = jnp.concatenate(ranks, axis=0).astype(jnp.int32)
    cnt_ref[...] = run.astype(jnp.int32)


def _oproj(merged, x2, wo_bf, g, wr_t, br, tm):
    t, d = x2.shape
    n_e = wr_t.shape[0]
    const = lambda shape: pl.BlockSpec(shape, lambda i: (0, 0))
    row = pl.BlockSpec((tm, d), lambda i: (i, 0))
    top = pl.BlockSpec((TOP_K, tm), lambda i: (0, i))
    return pl.pallas_call(
        _oproj_kernel,
        grid=(t // tm,),
        in_specs=[row, row, const((d, d)), const((1, d)), const((n_e, d)), const((n_e, 1))],
        out_specs=[row, row, top, top, top, const((n_e, V7X_LANES))],
        out_shape=[jax.ShapeDtypeStruct((t, d), F32), jax.ShapeDtypeStruct((t, d), F32),
                   jax.ShapeDtypeStruct((TOP_K, t), jnp.int32), jax.ShapeDtypeStruct((TOP_K, t), F32),
                   jax.ShapeDtypeStruct((TOP_K, t), jnp.int32), jax.ShapeDtypeStruct((n_e, V7X_LANES), jnp.int32)],
        scratch_shapes=[pltpu.VMEM((n_e, V7X_LANES), F32)],
        compiler_params=_params("arbitrary"),
        name="oproj_router",
    )(merged, x2, wo_bf, g.reshape(1, d), wr_t, br.reshape(n_e, 1))


def _dispatch_kernel(dest_ref, fill_ref, x_ref, xs_ref, zero_ref, sem, zsem, *, n_e, n_blocks, tm_moe, tm):
    @pl.when(pl.program_id(0) == 0)
    def _():
        zero_ref[...] = jnp.zeros_like(zero_ref)

        def zcopy(b):
            return pltpu.make_async_copy(
                zero_ref, xs_ref.at[pl.ds(pl.multiple_of(b * tm_moe, tm_moe), tm_moe), :], zsem)

        def zstart(b, carry):
            zcopy(b).start()
            return carry

        def zwait(b, carry):
            zcopy(b).wait()
            return carry

        for e in range(n_e):
            @pl.when(fill_ref[e] >= 0)
            def _():
                zcopy(fill_ref[e]).start()
        lax.fori_loop(fill_ref[n_e], n_blocks, zstart, 0)
        for e in range(n_e):
            @pl.when(fill_ref[e] >= 0)
            def _():
                zcopy(fill_ref[e]).wait()
        lax.fori_loop(fill_ref[n_e], n_blocks, zwait, 0)

    def copy(k, g, j):
        row = dest_ref[k * tm + g * ROW_GROUP + j]
        return pltpu.make_async_copy(x_ref.at[g, pl.ds(j, 1), :], xs_ref.at[pl.ds(row, 1), :], sem)

    _for_each_row_copy(tm, copy)


ROW_GROUP = 8


def _for_each_row_copy(tm, copy):
    def start(g, carry):
        for j in range(ROW_GROUP):
            for k in range(TOP_K):
                copy(k, g, j).start(priority=(j * TOP_K + k) % 2)
        return carry

    def wait(g, carry):
        for j in range(ROW_GROUP):
            for k in range(TOP_K):
                copy(k, g, j).wait()
        return carry

    lax.fori_loop(0, tm // ROW_GROUP, start, 0)
    lax.fori_loop(0, tm // ROW_GROUP, wait, 0)


def _tile_major(dest, tm):
    k, t = dest.shape
    return dest.reshape(k, t // tm, tm).transpose(1, 0, 2).reshape(-1)


def _dispatch(dest_tiles, fill, xn, n_blocks, tm_moe, tm):
    t, d = xn.shape
    n_e = fill.shape[0] - 1
    kernel = functools.partial(_dispatch_kernel, n_e=n_e, n_blocks=n_blocks, tm_moe=tm_moe, tm=tm)
    return pl.pallas_call(
        kernel,
        grid=(t // tm,),
        in_specs=[pl.BlockSpec((TOP_K * tm,), lambda i: (i,), memory_space=pltpu.SMEM),
                  pl.BlockSpec(memory_space=pltpu.SMEM),
                  pl.BlockSpec((tm // ROW_GROUP, ROW_GROUP, d), lambda i: (i, 0, 0))],
        out_specs=pl.BlockSpec(memory_space=pl.ANY),
        out_shape=jax.ShapeDtypeStruct((n_blocks * tm_moe, d), F32),
        scratch_shapes=[pltpu.VMEM((tm_moe, d), F32), pltpu.SemaphoreType.DMA(()), pltpu.SemaphoreType.DMA(())],
        compiler_params=_params("arbitrary"),
        name="moe_dispatch",
    )(dest_tiles, fill, xn.reshape(t // ROW_GROUP, ROW_GROUP, d))


N_ITEM_ARRAYS = 11


def _expert_items(blk_start, blk_count, n_col, n_blocks):
    n_e = blk_count.shape[0]
    n_items = n_blocks * n_col
    item_end = jnp.cumsum(blk_count * n_col)
    item_start = item_end - blk_count * n_col
    n_used = item_end[-1]
    blocks_used = jnp.sum(blk_count)
    i = jnp.arange(n_items, dtype=jnp.int32)
    ok = i < n_used
    nonempty_before = jnp.cumsum(blk_count > 0) - (blk_count > 0)
    experts = jnp.arange(n_e, dtype=jnp.int32)

    def at(idx):
        ic = jnp.clip(idx, 0, jnp.maximum(n_used - 1, 0))
        e = jnp.sum(ic[:, None] >= item_end[None, :], axis=1).astype(jnp.int32)
        mine = e[:, None] == experts[None, :]
        pick = lambda table: jnp.sum(jnp.where(mine, table[None, :], 0), axis=1)
        local = ic - pick(item_start)
        cnt = jnp.maximum(pick(blk_count), 1)
        first_blk = pick(blk_start)
        return e, local // cnt, first_blk + local % cnt, cnt, first_blk, pick(nonempty_before)

    e, wcol, xblk, cnt, first_blk, groups_before = at(i)
    spare = i - n_used
    oblk = jnp.where(ok, xblk, blocks_used + spare // n_col)
    ocol = jnp.where(ok, wcol, spare % n_col)
    new = ok & (xblk == first_blk)
    slot = (groups_before * n_col + wcol) % 2
    nxt = i + cnt
    nxt_e, nxt_col = at(nxt)[:2]
    has_nxt = new & (nxt < n_used)
    as_i32 = lambda a: a.astype(jnp.int32)
    items = tuple(map(as_i32, (xblk, oblk, wcol, ocol, e, new, ok, slot, nxt_e, nxt_col, has_nxt)))
    assert len(items) == N_ITEM_ARRAYS and n_e == item_end.shape[0]
    return items


def _weight_tile_copy(w_hbm, stage_ref, sem, e, col, slot, which, tn):
    cols = pl.ds(pl.multiple_of(col * tn, tn), tn)
    return pltpu.make_async_copy(w_hbm.at[e, :, cols], stage_ref.at[slot, which], sem.at[slot, which])


def _stream_weight_tiles(i, items, w_hbms, stage_ref, sem, tn):
    _, _, wcol, _, exp, new, _, slot, nxt_e, nxt_col, has_nxt = items

    @pl.when(new[i] == 1)
    def _():
        s = slot[i]

        @pl.when(i == 0)
        def _():
            for which, w in enumerate(w_hbms):
                _weight_tile_copy(w, stage_ref, sem, exp[i], wcol[i], s, which, tn).start(priority=1)

        for which, w in enumerate(w_hbms):
            _weight_tile_copy(w, stage_ref, sem, exp[i], wcol[i], s, which, tn).wait()

        @pl.when(has_nxt[i] == 1)
        def _():
            for which, w in enumerate(w_hbms):
                _weight_tile_copy(w, stage_ref, sem, nxt_e[i], nxt_col[i], 1 - s, which, tn).start(priority=1)


WEIGHT_K_CHUNKS = 4


def _dot_rounding_tile(x, stage_ref, slot, which, bf_ref):
    kc = bf_ref.shape[0] // WEIGHT_K_CHUNKS
    acc = None
    for c in range(WEIGHT_K_CHUNKS):
        rows = slice(c * kc, (c + 1) * kc)
        w = stage_ref[slot, which, rows, :].astype(BF16)
        bf_ref[rows, :] = w
        part = jnp.dot(x[:, rows], w, preferred_element_type=F32)
        acc = part if acc is None else acc + part
    return acc


def _up_kernel(*refs):
    items = refs[:N_ITEM_ARRAYS]
    x_ref, wg_hbm, wu_hbm, bg_ref, bu_ref, o_ref, stage_ref, wgb_ref, wub_ref, sem = refs[N_ITEM_ARRAYS:]
    new, ok, slot = items[5], items[6], items[7]
    i = pl.program_id(0)
    _stream_weight_tiles(i, items, (wg_hbm, wu_hbm), stage_ref, sem, wgb_ref.shape[1])

    def finish(g, u):
        g = jnp.minimum(g + bg_ref[0], SWIGLU_LIMIT)
        u = jnp.clip(u + bu_ref[0], -SWIGLU_LIMIT, SWIGLU_LIMIT)
        o_ref[...] = ((u + 1.0) * (g * jax.nn.sigmoid(SWIGLU_ALPHA * g))).astype(o_ref.dtype)

    @pl.when(new[i] == 1)
    def _():
        x = x_ref[...].astype(BF16)
        finish(_dot_rounding_tile(x, stage_ref, slot[i], 0, wgb_ref),
               _dot_rounding_tile(x, stage_ref, slot[i], 1, wub_ref))

    @pl.when(ok[i] - new[i] == 1)
    def _():
        x = x_ref[...].astype(BF16)
        finish(jnp.dot(x, wgb_ref[...], preferred_element_type=F32),
               jnp.dot(x, wub_ref[...], preferred_element_type=F32))

    @pl.when(ok[i] == 0)
    def _():
        o_ref[...] = jnp.zeros_like(o_ref)


def _down_kernel(*refs):
    items = refs[:N_ITEM_ARRAYS]
    h_ref, wd_hbm, bd_ref, o_ref, stage_ref, wdb_ref, sem = refs[N_ITEM_ARRAYS:]
    new, ok, slot = items[5], items[6], items[7]
    i = pl.program_id(0)
    _stream_weight_tiles(i, items, (wd_hbm,), stage_ref, sem, wdb_ref.shape[1])

    @pl.when(new[i] == 1)
    def _():
        o_ref[...] = _dot_rounding_tile(h_ref[...], stage_ref, slot[i], 0, wdb_ref) + bd_ref[0]

    @pl.when(ok[i] - new[i] == 1)
    def _():
        o_ref[...] = jnp.dot(h_ref[...], wdb_ref[...], preferred_element_type=F32) + bd_ref[0]

    @pl.when(ok[i] == 0)
    def _():
        o_ref[...] = jnp.zeros_like(o_ref)


def _item_map(fn):
    return lambda i, *items: fn(i, *items[:5])


def _expert_up(items, xs, w_gate, b_gate, w_up, b_up, tm, tf):
    rows, d = xs.shape
    n_e, _, f = w_gate.shape
    n_items = items[0].shape[0]
    bspec = pl.BlockSpec((1, 1, tf), _item_map(lambda i, xb, ob, wc, oc, e: (e[i], 0, wc[i])))
    hbm = pl.BlockSpec(memory_space=pl.ANY)
    grid_spec = pltpu.PrefetchScalarGridSpec(
        num_scalar_prefetch=N_ITEM_ARRAYS,
        grid=(n_items,),
        in_specs=[pl.BlockSpec((tm, d), _item_map(lambda i, xb, ob, wc, oc, e: (xb[i], 0))),
                  hbm, hbm, bspec, bspec],
        out_specs=pl.BlockSpec((tm, tf), _item_map(lambda i, xb, ob, wc, oc, e: (ob[i], oc[i]))),
        scratch_shapes=[pltpu.VMEM((2, 2, d, tf), F32), pltpu.VMEM((d, tf), BF16), pltpu.VMEM((d, tf), BF16),
                        pltpu.SemaphoreType.DMA((2, 2))],
    )
    return pl.pallas_call(
        _up_kernel,
        grid_spec=grid_spec,
        out_shape=jax.ShapeDtypeStruct((rows, f), BF16),
        compiler_params=_params("arbitrary"),
        name="moe_up",
    )(*items, xs, w_gate, w_up, b_gate.reshape(n_e, 1, f), b_up.reshape(n_e, 1, f))


def _expert_down(items, hid, w_down, b_down, tm, tn):
    rows, f = hid.shape
    n_e, _, d = w_down.shape
    n_items = items[0].shape[0]
    grid_spec = pltpu.PrefetchScalarGridSpec(
        num_scalar_prefetch=N_ITEM_ARRAYS,
        grid=(n_items,),
        in_specs=[pl.BlockSpec((tm, f), _item_map(lambda i, xb, ob, wc, oc, e: (xb[i], 0))),
                  pl.BlockSpec(memory_space=pl.ANY),
                  pl.BlockSpec((1, 1, tn), _item_map(lambda i, xb, ob, wc, oc, e: (e[i], 0, wc[i])))],
        out_specs=pl.BlockSpec((tm, tn), _item_map(lambda i, xb, ob, wc, oc, e: (ob[i], oc[i]))),
        scratch_shapes=[pltpu.VMEM((2, 1, f, tn), F32), pltpu.VMEM((f, tn), BF16),
                        pltpu.SemaphoreType.DMA((2, 1))],
    )
    return pl.pallas_call(
        _down_kernel,
        grid_spec=grid_spec,
        out_shape=jax.ShapeDtypeStruct((rows, d), F32),
        compiler_params=_params("arbitrary"),
        name="moe_down",
    )(*items, hid, w_down, b_down.reshape(n_e, 1, d))


def _combine_kernel(dest_ref, h1_ref, w_ref, g_ref, ys_ref, o_ref, buf_ref, sem):
    tm, d = h1_ref.shape

    def copy(k, g, j):
        row = dest_ref[k * tm + g * ROW_GROUP + j]
        return pltpu.make_async_copy(ys_ref.at[pl.ds(row, 1), :], buf_ref.at[k, g, pl.ds(j, 1), :], sem)

    _for_each_row_copy(tm, copy)
    h2 = h1_ref[...]
    for k in range(TOP_K):
        h2 = h2 + w_ref[:, k:k + 1] * buf_ref[k].reshape(tm, d)
    ms = jnp.mean(h2 * h2, axis=-1, keepdims=True)
    o_ref[...] = h2 * lax.rsqrt(ms + NORM_EPS) * g_ref[...]


def _combine(dest_tiles, h1, w_tk, g, ys, tm):
    t, d = h1.shape
    return pl.pallas_call(
        _combine_kernel,
        grid=(t // tm,),
        in_specs=[pl.BlockSpec((TOP_K * tm,), lambda i: (i,), memory_space=pltpu.SMEM),
                  pl.BlockSpec((tm, d), lambda i: (i, 0)),
                  pl.BlockSpec((tm, TOP_K), lambda i: (i, 0)),
                  pl.BlockSpec((1, d), lambda i: (0, 0)),
                  pl.BlockSpec(memory_space=pl.ANY)],
        out_specs=pl.BlockSpec((tm, d), lambda i: (i, 0)),
        out_shape=jax.ShapeDtypeStruct((t, d), F32),
        scratch_shapes=[pltpu.VMEM((TOP_K, tm // ROW_GROUP, ROW_GROUP, d), F32), pltpu.SemaphoreType.DMA(())],
        compiler_params=_params("arbitrary"),
        name="moe_combine",
    )(dest_tiles, h1, w_tk, g.reshape(1, d), ys)


def _rope_tables(seq):
    half = ROPE_DIM // 2
    inv_freq = jnp.exp(-math.log(ROPE_THETA) * jnp.arange(half, dtype=F32) / half)
    ang = jnp.arange(seq, dtype=jnp.int32).astype(F32)[:, None] * inv_freq[None, :]
    cos, sin = jnp.cos(ang), jnp.sin(ang)
    zeros = jnp.zeros((seq, HEAD_DIM - ROPE_DIM), F32)
    zh = jnp.zeros((seq, half), F32)
    c = jnp.concatenate([cos, cos, jnp.ones_like(zeros)], axis=1)
    sa = jnp.concatenate([-sin, zh, zeros], axis=1)
    sb = jnp.concatenate([zh, sin, zeros], axis=1)
    return c, sa, sb


def _dest_kernel(base_ref, te_ref, rk_ref, o_ref):
    te = te_ref[...]
    rows = rk_ref[...]
    for e in range(base_ref.shape[0]):
        rows = rows + jnp.where(te == e, base_ref[e], 0)
    o_ref[...] = rows


def _dest_rows(base, top_e, rank, tl):
    k, t = top_e.shape
    blk = pl.BlockSpec((k, tl), lambda i: (0, i))
    return pl.pallas_call(
        _dest_kernel,
        grid=(t // tl,),
        in_specs=[pl.BlockSpec(memory_space=pltpu.SMEM), blk, blk],
        out_specs=blk,
        out_shape=jax.ShapeDtypeStruct((k, t), jnp.int32),
        compiler_params=_params("arbitrary"),
        name="moe_dest_rows",
    )(base, top_e, rank)


def _routing_tables(counts, tm):
    blk_count = (counts + tm - 1) // tm
    blk_end = jnp.cumsum(blk_count)
    blk_start = blk_end - blk_count
    fill = jnp.concatenate([jnp.where(blk_count > 0, blk_end - 1, -1), blk_end[-1:]])
    as_i32 = lambda a: a.astype(jnp.int32)
    return as_i32(blk_start), as_i32(blk_count), as_i32(blk_start * tm), as_i32(fill)


def kernel(x, norm_mix, w_in, conv_w, w_conv_out, w_attn_out, w_o, norm_ffn, w_router, b_router,
           w_gate, b_gate, w_up, b_up, w_down, b_down, norm_final):
    batch, seq, d = x.shape
    cw = w_conv_out.shape[1]
    aw = w_attn_out.shape[1]
    n_heads = aw // HEAD_DIM
    n_e = w_router.shape[-1]
    f = w_gate.shape[-1]
    t = batch * seq
    assert w_in.shape[0] == 1, "single layer: the final rmsnorm is fused into the MoE combine"
    assert cw == d and aw == d and seq % MOBA_BLOCK == 0

    tm_norm = min(512, t)
    tm_in, tn_in = min(1024, t), min(1024, w_in.shape[-1])
    tm_mid = min(256, seq)
    tm_moe = min(256, t)
    tf_up = min(1024, f)
    tn_down = min(2048, d)
    tm_disp = min(128, t)
    tm_comb = min(128, t)
    n_blocks = (t * TOP_K + n_e * (tm_moe - 1)) // tm_moe
    col = lambda width_off: width_off // HEAD_DIM

    rope_c, rope_sa, rope_sb = _rope_tables(seq)
    h = x.reshape(t, d)
    xn = _rmsnorm(h, norm_mix[0], tm_norm, BF16)
    proj = _inproj(xn, w_in[0], tm_in, tn_in)
    attn = _attention(proj, rope_c, rope_sa, rope_sb, batch, seq, n_heads,
                      col(3 * cw), col(3 * cw + aw), col(3 * cw + 2 * aw))
    merged = _merge(proj, attn, conv_w[0], w_conv_out[0].astype(BF16), w_attn_out[0].astype(BF16),
                    seq, cw, aw, d, tm_mid)
    h1, xn2, top_e, top_w, rank, counts = _oproj(merged, h, w_o[0].astype(BF16), norm_ffn[0],
                                                 w_router[0].T, b_router[0], tm_mid)
    blk_start, blk_count, base, fill = _routing_tables(counts[:, 0], tm_moe)
    dest = _dest_rows(base, top_e, rank, min(2048, t))
    xs = _dispatch(_tile_major(dest, tm_disp), fill, xn2, n_blocks, tm_moe, tm_disp)
    up_items = _expert_items(blk_start, blk_count, f // tf_up, n_blocks)
    hid = _expert_up(up_items, xs, w_gate[0], b_gate[0], w_up[0], b_up[0], tm_moe, tf_up)
    same_tiling = f // tf_up == d // tn_down
    down_items = up_items if same_tiling else _expert_items(blk_start, blk_count, d // tn_down, n_blocks)
    ys = _expert_down(down_items, hid, w_down[0], b_down[0], tm_moe, tn_down)
    out = _combine(_tile_major(dest, tm_comb), h1, top_w.T, norm_final, ys, tm_comb)
    return out.reshape(batch, seq, d)
```

```python
import functools
import math

import jax
import jax.numpy as jnp
from jax import lax
from jax.experimental import pallas as pl
from jax.experimental.pallas import tpu as pltpu

HEAD_DIM = 128
ROPE_DIM = HEAD_DIM // 4
ROPE_THETA = 500000.0
MOBA_BLOCK = 256
MOBA_TOPK = 3
CONV_K = 3
TOP_K = 4
SWIGLU_LIMIT = 7.0
SWIGLU_ALPHA = 1.702
NORM_EPS = 1e-5
NEG = -1e30
ATTN_HEADS_PER_STEP = 4

V7X_LANES = 128
V7X_BF16_SUBLANES = 16
V7X_VMEM_BYTES = 64 * 1024 * 1024
VMEM_LIMIT = V7X_VMEM_BYTES - 8 * 1024 * 1024

F32 = jnp.float32
BF16 = jnp.bfloat16
NT_DIMS = (((1,), (1,)), ((), ()))
NN_DIMS = (((1,), (0,)), ((), ()))


def _params(*sem):
    return pltpu.CompilerParams(dimension_semantics=sem, vmem_limit_bytes=VMEM_LIMIT)


def _split_bf16(a):
    hi = a.astype(BF16)
    return hi, (a - hi.astype(F32)).astype(BF16)


def _dot3(a, b, dims):
    ah, al = _split_bf16(a)
    bh, bl = _split_bf16(b)
    rows = a.shape[0]
    dot = lambda u, v: lax.dot_general(u, v, dims, preferred_element_type=F32)
    both = dot(jnp.concatenate([ah, al], axis=0), bh)
    return both[:rows] + (both[rows:] + dot(ah, bl))


def _rmsnorm_kernel(x_ref, g_ref, o_ref):
    x = x_ref[...]
    ms = jnp.mean(x * x, axis=-1, keepdims=True)
    o_ref[...] = (x * lax.rsqrt(ms + NORM_EPS) * g_ref[...]).astype(o_ref.dtype)


def _rmsnorm(x, g, tm, out_dtype):
    t, d = x.shape
    return pl.pallas_call(
        _rmsnorm_kernel,
        grid=(t // tm,),
        in_specs=[pl.BlockSpec((tm, d), lambda i: (i, 0)), pl.BlockSpec((1, d), lambda i: (0, 0))],
        out_specs=pl.BlockSpec((tm, d), lambda i: (i, 0)),
        out_shape=jax.ShapeDtypeStruct((t, d), out_dtype),
        compiler_params=_params("arbitrary"),
        name="rmsnorm",
    )(x, g.reshape(1, d))


def _inproj_kernel(x_ref, w_ref, o_ref, wbf_ref):
    @pl.when(pl.program_id(1) == 0)
    def _():
        wbf_ref[...] = w_ref[...].astype(BF16)

    o_ref[...] = jnp.dot(x_ref[...], wbf_ref[...], preferred_element_type=F32).astype(o_ref.dtype)


def _inproj(xn, w, tm, tn):
    t, d = xn.shape
    n = w.shape[1]
    return pl.pallas_call(
        _inproj_kernel,
        grid=(n // tn, t // tm),
        in_specs=[pl.BlockSpec((tm, d), lambda j, i: (i, 0)), pl.BlockSpec((d, tn), lambda j, i: (0, j))],
        out_specs=pl.BlockSpec((tm, tn), lambda j, i: (i, j)),
        out_shape=jax.ShapeDtypeStruct((t, n), BF16),
        scratch_shapes=[pltpu.VMEM((d, tn), BF16)],
        compiler_params=_params("arbitrary", "arbitrary"),
        name="inproj",
    )(xn, w)


def _rope(t, c, sa, sb):
    half = ROPE_DIM // 2
    return t * c + pltpu.roll(t, HEAD_DIM - half, 1) * sa + pltpu.roll(t, half, 1) * sb


def _attn_kernel(q_ref, k_ref, v_ref, c_ref, sa_ref, sb_ref, o_ref,
                 kaug_ref, vt_ref, kmean_ref, m_ref, l_ref, acc_ref, qaug_ref, s0_ref, s1_ref,
                 *, n_kb, kb_pad, n_hd, exp_scale):
    blk = MOBA_BLOCK
    qi = pl.program_id(1)
    heads = range(n_hd)
    hcols = lambda hd: slice(hd * HEAD_DIM, (hd + 1) * HEAD_DIM)

    @pl.when(qi == 0)
    def _prep():
        kmean_ref[...] = jnp.zeros_like(kmean_ref)
        qaug_ref[...] = jnp.zeros_like(qaug_ref)
        lane = lax.broadcasted_iota(jnp.int32, (blk, HEAD_DIM), 1)

        def body(j, carry):
            rows = pl.ds(pl.multiple_of(j * blk, blk), blk)
            onehot = jnp.where(lane == j, 1.0, 0.0).astype(BF16)
            for hd in heads:
                kr = _rope(k_ref[rows, hcols(hd)].astype(F32), c_ref[rows, :], sa_ref[rows, :], sb_ref[rows, :])
                kaug_ref[hd, rows, 0:HEAD_DIM] = kr.astype(BF16)
                kaug_ref[hd, rows, HEAD_DIM:2 * HEAD_DIM] = onehot
                kmean_ref[hd, pl.ds(j, 1), :] = jnp.mean(kr, axis=0, keepdims=True)
                vt_ref[hd, j] = v_ref[rows, hcols(hd)].astype(F32).T.astype(BF16)
            return carry

        lax.fori_loop(0, n_kb, body, 0)

    rows = pl.ds(pl.multiple_of(qi * blk, blk), blk)
    sub = lax.broadcasted_iota(jnp.int32, (kb_pad, blk), 0)
    valid = sub < qi
    key_i = lax.broadcasted_iota(jnp.int32, (blk, blk), 0)
    qry_i = lax.broadcasted_iota(jnp.int32, (blk, blk), 1)
    own_scores = []
    for hd in heads:
        qr = _rope(q_ref[:, hcols(hd)].astype(F32), c_ref[rows, :], sa_ref[rows, :], sb_ref[rows, :])
        qrt = qr.T
        qrt_bf = (qrt * exp_scale).astype(BF16)
        qaug_ref[hd, 0:HEAD_DIM, :] = qrt_bf
        own_scores.append(jnp.dot(kaug_ref[hd, rows, 0:HEAD_DIM], qrt_bf, preferred_element_type=F32))

        gate = _dot3(kmean_ref[hd], qrt, NN_DIMS)
        g = jnp.where(valid, gate, -jnp.inf)
        bias = jnp.full((kb_pad, blk), NEG, F32)
        for _ in range(MOBA_TOPK):
            top = jnp.max(g, axis=0, keepdims=True)
            idx = jnp.min(jnp.where(g == top, sub, kb_pad), axis=0, keepdims=True)
            pick = sub == idx
            bias = jnp.where(pick, jnp.where(valid, 0.0, bias), bias)
            g = jnp.where(pick, -jnp.inf, g)
        qaug_ref[hd, HEAD_DIM:HEAD_DIM + kb_pad, :] = bias.astype(BF16)

    def scores(trip, s_ref):
        keys = pl.ds(pl.multiple_of(trip * 2 * blk, 2 * blk), 2 * blk)
        for hd in heads:
            s_ref[hd] = jnp.dot(kaug_ref[hd, keys, :], qaug_ref[hd], preferred_element_type=F32)

    def consume(trip, s_ref):
        j0 = 2 * trip
        pbs, alphas = [], []
        for hd in heads:
            sj = s_ref[hd]
            m_prev = m_ref[hd]
            m_new = jnp.maximum(m_prev, jnp.max(sj, axis=0, keepdims=True))
            alpha = jnp.exp2(m_prev - m_new)
            pj = jnp.exp2(sj - m_new)
            l_ref[hd] = alpha * l_ref[hd] + jnp.sum(pj, axis=0, keepdims=True)
            m_ref[hd] = m_new
            pbs.append(pj.astype(BF16))
            alphas.append(alpha)
        for hd in heads:
            pv = (jnp.dot(vt_ref[hd, j0], pbs[hd][0:blk], preferred_element_type=F32)
                  + jnp.dot(vt_ref[hd, j0 + 1], pbs[hd][blk:2 * blk], preferred_element_type=F32))
            acc_ref[hd] = alphas[hd] * acc_ref[hd] + pv

    def trip_ahead(trip, cur_ref, nxt_ref):
        scores(trip + 1, nxt_ref)
        consume(trip, cur_ref)

    scores(0, s0_ref)
    own_ps = []
    for hd in heads:
        s = jnp.where(key_i <= qry_i, own_scores[hd], NEG)
        m0 = jnp.max(s, axis=0, keepdims=True)
        p = jnp.exp2(s - m0)
        m_ref[hd] = m0
        l_ref[hd] = jnp.sum(p, axis=0, keepdims=True)
        own_ps.append(p.astype(BF16))
    for hd in heads:
        acc_ref[hd] = jnp.dot(vt_ref[hd, qi], own_ps[hd], preferred_element_type=F32)

    n_trips = (qi + 1) // 2

    @pl.when(n_trips > 0)
    def _():
        def two_trips(r, carry):
            trip_ahead(2 * r, s0_ref, s1_ref)
            trip_ahead(2 * r + 1, s1_ref, s0_ref)
            return carry

        lax.fori_loop(0, (n_trips - 1) // 2, two_trips, 0)
        last = n_trips - 1

        @pl.when(n_trips % 2 == 1)
        def _():
            consume(last, s0_ref)

        @pl.when(n_trips % 2 == 0)
        def _():
            trip_ahead(last - 1, s0_ref, s1_ref)
            consume(last, s1_ref)

    for hd in heads:
        o_ref[:, hcols(hd)] = (acc_ref[hd] / l_ref[hd]).T.astype(o_ref.dtype)


def _attention(proj, rope_c, rope_sa, rope_sb, batch, seq, n_heads, q_col, k_col, v_col):
    t = proj.shape[0]
    blk = MOBA_BLOCK
    n_kb = seq // blk
    kb_pad = -(-n_kb // V7X_BF16_SUBLANES) * V7X_BF16_SUBLANES
    n_hd = ATTN_HEADS_PER_STEP
    wide = n_hd * HEAD_DIM
    assert n_kb % 2 == 0 and kb_pad <= HEAD_DIM and n_heads % n_hd == 0
    assert q_col % n_hd == 0 and k_col % n_hd == 0 and v_col % n_hd == 0
    groups = n_heads // n_hd
    kernel = functools.partial(_attn_kernel, n_kb=n_kb, kb_pad=kb_pad, n_hd=n_hd,
                               exp_scale=HEAD_DIM ** -0.5 * math.log2(math.e))
    tab = pl.BlockSpec((seq, HEAD_DIM), lambda bg, qi: (0, 0))
    return pl.pallas_call(
        kernel,
        grid=(batch * groups, n_kb),
        in_specs=[
            pl.BlockSpec((blk, wide), lambda bg, qi: ((bg // groups) * n_kb + qi, q_col // n_hd + bg % groups)),
            pl.BlockSpec((seq, wide), lambda bg, qi: (bg // groups, k_col // n_hd + bg % groups)),
            pl.BlockSpec((seq, wide), lambda bg, qi: (bg // groups, v_col // n_hd + bg % groups)),
            tab, tab, tab,
        ],
        out_specs=pl.BlockSpec((blk, wide), lambda bg, qi: ((bg // groups) * n_kb + qi, bg % groups)),
        out_shape=jax.ShapeDtypeStruct((t, n_heads * HEAD_DIM), BF16),
        scratch_shapes=[
            pltpu.VMEM((n_hd, seq, 2 * HEAD_DIM), BF16),
            pltpu.VMEM((n_hd, n_kb, HEAD_DIM, blk), BF16),
            pltpu.VMEM((n_hd, kb_pad, HEAD_DIM), F32),
            pltpu.VMEM((n_hd, 1, blk), F32),
            pltpu.VMEM((n_hd, 1, blk), F32),
            pltpu.VMEM((n_hd, HEAD_DIM, blk), F32),
            pltpu.VMEM((n_hd, 2 * HEAD_DIM, blk), BF16),
            pltpu.VMEM((n_hd, 2 * blk, blk), F32),
            pltpu.VMEM((n_hd, 2 * blk, blk), F32),
        ],
        compiler_params=_params("arbitrary", "arbitrary"),
        name="moba_attention",
    )(proj, proj, proj, rope_c, rope_sa, rope_sb)


def _merge_kernel(b_ref, c_ref, h_ref, cp_ref, hp_ref, gc_ref, ga_ref, at_ref, cw_ref, wc_ref, wa_ref,
                  o_ref, *, tiles_per_seq):
    hal = V7X_BF16_SUBLANES
    u = c_ref[...].astype(F32) * h_ref[...].astype(F32)
    up = cp_ref[...].astype(F32) * hp_ref[...].astype(F32)
    up = jnp.where(pl.program_id(0) % tiles_per_seq == 0, 0.0, up)
    row = lax.broadcasted_iota(jnp.int32, u.shape, 0)
    u1 = jnp.where(row == 0, up[hal - 1:hal, :], pltpu.roll(u, 1, 0))
    u2 = jnp.where(row == 0, up[hal - 2:hal - 1, :],
                   jnp.where(row == 1, up[hal - 1:hal, :], pltpu.roll(u, 2, 0)))
    y = cw_ref[0:1, :] * u2 + cw_ref[1:2, :] * u1 + cw_ref[2:3, :] * u
    cm = (b_ref[...].astype(F32) * y).astype(BF16)
    yc = jnp.dot(cm, wc_ref[...], preferred_element_type=F32)
    ya = jnp.dot(at_ref[...], wa_ref[...], preferred_element_type=F32)
    merged = jax.nn.sigmoid(gc_ref[...].astype(F32)) * yc + jax.nn.sigmoid(ga_ref[...].astype(F32)) * ya
    o_ref[...] = merged.astype(o_ref.dtype)


def _merge(proj, attn, conv_w, wc_bf, wa_bf, seq, cw, aw, d, tm):
    t = proj.shape[0]
    hal = V7X_BF16_SUBLANES
    kernel = functools.partial(_merge_kernel, tiles_per_seq=seq // tm)
    col = lambda cidx: pl.BlockSpec((tm, cw), lambda i: (i, cidx))
    halo = lambda cidx: pl.BlockSpec((hal, cw), lambda i: (jnp.maximum(i * (tm // hal) - 1, 0), cidx))
    const = lambda shape: pl.BlockSpec(shape, lambda i: (0, 0))
    return pl.pallas_call(
        kernel,
        grid=(t // tm,),
        in_specs=[col(0), col(1), col(2), halo(1), halo(2), col(6), col(7),
                  pl.BlockSpec((tm, aw), lambda i: (i, 0)),
                  const((CONV_K, cw)), const((cw, d)), const((aw, d))],
        out_specs=pl.BlockSpec((tm, d), lambda i: (i, 0)),
        out_shape=jax.ShapeDtypeStruct((t, d), BF16),
        compiler_params=_params("arbitrary"),
        name="merge_branches",
    )(proj, proj, proj, proj, proj, proj, proj, attn, conv_w, wc_bf, wa_bf)


def _oproj_kernel(mg_ref, x_ref, wo_ref, g_ref, wrt_ref, br_ref,
                  h1_ref, xn_ref, te_ref, tw_ref, rk_ref, cnt_ref, run_ref):
    h1 = x_ref[...] + jnp.dot(mg_ref[...], wo_ref[...], preferred_element_type=F32)
    h1_ref[...] = h1
    ms = jnp.mean(h1 * h1, axis=-1, keepdims=True)
    xn = h1 * lax.rsqrt(ms + NORM_EPS) * g_ref[...]
    xn_ref[...] = xn
    lg = _dot3(wrt_ref[...], xn, NT_DIMS) + br_ref[...]
    n_e, tm = lg.shape
    sub = lax.broadcasted_iota(jnp.int32, lg.shape, 0)
    vals, idxs = [], []
    for _ in range(TOP_K):
        top = jnp.max(lg, axis=0, keepdims=True)
        idx = jnp.min(jnp.where(lg == top, sub, n_e), axis=0, keepdims=True)
        vals.append(top)
        idxs.append(idx)
        lg = jnp.where(sub == idx, -jnp.inf, lg)
    v = jnp.concatenate(vals, axis=0)
    ex = jnp.exp(v - vals[0])
    te_ref[...] = jnp.concatenate(idxs, axis=0)
    tw_ref[...] = ex / jnp.sum(ex, axis=0, keepdims=True)

    @pl.when(pl.program_id(0) == 0)
    def _():
        run_ref[...] = jnp.zeros_like(run_ref)

    upper = jnp.where(lax.broadcasted_iota(jnp.int32, (tm, tm), 0) <= lax.broadcasted_iota(jnp.int32, (tm, tm), 1),
                      1.0, 0.0).astype(BF16)
    run = run_ref[...]
    ranks = []
    for k in range(TOP_K):
        onehot = jnp.where(sub == idxs[k], 1.0, 0.0)
        incl = jnp.dot(onehot.astype(BF16), upper, preferred_element_type=F32)
        ranks.append(jnp.sum(onehot * (incl - 1.0 + run[:, 0:1]), axis=0, keepdims=True))
        run = run + incl[:, tm - 1:tm]
    run_ref[...] = run
    rk_ref[...] = jnp.concatenate(ranks, axis=0).astype(jnp.int32)
    cnt_ref[...] = run.astype(jnp.int32)


def _oproj(merged, x2, wo_bf, g, wr_t, br, tm):
    t, d = x2.shape
    n_e = wr_t.shape[0]
    const = lambda shape: pl.BlockSpec(shape, lambda i: (0, 0))
    row = pl.BlockSpec((tm, d), lambda i: (i, 0))
    top = pl.BlockSpec((TOP_K, tm), lambda i: (0, i))
    return pl.pallas_call(
        _oproj_kernel,
        grid=(t // tm,),
        in_specs=[row, row, const((d, d)), const((1, d)), const((n_e, d)), const((n_e, 1))],
        out_specs=[row, row, top, top, top, const((n_e, V7X_LANES))],
        out_shape=[jax.ShapeDtypeStruct((t, d), F32), jax.ShapeDtypeStruct((t, d), F32),
                   jax.ShapeDtypeStruct((TOP_K, t), jnp.int32), jax.ShapeDtypeStruct((TOP_K, t), F32),
                   jax.ShapeDtypeStruct((TOP_K, t), jnp.int32), jax.ShapeDtypeStruct((n_e, V7X_LANES), jnp.int32)],
        scratch_shapes=[pltpu.VMEM((n_e, V7X_LANES), F32)],
        compiler_params=_params("arbitrary"),
        name="oproj_router",
    )(merged, x2, wo_bf, g.reshape(1, d), wr_t, br.reshape(n_e, 1))


def _dispatch_kernel(dest_ref, fill_ref, x_ref, xs_ref, zero_ref, sem, zsem, *, n_e, n_blocks, tm_moe, tm):
    @pl.when(pl.program_id(0) == 0)
    def _():
        zero_ref[...] = jnp.zeros_like(zero_ref)

        def zcopy(b):
            return pltpu.make_async_copy(
                zero_ref, xs_ref.at[pl.ds(pl.multiple_of(b * tm_moe, tm_moe), tm_moe), :], zsem)

        def zstart(b, carry):
            zcopy(b).start()
            return carry

        def zwait(b, carry):
            zcopy(b).wait()
            return carry

        for e in range(n_e):
            @pl.when(fill_ref[e] >= 0)
            def _():
                zcopy(fill_ref[e]).start()
        lax.fori_loop(fill_ref[n_e], n_blocks, zstart, 0)
        for e in range(n_e):
            @pl.when(fill_ref[e] >= 0)
            def _():
                zcopy(fill_ref[e]).wait()
        lax.fori_loop(fill_ref[n_e], n_blocks, zwait, 0)

    def copy(k, g, j):
        row = dest_ref[k * tm + g * ROW_GROUP + j]
        return pltpu.make_async_copy(x_ref.at[g, pl.ds(j, 1), :], xs_ref.at[pl.ds(row, 1), :], sem)

    _for_each_row_copy(tm, copy)


ROW_GROUP = 8


def _start_row_copies(tm, copy):
    def start(g, carry):
        for j in range(ROW_GROUP):
            for k in range(TOP_K):
                copy(k, g, j).start(priority=(j * TOP_K + k) % 2)
        return carry

    lax.fori_loop(0, tm // ROW_GROUP, start, 0)


def _wait_row_copies(tm, copy):
    def wait(g, carry):
        for j in range(ROW_GROUP):
            for k in range(TOP_K):
                copy(k, g, j).wait()
        return carry

    lax.fori_loop(0, tm // ROW_GROUP, wait, 0)


def _for_each_row_copy(tm, copy):
    _start_row_copies(tm, copy)
    _wait_row_copies(tm, copy)


def _tile_major(dest, tm):
    k, t = dest.shape
    return dest.reshape(k, t // tm, tm).transpose(1, 0, 2).reshape(-1)


def _dispatch(dest_tiles, fill, xn, n_blocks, tm_moe, tm):
    t, d = xn.shape
    n_e = fill.shape[0] - 1
    kernel = functools.partial(_dispatch_kernel, n_e=n_e, n_blocks=n_blocks, tm_moe=tm_moe, tm=tm)
    return pl.pallas_call(
        kernel,
        grid=(t // tm,),
        in_specs=[pl.BlockSpec((TOP_K * tm,), lambda i: (i,), memory_space=pltpu.SMEM),
                  pl.BlockSpec(memory_space=pltpu.SMEM),
                  pl.BlockSpec((tm // ROW_GROUP, ROW_GROUP, d), lambda i: (i, 0, 0))],
        out_specs=pl.BlockSpec(memory_space=pl.ANY),
        out_shape=jax.ShapeDtypeStruct((n_blocks * tm_moe, d), F32),
        scratch_shapes=[pltpu.VMEM((tm_moe, d), F32), pltpu.SemaphoreType.DMA(()), pltpu.SemaphoreType.DMA(())],
        compiler_params=_params("arbitrary"),
        name="moe_dispatch",
    )(dest_tiles, fill, xn.reshape(t // ROW_GROUP, ROW_GROUP, d))


WEIGHT_K_CHUNKS = 4


def _dot_rounding_tile(x, stage_ref, slot, which, bf_ref):
    kc = bf_ref.shape[0] // WEIGHT_K_CHUNKS
    acc = None
    for c in range(WEIGHT_K_CHUNKS):
        rows = slice(c * kc, (c + 1) * kc)
        w = stage_ref[slot, which, rows, :].astype(BF16)
        bf_ref[rows, :] = w
        part = jnp.dot(x[:, rows], w, preferred_element_type=F32)
        acc = part if acc is None else acc + part
    return acc


N_GROUP_ARRAYS = 6


def _expert_groups(blk_start, blk_count, n_col):
    n_e = blk_count.shape[0]
    experts = jnp.arange(n_e, dtype=jnp.int32)
    nonempty = blk_count > 0
    before = jnp.cumsum(nonempty) - nonempty
    later = jnp.where(nonempty[None, :] & (experts[None, :] > experts[:, None]), experts[None, :], n_e)
    nxt_expert = jnp.min(later, axis=1)
    rep = lambda a: jnp.repeat(a, n_col)
    col = jnp.tile(jnp.arange(n_col, dtype=jnp.int32), n_e)
    first = rep(nonempty & (before == 0)) & (col == 0)
    slot = (rep(before) * n_col + col) % 2
    nxt = jnp.where(col + 1 < n_col, rep(experts) * n_col + col + 1,
                    jnp.where(rep(nxt_expert) < n_e, rep(nxt_expert) * n_col, -1))
    nxt = jnp.where(rep(nonempty), nxt, -1)
    as_i32 = lambda a: a.astype(jnp.int32)
    groups = (rep(blk_start), rep(blk_count), slot, first, nxt, jnp.sum(blk_count).reshape(1))
    assert len(groups) == N_GROUP_ARRAYS
    return tuple(map(as_i32, groups))


def _expert_group_step(groups, w_hbms, rows_hbm, out_hbm, stage_ref, in_buf, out_buf,
                       sem_w, sem_i, sem_o, compute, *, n_col, n_blocks):
    blk0, cnt, slot, first, nxt, used = groups
    g = pl.program_id(0)
    e, col = g // n_col, g % n_col
    tm = in_buf.shape[1]
    tn = out_buf.shape[2]
    n = cnt[g]

    def weight_copies(grp, s):
        ge, gc = grp // n_col, grp % n_col
        cols = pl.ds(pl.multiple_of(gc * tn, tn), tn)
        return [pltpu.make_async_copy(w.at[ge, :, cols], stage_ref.at[s, which], sem_w.at[s, which])
                for which, w in enumerate(w_hbms)]

    def in_copy(b, s):
        rows = pl.ds(pl.multiple_of((blk0[g] + b) * tm, tm), tm)
        return pltpu.make_async_copy(rows_hbm.at[rows, :], in_buf.at[s], sem_i.at[s])

    def out_copy(blk, c, s):
        rows = pl.ds(pl.multiple_of(blk * tm, tm), tm)
        cols = pl.ds(pl.multiple_of(c * tn, tn), tn)
        return pltpu.make_async_copy(out_buf.at[s], out_hbm.at[rows, cols], sem_o.at[s])

    @pl.when(n > 0)
    def _():
        s = slot[g]
        in_copy(0, 0).start()

        @pl.when(first[g] == 1)
        def _():
            for cp in weight_copies(g, s):
                cp.start(priority=1)

        for cp in weight_copies(g, s):
            cp.wait()

        @pl.when(nxt[g] >= 0)
        def _():
            for cp in weight_copies(nxt[g], 1 - s):
                cp.start(priority=1)

        def body(b, carry):
            bs = b % 2
            in_copy(b, bs).wait()

            @pl.when(b + 1 < n)
            def _():
                in_copy(b + 1, 1 - bs).start()

            @pl.when(b >= 2)
            def _():
                out_copy(blk0[g] + b - 2, col, bs).wait()

            @pl.when(b == 0)
            def _():
                out_buf[bs] = compute(in_buf[bs], s, True).astype(out_buf.dtype)

            @pl.when(b > 0)
            def _():
                out_buf[bs] = compute(in_buf[bs], s, False).astype(out_buf.dtype)

            out_copy(blk0[g] + b, col, bs).start()
            return carry

        lax.fori_loop(0, n, body, 0)

        @pl.when(n >= 2)
        def _():
            out_copy(blk0[g] + n - 2, col, n % 2).wait()

        out_copy(blk0[g] + n - 1, col, (n - 1) % 2).wait()

    @pl.when(g == pl.num_programs(0) - 1)
    def _():
        out_buf[0] = jnp.zeros(out_buf.shape[1:], out_buf.dtype)

        def zstart(i, carry):
            out_copy(used[0] + i // n_col, i % n_col, 0).start()
            return carry

        def zwait(i, carry):
            out_copy(used[0] + i // n_col, i % n_col, 0).wait()
            return carry

        spare = (n_blocks - used[0]) * n_col
        lax.fori_loop(0, spare, zstart, 0)
        lax.fori_loop(0, spare, zwait, 0)


def _up_group_kernel(*refs, n_col, n_blocks):
    groups = refs[:N_GROUP_ARRAYS]
    (bg_ref, bu_ref, xs_hbm, wg_hbm, wu_hbm, hid_hbm,
     stage_ref, wgb_ref, wub_ref, in_buf, out_buf, sem_w, sem_i, sem_o) = refs[N_GROUP_ARRAYS:]

    def compute(x, slot, first):
        x = x.astype(BF16)
        if first:
            g = _dot_rounding_tile(x, stage_ref, slot, 0, wgb_ref)
            u = _dot_rounding_tile(x, stage_ref, slot, 1, wub_ref)
        else:
            g = jnp.dot(x, wgb_ref[...], preferred_element_type=F32)
            u = jnp.dot(x, wub_ref[...], preferred_element_type=F32)
        g = jnp.minimum(g + bg_ref[0], SWIGLU_LIMIT)
        u = jnp.clip(u + bu_ref[0], -SWIGLU_LIMIT, SWIGLU_LIMIT)
        return (u + 1.0) * (g * jax.nn.sigmoid(SWIGLU_ALPHA * g))

    _expert_group_step(groups, (wg_hbm, wu_hbm), xs_hbm, hid_hbm, stage_ref,
                       in_buf, out_buf, sem_w, sem_i, sem_o, compute, n_col=n_col, n_blocks=n_blocks)


def _down_group_kernel(*refs, n_col, n_blocks):
    groups = refs[:N_GROUP_ARRAYS]
    (bd_ref, hid_hbm, wd_hbm, ys_hbm,
     stage_ref, wdb_ref, in_buf, out_buf, sem_w, sem_i, sem_o) = refs[N_GROUP_ARRAYS:]

    def compute(h, slot, first):
        if first:
            y = _dot_rounding_tile(h, stage_ref, slot, 0, wdb_ref)
        else:
            y = jnp.dot(h, wdb_ref[...], preferred_element_type=F32)
        return y + bd_ref[0]

    _expert_group_step(groups, (wd_hbm,), hid_hbm, ys_hbm, stage_ref,
                       in_buf, out_buf, sem_w, sem_i, sem_o, compute, n_col=n_col, n_blocks=n_blocks)


def _moe_up(groups, xs, w_gate, b_gate, w_up, b_up, tm, tf, n_blocks):
    rows, d = xs.shape
    n_e, _, f = w_gate.shape
    n_col = f // tf
    bias = pl.BlockSpec((1, 1, tf), lambda g, *_: (g // n_col, 0, g % n_col))
    hbm = pl.BlockSpec(memory_space=pl.ANY)
    grid_spec = pltpu.PrefetchScalarGridSpec(
        num_scalar_prefetch=N_GROUP_ARRAYS,
        grid=(n_e * n_col,),
        in_specs=[bias, bias, hbm, hbm, hbm],
        out_specs=hbm,
        scratch_shapes=[pltpu.VMEM((2, 2, d, tf), F32), pltpu.VMEM((d, tf), BF16), pltpu.VMEM((d, tf), BF16),
                        pltpu.VMEM((2, tm, d), F32), pltpu.VMEM((2, tm, tf), BF16),
                        pltpu.SemaphoreType.DMA((2, 2)), pltpu.SemaphoreType.DMA((2,)),
                        pltpu.SemaphoreType.DMA((2,))],
    )
    return pl.pallas_call(
        functools.partial(_up_group_kernel, n_col=n_col, n_blocks=n_blocks),
        grid_spec=grid_spec,
        out_shape=jax.ShapeDtypeStruct((rows, f), BF16),
        compiler_params=_params("arbitrary"),
        name="moe_up",
    )(*groups, b_gate.reshape(n_e, 1, f), b_up.reshape(n_e, 1, f), xs, w_gate, w_up)


def _moe_down(groups, hid, w_down, b_down, tm, tn, n_blocks):
    rows, f = hid.shape
    n_e, _, d = w_down.shape
    n_col = d // tn
    grid_spec = pltpu.PrefetchScalarGridSpec(
        num_scalar_prefetch=N_GROUP_ARRAYS,
        grid=(n_e * n_col,),
        in_specs=[pl.BlockSpec((1, 1, tn), lambda g, *_: (g // n_col, 0, g % n_col)),
                  pl.BlockSpec(memory_space=pl.ANY), pl.BlockSpec(memory_space=pl.ANY)],
        out_specs=pl.BlockSpec(memory_space=pl.ANY),
        scratch_shapes=[pltpu.VMEM((2, 1, f, tn), F32), pltpu.VMEM((f, tn), BF16),
                        pltpu.VMEM((2, tm, f), BF16), pltpu.VMEM((2, tm, tn), F32),
                        pltpu.SemaphoreType.DMA((2, 1)), pltpu.SemaphoreType.DMA((2,)),
                        pltpu.SemaphoreType.DMA((2,))],
    )
    return pl.pallas_call(
        functools.partial(_down_group_kernel, n_col=n_col, n_blocks=n_blocks),
        grid_spec=grid_spec,
        out_shape=jax.ShapeDtypeStruct((rows, d), F32),
        compiler_params=_params("arbitrary"),
        name="moe_down",
    )(*groups, b_down.reshape(n_e, 1, d), hid, w_down)


def _combine_kernel(dest_ref, next_ref, h1_ref, w_ref, g_ref, ys_ref, o_ref, buf_ref, sem):
    tm, d = h1_ref.shape
    i = pl.program_id(0)

    def copies(idx_ref, slot):
        def copy(k, g, j):
            row = idx_ref[k * tm + g * ROW_GROUP + j]
            return pltpu.make_async_copy(ys_ref.at[pl.ds(row, 1), :], buf_ref.at[slot, k, g, pl.ds(j, 1), :],
                                         sem.at[slot])
        return copy

    def step(slot):
        @pl.when(i == 0)
        def _():
            _start_row_copies(tm, copies(dest_ref, slot))

        @pl.when(i + 1 < pl.num_programs(0))
        def _():
            _start_row_copies(tm, copies(next_ref, 1 - slot))

        _wait_row_copies(tm, copies(dest_ref, slot))
        h2 = h1_ref[...]
        for k in range(TOP_K):
            h2 = h2 + w_ref[:, k:k + 1] * buf_ref[slot, k].reshape(tm, d)
        ms = jnp.mean(h2 * h2, axis=-1, keepdims=True)
        o_ref[...] = h2 * lax.rsqrt(ms + NORM_EPS) * g_ref[...]

    for slot in range(2):
        pl.when(i % 2 == slot)(functools.partial(step, slot))


def _combine(dest_tiles, h1, w_tk, g, ys, tm):
    t, d = h1.shape
    last = t // tm - 1
    return pl.pallas_call(
        _combine_kernel,
        grid=(t // tm,),
        in_specs=[pl.BlockSpec((TOP_K * tm,), lambda i: (i,), memory_space=pltpu.SMEM),
                  pl.BlockSpec((TOP_K * tm,), lambda i: (jnp.minimum(i + 1, last),), memory_space=pltpu.SMEM),
                  pl.BlockSpec((tm, d), lambda i: (i, 0)),
                  pl.BlockSpec((tm, TOP_K), lambda i: (i, 0)),
                  pl.BlockSpec((1, d), lambda i: (0, 0)),
                  pl.BlockSpec(memory_space=pl.ANY)],
        out_specs=pl.BlockSpec((tm, d), lambda i: (i, 0)),
        out_shape=jax.ShapeDtypeStruct((t, d), F32),
        scratch_shapes=[pltpu.VMEM((2, TOP_K, tm // ROW_GROUP, ROW_GROUP, d), F32),
                        pltpu.SemaphoreType.DMA((2,))],
        compiler_params=_params("arbitrary"),
        name="moe_combine",
    )(dest_tiles, dest_tiles, h1, w_tk, g.reshape(1, d), ys)


def _rope_tables(seq):
    half = ROPE_DIM // 2
    inv_freq = jnp.exp(-math.log(ROPE_THETA) * jnp.arange(half, dtype=F32) / half)
    ang = jnp.arange(seq, dtype=jnp.int32).astype(F32)[:, None] * inv_freq[None, :]
    cos, sin = jnp.cos(ang), jnp.sin(ang)
    zeros = jnp.zeros((seq, HEAD_DIM - ROPE_DIM), F32)
    zh = jnp.zeros((seq, half), F32)
    c = jnp.concatenate([cos, cos, jnp.ones_like(zeros)], axis=1)
    sa = jnp.concatenate([-sin, zh, zeros], axis=1)
    sb = jnp.concatenate([zh, sin, zeros], axis=1)
    return c, sa, sb


def _dest_kernel(base_ref, te_ref, rk_ref, o_ref):
    te = te_ref[...]
    rows = rk_ref[...]
    for e in range(base_ref.shape[0]):
        rows = rows + jnp.where(te == e, base_ref[e], 0)
    o_ref[...] = rows


def _dest_rows(base, top_e, rank, tl):
    k, t = top_e.shape
    blk = pl.BlockSpec((k, tl), lambda i: (0, i))
    return pl.pallas_call(
        _dest_kernel,
        grid=(t // tl,),
        in_specs=[pl.BlockSpec(memory_space=pltpu.SMEM), blk, blk],
        out_specs=blk,
        out_shape=jax.ShapeDtypeStruct((k, t), jnp.int32),
        compiler_params=_params("arbitrary"),
        name="moe_dest_rows",
    )(base, top_e, rank)


def _routing_tables(counts, tm):
    blk_count = (counts + tm - 1) // tm
    blk_end = jnp.cumsum(blk_count)
    blk_start = blk_end - blk_count
    fill = jnp.concatenate([jnp.where(blk_count > 0, blk_end - 1, -1), blk_end[-1:]])
    as_i32 = lambda a: a.astype(jnp.int32)
    return as_i32(blk_start), as_i32(blk_count), as_i32(blk_start * tm), as_i32(fill)


def kernel(x, norm_mix, w_in, conv_w, w_conv_out, w_attn_out, w_o, norm_ffn, w_router, b_router,
           w_gate, b_gate, w_up, b_up, w_down, b_down, norm_final):
    batch, seq, d = x.shape
    cw = w_conv_out.shape[1]
    aw = w_attn_out.shape[1]
    n_heads = aw // HEAD_DIM
    n_e = w_router.shape[-1]
    f = w_gate.shape[-1]
    t = batch * seq
    assert w_in.shape[0] == 1, "single layer: the final rmsnorm is fused into the MoE combine"
    assert cw == d and aw == d and seq % MOBA_BLOCK == 0

    tm_norm = min(512, t)
    tm_in, tn_in = min(1024, t), min(1024, w_in.shape[-1])
    tm_mid = min(256, seq)
    tm_moe = min(256, t)
    tf_up = min(1024, f)
    tn_down = min(2048, d)
    tm_disp = min(128, t)
    tm_comb = min(128, t)
    n_blocks = (t * TOP_K + n_e * (tm_moe - 1)) // tm_moe
    col = lambda width_off: width_off // HEAD_DIM

    rope_c, rope_sa, rope_sb = _rope_tables(seq)
    h = x.reshape(t, d)
    xn = _rmsnorm(h, norm_mix[0], tm_norm, BF16)
    proj = _inproj(xn, w_in[0], tm_in, tn_in)
    attn = _attention(proj, rope_c, rope_sa, rope_sb, batch, seq, n_heads,
                      col(3 * cw), col(3 * cw + aw), col(3 * cw + 2 * aw))
    merged = _merge(proj, attn, conv_w[0], w_conv_out[0].astype(BF16), w_attn_out[0].astype(BF16),
                    seq, cw, aw, d, tm_mid)
    h1, xn2, top_e, top_w, rank, counts = _oproj(merged, h, w_o[0].astype(BF16), norm_ffn[0],
                                                 w_router[0].T, b_router[0], tm_mid)
    blk_start, blk_count, base, fill = _routing_tables(counts[:, 0], tm_moe)
    dest = _dest_rows(base, top_e, rank, min(2048, t))
    xs = _dispatch(_tile_major(dest, tm_disp), fill, xn2, n_blocks, tm_moe, tm_disp)
    hid = _moe_up(_expert_groups(blk_start, blk_count, f // tf_up), xs, w_gate[0], b_gate[0], w_up[0], b_up[0],
                  tm_moe, tf_up, n_blocks)
    ys = _moe_down(_expert_groups(blk_start, blk_count, d // tn_down), hid, w_down[0], b_down[0],
                   tm_moe, tn_down, n_blocks)
    out = _combine(_tile_major(dest, tm_comb), h1, top_w.T, norm_final, ys, tm_comb)
    return out.reshape(batch, seq, d)
```

```python
import functools
import math

import jax
import jax.numpy as jnp
from jax import lax
from jax.experimental import pallas as pl
from jax.experimental.pallas import tpu as pltpu

HEAD_DIM = 128
ROPE_DIM = HEAD_DIM // 4
ROPE_THETA = 500000.0
MOBA_BLOCK = 256
MOBA_TOPK = 3
CONV_K = 3
TOP_K = 4
SWIGLU_LIMIT = 7.0
SWIGLU_ALPHA = 1.702
NORM_EPS = 1e-5
NEG = -1e30
ATTN_HEADS_PER_STEP = 4

V7X_LANES = 128
V7X_BF16_SUBLANES = 16
V7X_VMEM_BYTES = 64 * 1024 * 1024
VMEM_LIMIT = V7X_VMEM_BYTES - 8 * 1024 * 1024

F32 = jnp.float32
BF16 = jnp.bfloat16
NT_DIMS = (((1,), (1,)), ((), ()))
NN_DIMS = (((1,), (0,)), ((), ()))


def _params(*sem):
    return pltpu.CompilerParams(dimension_semantics=sem, vmem_limit_bytes=VMEM_LIMIT)


def _split_bf16(a):
    hi = a.astype(BF16)
    return hi, (a - hi.astype(F32)).astype(BF16)


def _dot3(a, b, dims):
    ah, al = _split_bf16(a)
    bh, bl = _split_bf16(b)
    rows = a.shape[0]
    dot = lambda u, v: lax.dot_general(u, v, dims, preferred_element_type=F32)
    both = dot(jnp.concatenate([ah, al], axis=0), bh)
    return both[:rows] + (both[rows:] + dot(ah, bl))


def _rmsnorm_kernel(x_ref, g_ref, o_ref):
    x = x_ref[...]
    ms = jnp.mean(x * x, axis=-1, keepdims=True)
    o_ref[...] = (x * lax.rsqrt(ms + NORM_EPS) * g_ref[...]).astype(o_ref.dtype)


def _rmsnorm(x, g, tm, out_dtype):
    t, d = x.shape
    return pl.pallas_call(
        _rmsnorm_kernel,
        grid=(t // tm,),
        in_specs=[pl.BlockSpec((tm, d), lambda i: (i, 0)), pl.BlockSpec((1, d), lambda i: (0, 0))],
        out_specs=pl.BlockSpec((tm, d), lambda i: (i, 0)),
        out_shape=jax.ShapeDtypeStruct((t, d), out_dtype),
        compiler_params=_params("arbitrary"),
        name="rmsnorm",
    )(x, g.reshape(1, d))


def _inproj_kernel(x_ref, w_ref, o_ref, wbf_ref):
    @pl.when(pl.program_id(1) == 0)
    def _():
        wbf_ref[...] = w_ref[...].astype(BF16)

    o_ref[...] = jnp.dot(x_ref[...], wbf_ref[...], preferred_element_type=F32).astype(o_ref.dtype)


def _inproj(xn, w, tm, tn):
    t, d = xn.shape
    n = w.shape[1]
    return pl.pallas_call(
        _inproj_kernel,
        grid=(n // tn, t // tm),
        in_specs=[pl.BlockSpec((tm, d), lambda j, i: (i, 0)), pl.BlockSpec((d, tn), lambda j, i: (0, j))],
        out_specs=pl.BlockSpec((tm, tn), lambda j, i: (i, j)),
        out_shape=jax.ShapeDtypeStruct((t, n), BF16),
        scratch_shapes=[pltpu.VMEM((d, tn), BF16)],
        compiler_params=_params("arbitrary", "arbitrary"),
        name="inproj",
    )(xn, w)


def _rope(t, c, sa, sb):
    half = ROPE_DIM // 2
    return t * c + pltpu.roll(t, HEAD_DIM - half, 1) * sa + pltpu.roll(t, half, 1) * sb


def _attn_kernel(q_ref, k_ref, v_ref, c_ref, sa_ref, sb_ref, o_ref,
                 kaug_ref, vt_ref, kmean_ref, m_ref, l_ref, acc_ref, qaug_ref, s0_ref, s1_ref,
                 *, n_kb, kb_pad, n_hd, exp_scale):
    blk = MOBA_BLOCK
    qi = pl.program_id(1)
    heads = range(n_hd)
    hcols = lambda hd: slice(hd * HEAD_DIM, (hd + 1) * HEAD_DIM)

    @pl.when(qi == 0)
    def _prep():
        kmean_ref[...] = jnp.zeros_like(kmean_ref)
        qaug_ref[...] = jnp.zeros_like(qaug_ref)
        lane = lax.broadcasted_iota(jnp.int32, (blk, HEAD_DIM), 1)

        def body(j, carry):
            rows = pl.ds(pl.multiple_of(j * blk, blk), blk)
            onehot = jnp.where(lane == j, 1.0, 0.0).astype(BF16)
            for hd in heads:
                kr = _rope(k_ref[rows, hcols(hd)].astype(F32), c_ref[rows, :], sa_ref[rows, :], sb_ref[rows, :])
                kaug_ref[hd, rows, 0:HEAD_DIM] = kr.astype(BF16)
                kaug_ref[hd, rows, HEAD_DIM:2 * HEAD_DIM] = onehot
                kmean_ref[hd, pl.ds(j, 1), :] = jnp.mean(kr, axis=0, keepdims=True)
                vt_ref[hd, j] = v_ref[rows, hcols(hd)].astype(F32).T.astype(BF16)
            return carry

        lax.fori_loop(0, n_kb, body, 0)

    rows = pl.ds(pl.multiple_of(qi * blk, blk), blk)
    sub = lax.broadcasted_iota(jnp.int32, (kb_pad, blk), 0)
    valid = sub < qi
    key_i = lax.broadcasted_iota(jnp.int32, (blk, blk), 0)
    qry_i = lax.broadcasted_iota(jnp.int32, (blk, blk), 1)
    own_scores = []
    for hd in heads:
        qr = _rope(q_ref[:, hcols(hd)].astype(F32), c_ref[rows, :], sa_ref[rows, :], sb_ref[rows, :])
        qrt = qr.T
        qrt_bf = (qrt * exp_scale).astype(BF16)
        qaug_ref[hd, 0:HEAD_DIM, :] = qrt_bf
        own_scores.append(jnp.dot(kaug_ref[hd, rows, 0:HEAD_DIM], qrt_bf, preferred_element_type=F32))

        gate = _dot3(kmean_ref[hd], qrt, NN_DIMS)
        g = jnp.where(valid, gate, -jnp.inf)
        bias = jnp.full((kb_pad, blk), NEG, F32)
        for _ in range(MOBA_TOPK):
            top = jnp.max(g, axis=0, keepdims=True)
            idx = jnp.min(jnp.where(g == top, sub, kb_pad), axis=0, keepdims=True)
            pick = sub == idx
            bias = jnp.where(pick, jnp.where(valid, 0.0, bias), bias)
            g = jnp.where(pick, -jnp.inf, g)
        qaug_ref[hd, HEAD_DIM:HEAD_DIM + kb_pad, :] = bias.astype(BF16)

    def scores(trip, s_ref):
        keys = pl.ds(pl.multiple_of(trip * 2 * blk, 2 * blk), 2 * blk)
        for hd in heads:
            s_ref[hd] = jnp.dot(kaug_ref[hd, keys, :], qaug_ref[hd], preferred_element_type=F32)

    def consume(trip, s_ref):
        j0 = 2 * trip
        pbs, alphas = [], []
        for hd in heads:
            sj = s_ref[hd]
            m_prev = m_ref[hd]
            m_new = jnp.maximum(m_prev, jnp.max(sj, axis=0, keepdims=True))
            alpha = jnp.exp2(m_prev - m_new)
            pj = jnp.exp2(sj - m_new)
            l_ref[hd] = alpha * l_ref[hd] + jnp.sum(pj, axis=0, keepdims=True)
            m_ref[hd] = m_new
            pbs.append(pj.astype(BF16))
            alphas.append(alpha)
        for hd in heads:
            pv = (jnp.dot(vt_ref[hd, j0], pbs[hd][0:blk], preferred_element_type=F32)
                  + jnp.dot(vt_ref[hd, j0 + 1], pbs[hd][blk:2 * blk], preferred_element_type=F32))
            acc_ref[hd] = alphas[hd] * acc_ref[hd] + pv

    def trip_ahead(trip, cur_ref, nxt_ref):
        scores(trip + 1, nxt_ref)
        consume(trip, cur_ref)

    scores(0, s0_ref)
    own_ps = []
    for hd in heads:
        s = jnp.where(key_i <= qry_i, own_scores[hd], NEG)
        m0 = jnp.max(s, axis=0, keepdims=True)
        p = jnp.exp2(s - m0)
        m_ref[hd] = m0
        l_ref[hd] = jnp.sum(p, axis=0, keepdims=True)
        own_ps.append(p.astype(BF16))
    for hd in heads:
        acc_ref[hd] = jnp.dot(vt_ref[hd, qi], own_ps[hd], preferred_element_type=F32)

    n_trips = (qi + 1) // 2

    @pl.when(n_trips > 0)
    def _():
        def two_trips(r, carry):
            trip_ahead(2 * r, s0_ref, s1_ref)
            trip_ahead(2 * r + 1, s1_ref, s0_ref)
            return carry

        lax.fori_loop(0, (n_trips - 1) // 2, two_trips, 0)
        last = n_trips - 1

        @pl.when(n_trips % 2 == 1)
        def _():
            consume(last, s0_ref)

        @pl.when(n_trips % 2 == 0)
        def _():
            trip_ahead(last - 1, s0_ref, s1_ref)
            consume(last, s1_ref)

    for hd in heads:
        o_ref[:, hcols(hd)] = (acc_ref[hd] / l_ref[hd]).T.astype(o_ref.dtype)


def _attention(proj, rope_c, rope_sa, rope_sb, batch, seq, n_heads, q_col, k_col, v_col):
    t = proj.shape[0]
    blk = MOBA_BLOCK
    n_kb = seq // blk
    kb_pad = -(-n_kb // V7X_BF16_SUBLANES) * V7X_BF16_SUBLANES
    n_hd = ATTN_HEADS_PER_STEP
    wide = n_hd * HEAD_DIM
    assert n_kb % 2 == 0 and kb_pad <= HEAD_DIM and n_heads % n_hd == 0
    assert q_col % n_hd == 0 and k_col % n_hd == 0 and v_col % n_hd == 0
    groups = n_heads // n_hd
    kernel = functools.partial(_attn_kernel, n_kb=n_kb, kb_pad=kb_pad, n_hd=n_hd,
                               exp_scale=HEAD_DIM ** -0.5 * math.log2(math.e))
    tab = pl.BlockSpec((seq, HEAD_DIM), lambda bg, qi: (0, 0))
    return pl.pallas_call(
        kernel,
        grid=(batch * groups, n_kb),
        in_specs=[
            pl.BlockSpec((blk, wide), lambda bg, qi: ((bg // groups) * n_kb + qi, q_col // n_hd + bg % groups)),
            pl.BlockSpec((seq, wide), lambda bg, qi: (bg // groups, k_col // n_hd + bg % groups)),
            pl.BlockSpec((seq, wide), lambda bg, qi: (bg // groups, v_col // n_hd + bg % groups)),
            tab, tab, tab,
        ],
        out_specs=pl.BlockSpec((blk, wide), lambda bg, qi: ((bg // groups) * n_kb + qi, bg % groups)),
        out_shape=jax.ShapeDtypeStruct((t, n_heads * HEAD_DIM), BF16),
        scratch_shapes=[
            pltpu.VMEM((n_hd, seq, 2 * HEAD_DIM), BF16),
            pltpu.VMEM((n_hd, n_kb, HEAD_DIM, blk), BF16),
            pltpu.VMEM((n_hd, kb_pad, HEAD_DIM), F32),
            pltpu.VMEM((n_hd, 1, blk), F32),
            pltpu.VMEM((n_hd, 1, blk), F32),
            pltpu.VMEM((n_hd, HEAD_DIM, blk), F32),
            pltpu.VMEM((n_hd, 2 * HEAD_DIM, blk), BF16),
            pltpu.VMEM((n_hd, 2 * blk, blk), F32),
            pltpu.VMEM((n_hd, 2 * blk, blk), F32),
        ],
        compiler_params=_params("arbitrary", "arbitrary"),
        name="moba_attention",
    )(proj, proj, proj, rope_c, rope_sa, rope_sb)


def _merge_kernel(b_ref, c_ref, h_ref, cp_ref, hp_ref, gc_ref, ga_ref, at_ref, cw_ref, wc_ref, wa_ref,
                  o_ref, *, tiles_per_seq):
    hal = V7X_BF16_SUBLANES
    u = c_ref[...].astype(F32) * h_ref[...].astype(F32)
    up = cp_ref[...].astype(F32) * hp_ref[...].astype(F32)
    up = jnp.where(pl.program_id(0) % tiles_per_seq == 0, 0.0, up)
    row = lax.broadcasted_iota(jnp.int32, u.shape, 0)
    u1 = jnp.where(row == 0, up[hal - 1:hal, :], pltpu.roll(u, 1, 0))
    u2 = jnp.where(row == 0, up[hal - 2:hal - 1, :],
                   jnp.where(row == 1, up[hal - 1:hal, :], pltpu.roll(u, 2, 0)))
    y = cw_ref[0:1, :] * u2 + cw_ref[1:2, :] * u1 + cw_ref[2:3, :] * u
    cm = (b_ref[...].astype(F32) * y).astype(BF16)
    yc = jnp.dot(cm, wc_ref[...], preferred_element_type=F32)
    ya = jnp.dot(at_ref[...], wa_ref[...], preferred_element_type=F32)
    merged = jax.nn.sigmoid(gc_ref[...].astype(F32)) * yc + jax.nn.sigmoid(ga_ref[...].astype(F32)) * ya
    o_ref[...] = merged.astype(o_ref.dtype)


def _merge(proj, attn, conv_w, wc_bf, wa_bf, seq, cw, aw, d, tm):
    t = proj.shape[0]
    hal = V7X_BF16_SUBLANES
    kernel = functools.partial(_merge_kernel, tiles_per_seq=seq // tm)
    col = lambda cidx: pl.BlockSpec((tm, cw), lambda i: (i, cidx))
    halo = lambda cidx: pl.BlockSpec((hal, cw), lambda i: (jnp.maximum(i * (tm // hal) - 1, 0), cidx))
    const = lambda shape: pl.BlockSpec(shape, lambda i: (0, 0))
    return pl.pallas_call(
        kernel,
        grid=(t // tm,),
        in_specs=[col(0), col(1), col(2), halo(1), halo(2), col(6), col(7),
                  pl.BlockSpec((tm, aw), lambda i: (i, 0)),
                  const((CONV_K, cw)), const((cw, d)), const((aw, d))],
        out_specs=pl.BlockSpec((tm, d), lambda i: (i, 0)),
        out_shape=jax.ShapeDtypeStruct((t, d), BF16),
        compiler_params=_params("arbitrary"),
        name="merge_branches",
    )(proj, proj, proj, proj, proj, proj, proj, attn, conv_w, wc_bf, wa_bf)


def _oproj_kernel(mg_ref, x_ref, wo_ref, g_ref, wrt_ref, br_ref,
                  h1_ref, xn_ref, te_ref, tw_ref, rk_ref, cnt_ref, run_ref):
    h1 = x_ref[...] + jnp.dot(mg_ref[...], wo_ref[...], preferred_element_type=F32)
    h1_ref[...] = h1
    ms = jnp.mean(h1 * h1, axis=-1, keepdims=True)
    xn = h1 * lax.rsqrt(ms + NORM_EPS) * g_ref[...]
    xn_ref[...] = xn
    lg = _dot3(wrt_ref[...], xn, NT_DIMS) + br_ref[...]
    n_e, tm = lg.shape
    sub = lax.broadcasted_iota(jnp.int32, lg.shape, 0)
    vals, idxs = [], []
    for _ in range(TOP_K):
        top = jnp.max(lg, axis=0, keepdims=True)
        idx = jnp.min(jnp.where(lg == top, sub, n_e), axis=0, keepdims=True)
        vals.append(top)
        idxs.append(idx)
        lg = jnp.where(sub == idx, -jnp.inf, lg)
    v = jnp.concatenate(vals, axis=0)
    ex = jnp.exp(v - vals[0])
    te_ref[...] = jnp.concatenate(idxs, axis=0)
    tw_ref[...] = ex / jnp.sum(ex, axis=0, keepdims=True)

    @pl.when(pl.program_id(0) == 0)
    def _():
        run_ref[...] = jnp.zeros_like(run_ref)

    upper = jnp.where(lax.broadcasted_iota(jnp.int32, (tm, tm), 0) <= lax.broadcasted_iota(jnp.int32, (tm, tm), 1),
                      1.0, 0.0).astype(BF16)
    run = run_ref[...]
    ranks = []
    for k in range(TOP_K):
        onehot = jnp.where(sub == idxs[k], 1.0, 0.0)
        incl = jnp.dot(onehot.astype(BF16), upper, preferred_element_type=F32)
        ranks.append(jnp.sum(onehot * (incl - 1.0 + run[:, 0:1]), axis=0, keepdims=True))
        run = run + incl[:, tm - 1:tm]
    run_ref[...] = run
    rk_ref[...] = jnp.concatenate(ranks, axis=0).astype(jnp.int32)
    cnt_ref[...] = run.astype(jnp.int32)


def _oproj(merged, x2, wo_bf, g, wr_t, br, tm):
    t, d = x2.shape
    n_e = wr_t.shape[0]
    const = lambda shape: pl.BlockSpec(shape, lambda i: (0, 0))
    row = pl.BlockSpec((tm, d), lambda i: (i, 0))
    top = pl.BlockSpec((TOP_K, tm), lambda i: (0, i))
    return pl.pallas_call(
        _oproj_kernel,
        grid=(t // tm,),
        in_specs=[row, row, const((d, d)), const((1, d)), const((n_e, d)), const((n_e, 1))],
        out_specs=[row, row, top, top, top, const((n_e, V7X_LANES))],
        out_shape=[jax.ShapeDtypeStruct((t, d), F32), jax.ShapeDtypeStruct((t, d), F32),
                   jax.ShapeDtypeStruct((TOP_K, t), jnp.int32), jax.ShapeDtypeStruct((TOP_K, t), F32),
                   jax.ShapeDtypeStruct((TOP_K, t), jnp.int32), jax.ShapeDtypeStruct((n_e, V7X_LANES), jnp.int32)],
        scratch_shapes=[pltpu.VMEM((n_e, V7X_LANES), F32)],
        compiler_params=_params("arbitrary"),
        name="oproj_router",
    )(merged, x2, wo_bf, g.reshape(1, d), wr_t, br.reshape(n_e, 1))


def _dispatch_kernel(dest_ref, fill_ref, x_hbm, xs_ref, zero_ref, x_buf, sem, lsem, zsem,
                     *, n_e, n_blocks, tm_moe, tm):
    @pl.when(pl.program_id(0) == 0)
    def _():
        zero_ref[...] = jnp.zeros_like(zero_ref)

        def zcopy(b):
            return pltpu.make_async_copy(
                zero_ref, xs_ref.at[pl.ds(pl.multiple_of(b * tm_moe, tm_moe), tm_moe), :], zsem)

        def zstart(b, carry):
            zcopy(b).start()
            return carry

        def zwait(b, carry):
            zcopy(b).wait()
            return carry

        for e in range(n_e):
            @pl.when(fill_ref[e] >= 0)
            def _():
                zcopy(fill_ref[e]).start()
        lax.fori_loop(fill_ref[n_e], n_blocks, zstart, 0)
        for e in range(n_e):
            @pl.when(fill_ref[e] >= 0)
            def _():
                zcopy(fill_ref[e]).wait()
        lax.fori_loop(fill_ref[n_e], n_blocks, zwait, 0)

    i = pl.program_id(0)
    groups = tm // ROW_GROUP

    def load(tile, slot):
        rows = pl.ds(pl.multiple_of(tile * groups, groups), groups)
        return pltpu.make_async_copy(x_hbm.at[rows], x_buf.at[slot], lsem.at[slot])

    def scatters(slot):
        def copy(k, g, j):
            row = dest_ref[k * tm + g * ROW_GROUP + j]
            return pltpu.make_async_copy(x_buf.at[slot, g, pl.ds(j, 1), :], xs_ref.at[pl.ds(row, 1), :],
                                         sem.at[slot])
        return copy

    def step(slot):
        nxt, prv = (slot + 1) % DISPATCH_BUFFERS, (slot - 1) % DISPATCH_BUFFERS

        @pl.when(i == 0)
        def _():
            load(i, slot).start()

        @pl.when(i + 1 < pl.num_programs(0))
        def _():
            load(i + 1, nxt).start()

        load(i, slot).wait()
        _start_row_copies(tm, scatters(slot))

        @pl.when(i > 0)
        def _():
            _wait_row_copies(tm, scatters(prv))

        @pl.when(i + 1 == pl.num_programs(0))
        def _():
            _wait_row_copies(tm, scatters(slot))

    for slot in range(DISPATCH_BUFFERS):
        pl.when(i % DISPATCH_BUFFERS == slot)(functools.partial(step, slot))


DISPATCH_BUFFERS = 3

ROW_GROUP = 8


def _start_row_copies(tm, copy):
    def start(g, carry):
        for j in range(ROW_GROUP):
            for k in range(TOP_K):
                copy(k, g, j).start(priority=(j * TOP_K + k) % 2)
        return carry

    lax.fori_loop(0, tm // ROW_GROUP, start, 0)


def _wait_row_copies(tm, copy):
    def wait(g, carry):
        for j in range(ROW_GROUP):
            for k in range(TOP_K):
                copy(k, g, j).wait()
        return carry

    lax.fori_loop(0, tm // ROW_GROUP, wait, 0)


def _for_each_row_copy(tm, copy):
    _start_row_copies(tm, copy)
    _wait_row_copies(tm, copy)


def _tile_major(dest, tm):
    k, t = dest.shape
    return dest.reshape(k, t // tm, tm).transpose(1, 0, 2).reshape(-1)


def _dispatch(dest_tiles, fill, xn, n_blocks, tm_moe, tm):
    t, d = xn.shape
    n_e = fill.shape[0] - 1
    kernel = functools.partial(_dispatch_kernel, n_e=n_e, n_blocks=n_blocks, tm_moe=tm_moe, tm=tm)
    return pl.pallas_call(
        kernel,
        grid=(t // tm,),
        in_specs=[pl.BlockSpec((TOP_K * tm,), lambda i: (i,), memory_space=pltpu.SMEM),
                  pl.BlockSpec(memory_space=pltpu.SMEM),
                  pl.BlockSpec(memory_space=pl.ANY)],
        out_specs=pl.BlockSpec(memory_space=pl.ANY),
        out_shape=jax.ShapeDtypeStruct((n_blocks * tm_moe, d), F32),
        scratch_shapes=[pltpu.VMEM((tm_moe, d), F32),
                        pltpu.VMEM((DISPATCH_BUFFERS, tm // ROW_GROUP, ROW_GROUP, d), F32),
                        pltpu.SemaphoreType.DMA((DISPATCH_BUFFERS,)), pltpu.SemaphoreType.DMA((DISPATCH_BUFFERS,)),
                        pltpu.SemaphoreType.DMA(())],
        compiler_params=_params("arbitrary"),
        name="moe_dispatch",
    )(dest_tiles, fill, xn.reshape(t // ROW_GROUP, ROW_GROUP, d))


WEIGHT_K_CHUNKS = 4


def _dot_rounding_tile(x, stage_ref, slot, which, bf_ref):
    kc = bf_ref.shape[0] // WEIGHT_K_CHUNKS
    acc = None
    for c in range(WEIGHT_K_CHUNKS):
        rows = slice(c * kc, (c + 1) * kc)
        w = stage_ref[slot, which, rows, :].astype(BF16)
        bf_ref[rows, :] = w
        part = jnp.dot(x[:, rows], w, preferred_element_type=F32)
        acc = part if acc is None else acc + part
    return acc


N_ITEM_ARRAYS = 11


def _expert_items(blk_start, blk_count, n_col, n_blocks):
    n_e = blk_count.shape[0]
    n_items = n_blocks * n_col
    item_end = jnp.cumsum(blk_count * n_col)
    item_start = item_end - blk_count * n_col
    n_used = item_end[-1]
    blocks_used = jnp.sum(blk_count)
    i = jnp.arange(n_items, dtype=jnp.int32)
    ok = i < n_used
    nonempty_before = jnp.cumsum(blk_count > 0) - (blk_count > 0)
    experts = jnp.arange(n_e, dtype=jnp.int32)

    def at(idx):
        ic = jnp.clip(idx, 0, jnp.maximum(n_used - 1, 0))
        e = jnp.sum(ic[:, None] >= item_end[None, :], axis=1).astype(jnp.int32)
        mine = e[:, None] == experts[None, :]
        pick = lambda table: jnp.sum(jnp.where(mine, table[None, :], 0), axis=1)
        local = ic - pick(item_start)
        cnt = jnp.maximum(pick(blk_count), 1)
        first_blk = pick(blk_start)
        return e, local // cnt, first_blk + local % cnt, cnt, first_blk, pick(nonempty_before)

    e, wcol, xblk, cnt, first_blk, groups_before = at(i)
    spare = i - n_used
    oblk = jnp.where(ok, xblk, blocks_used + spare // n_col)
    ocol = jnp.where(ok, wcol, spare % n_col)
    new = ok & (xblk == first_blk)
    slot = (groups_before * n_col + wcol) % 2
    nxt = i + cnt
    nxt_e, nxt_col = at(nxt)[:2]
    has_nxt = new & (nxt < n_used)
    as_i32 = lambda a: a.astype(jnp.int32)
    items = tuple(map(as_i32, (xblk, oblk, wcol, ocol, e, new, ok, slot, nxt_e, nxt_col, has_nxt)))
    assert len(items) == N_ITEM_ARRAYS and n_e == item_end.shape[0]
    return items


def _weight_tile_copy(w_hbm, stage_ref, sem, e, col, slot, which, tn):
    cols = pl.ds(pl.multiple_of(col * tn, tn), tn)
    return pltpu.make_async_copy(w_hbm.at[e, :, cols], stage_ref.at[slot, which], sem.at[slot, which])


def _stream_weight_tiles(i, items, w_hbms, stage_ref, sem, tn):
    _, _, wcol, _, exp, new, _, slot, nxt_e, nxt_col, has_nxt = items

    @pl.when(new[i] == 1)
    def _():
        s = slot[i]

        @pl.when(i == 0)
        def _():
            for which, w in enumerate(w_hbms):
                _weight_tile_copy(w, stage_ref, sem, exp[i], wcol[i], s, which, tn).start(priority=1)

        for which, w in enumerate(w_hbms):
            _weight_tile_copy(w, stage_ref, sem, exp[i], wcol[i], s, which, tn).wait()

        @pl.when(has_nxt[i] == 1)
        def _():
            for which, w in enumerate(w_hbms):
                _weight_tile_copy(w, stage_ref, sem, nxt_e[i], nxt_col[i], 1 - s, which, tn).start(priority=1)


def _up_kernel(*refs):
    items = refs[:N_ITEM_ARRAYS]
    x_ref, wg_hbm, wu_hbm, bg_ref, bu_ref, o_ref, stage_ref, wgb_ref, wub_ref, sem = refs[N_ITEM_ARRAYS:]
    new, ok, slot = items[5], items[6], items[7]
    i = pl.program_id(0)
    _stream_weight_tiles(i, items, (wg_hbm, wu_hbm), stage_ref, sem, wgb_ref.shape[1])

    def finish(g, u):
        g = jnp.minimum(g + bg_ref[0], SWIGLU_LIMIT)
        u = jnp.clip(u + bu_ref[0], -SWIGLU_LIMIT, SWIGLU_LIMIT)
        o_ref[...] = ((u + 1.0) * (g * jax.nn.sigmoid(SWIGLU_ALPHA * g))).astype(o_ref.dtype)

    @pl.when(new[i] == 1)
    def _():
        x = x_ref[...].astype(BF16)
        finish(_dot_rounding_tile(x, stage_ref, slot[i], 0, wgb_ref),
               _dot_rounding_tile(x, stage_ref, slot[i], 1, wub_ref))

    @pl.when(ok[i] - new[i] == 1)
    def _():
        x = x_ref[...].astype(BF16)
        finish(jnp.dot(x, wgb_ref[...], preferred_element_type=F32),
               jnp.dot(x, wub_ref[...], preferred_element_type=F32))

    @pl.when(ok[i] == 0)
    def _():
        o_ref[...] = jnp.zeros_like(o_ref)


def _down_kernel(*refs):
    items = refs[:N_ITEM_ARRAYS]
    h_ref, wd_hbm, bd_ref, o_ref, stage_ref, wdb_ref, sem = refs[N_ITEM_ARRAYS:]
    new, ok, slot = items[5], items[6], items[7]
    i = pl.program_id(0)
    _stream_weight_tiles(i, items, (wd_hbm,), stage_ref, sem, wdb_ref.shape[1])

    @pl.when(new[i] == 1)
    def _():
        o_ref[...] = _dot_rounding_tile(h_ref[...], stage_ref, slot[i], 0, wdb_ref) + bd_ref[0]

    @pl.when(ok[i] - new[i] == 1)
    def _():
        o_ref[...] = jnp.dot(h_ref[...], wdb_ref[...], preferred_element_type=F32) + bd_ref[0]

    @pl.when(ok[i] == 0)
    def _():
        o_ref[...] = jnp.zeros_like(o_ref)


def _item_map(fn):
    return lambda i, *items: fn(i, *items[:5])


def _expert_up(items, xs, w_gate, b_gate, w_up, b_up, tm, tf):
    rows, d = xs.shape
    n_e, _, f = w_gate.shape
    n_items = items[0].shape[0]
    bspec = pl.BlockSpec((1, 1, tf), _item_map(lambda i, xb, ob, wc, oc, e: (e[i], 0, wc[i])))
    hbm = pl.BlockSpec(memory_space=pl.ANY)
    grid_spec = pltpu.PrefetchScalarGridSpec(
        num_scalar_prefetch=N_ITEM_ARRAYS,
        grid=(n_items,),
        in_specs=[pl.BlockSpec((tm, d), _item_map(lambda i, xb, ob, wc, oc, e: (xb[i], 0))),
                  hbm, hbm, bspec, bspec],
        out_specs=pl.BlockSpec((tm, tf), _item_map(lambda i, xb, ob, wc, oc, e: (ob[i], oc[i]))),
        scratch_shapes=[pltpu.VMEM((2, 2, d, tf), F32), pltpu.VMEM((d, tf), BF16), pltpu.VMEM((d, tf), BF16),
                        pltpu.SemaphoreType.DMA((2, 2))],
    )
    return pl.pallas_call(
        _up_kernel,
        grid_spec=grid_spec,
        out_shape=jax.ShapeDtypeStruct((rows, f), BF16),
        compiler_params=_params("arbitrary"),
        name="moe_up",
    )(*items, xs, w_gate, w_up, b_gate.reshape(n_e, 1, f), b_up.reshape(n_e, 1, f))


def _expert_down(items, hid, w_down, b_down, tm, tn):
    rows, f = hid.shape
    n_e, _, d = w_down.shape
    n_items = items[0].shape[0]
    grid_spec = pltpu.PrefetchScalarGridSpec(
        num_scalar_prefetch=N_ITEM_ARRAYS,
        grid=(n_items,),
        in_specs=[pl.BlockSpec((tm, f), _item_map(lambda i, xb, ob, wc, oc, e: (xb[i], 0))),
                  pl.BlockSpec(memory_space=pl.ANY),
                  pl.BlockSpec((1, 1, tn), _item_map(lambda i, xb, ob, wc, oc, e: (e[i], 0, wc[i])))],
        out_specs=pl.BlockSpec((tm, tn), _item_map(lambda i, xb, ob, wc, oc, e: (ob[i], oc[i]))),
        scratch_shapes=[pltpu.VMEM((2, 1, f, tn), F32), pltpu.VMEM((f, tn), BF16),
                        pltpu.SemaphoreType.DMA((2, 1))],
    )
    return pl.pallas_call(
        _down_kernel,
        grid_spec=grid_spec,
        out_shape=jax.ShapeDtypeStruct((rows, d), F32),
        compiler_params=_params("arbitrary"),
        name="moe_down",
    )(*items, hid, w_down, b_down.reshape(n_e, 1, d))


def _combine_kernel(dest_ref, next_ref, h1_ref, w_ref, g_ref, ys_ref, o_ref, buf_ref, sem):
    tm, d = h1_ref.shape
    i = pl.program_id(0)

    def copies(idx_ref, slot):
        def copy(k, g, j):
            row = idx_ref[k * tm + g * ROW_GROUP + j]
            return pltpu.make_async_copy(ys_ref.at[pl.ds(row, 1), :], buf_ref.at[slot, k, g, pl.ds(j, 1), :],
                                         sem.at[slot])
        return copy

    def step(slot):
        @pl.when(i == 0)
        def _():
            _start_row_copies(tm, copies(dest_ref, slot))

        @pl.when(i + 1 < pl.num_programs(0))
        def _():
            _start_row_copies(tm, copies(next_ref, 1 - slot))

        _wait_row_copies(tm, copies(dest_ref, slot))
        h2 = h1_ref[...]
        for k in range(TOP_K):
            h2 = h2 + w_ref[:, k:k + 1] * buf_ref[slot, k].reshape(tm, d)
        ms = jnp.mean(h2 * h2, axis=-1, keepdims=True)
        o_ref[...] = h2 * lax.rsqrt(ms + NORM_EPS) * g_ref[...]

    for slot in range(2):
        pl.when(i % 2 == slot)(functools.partial(step, slot))


def _combine(dest_tiles, h1, w_tk, g, ys, tm):
    t, d = h1.shape
    last = t // tm - 1
    return pl.pallas_call(
        _combine_kernel,
        grid=(t // tm,),
        in_specs=[pl.BlockSpec((TOP_K * tm,), lambda i: (i,), memory_space=pltpu.SMEM),
                  pl.BlockSpec((TOP_K * tm,), lambda i: (jnp.minimum(i + 1, last),), memory_space=pltpu.SMEM),
                  pl.BlockSpec((tm, d), lambda i: (i, 0)),
                  pl.BlockSpec((tm, TOP_K), lambda i: (i, 0)),
                  pl.BlockSpec((1, d), lambda i: (0, 0)),
                  pl.BlockSpec(memory_space=pl.ANY)],
        out_specs=pl.BlockSpec((tm, d), lambda i: (i, 0)),
        out_shape=jax.ShapeDtypeStruct((t, d), F32),
        scratch_shapes=[pltpu.VMEM((2, TOP_K, tm // ROW_GROUP, ROW_GROUP, d), F32),
                        pltpu.SemaphoreType.DMA((2,))],
        compiler_params=_params("arbitrary"),
        name="moe_combine",
    )(dest_tiles, dest_tiles, h1, w_tk, g.reshape(1, d), ys)


def _rope_tables(seq):
    half = ROPE_DIM // 2
    inv_freq = jnp.exp(-math.log(ROPE_THETA) * jnp.arange(half, dtype=F32) / half)
    ang = jnp.arange(seq, dtype=jnp.int32).astype(F32)[:, None] * inv_freq[None, :]
    cos, sin = jnp.cos(ang), jnp.sin(ang)
    zeros = jnp.zeros((seq, HEAD_DIM - ROPE_DIM), F32)
    zh = jnp.zeros((seq, half), F32)
    c = jnp.concatenate([cos, cos, jnp.ones_like(zeros)], axis=1)
    sa = jnp.concatenate([-sin, zh, zeros], axis=1)
    sb = jnp.concatenate([zh, sin, zeros], axis=1)
    return c, sa, sb


def _dest_kernel(base_ref, te_ref, rk_ref, o_ref):
    te = te_ref[...]
    rows = rk_ref[...]
    for e in range(base_ref.shape[0]):
        rows = rows + jnp.where(te == e, base_ref[e], 0)
    o_ref[...] = rows


def _dest_rows(base, top_e, rank, tl):
    k, t = top_e.shape
    blk = pl.BlockSpec((k, tl), lambda i: (0, i))
    return pl.pallas_call(
        _dest_kernel,
        grid=(t // tl,),
        in_specs=[pl.BlockSpec(memory_space=pltpu.SMEM), blk, blk],
        out_specs=blk,
        out_shape=jax.ShapeDtypeStruct((k, t), jnp.int32),
        compiler_params=_params("arbitrary"),
        name="moe_dest_rows",
    )(base, top_e, rank)


def _routing_tables(counts, tm):
    blk_count = (counts + tm - 1) // tm
    blk_end = jnp.cumsum(blk_count)
    blk_start = blk_end - blk_count
    fill = jnp.concatenate([jnp.where(blk_count > 0, blk_end - 1, -1), blk_end[-1:]])
    as_i32 = lambda a: a.astype(jnp.int32)
    return as_i32(blk_start), as_i32(blk_count), as_i32(blk_start * tm), as_i32(fill)


def kernel(x, norm_mix, w_in, conv_w, w_conv_out, w_attn_out, w_o, norm_ffn, w_router, b_router,
           w_gate, b_gate, w_up, b_up, w_down, b_down, norm_final):
    batch, seq, d = x.shape
    cw = w_conv_out.shape[1]
    aw = w_attn_out.shape[1]
    n_heads = aw // HEAD_DIM
    n_e = w_router.shape[-1]
    f = w_gate.shape[-1]
    t = batch * seq
    assert w_in.shape[0] == 1, "single layer: the final rmsnorm is fused into the MoE combine"
    assert cw == d and aw == d and seq % MOBA_BLOCK == 0

    tm_norm = min(512, t)
    tm_in, tn_in = min(1024, t), min(1024, w_in.shape[-1])
    tm_mid = min(256, seq)
    tm_moe = min(256, t)
    tf_up = min(1024, f)
    tn_down = min(2048, d)
    tm_disp = min(128, t)
    tm_comb = min(128, t)
    n_blocks = (t * TOP_K + n_e * (tm_moe - 1)) // tm_moe
    col = lambda width_off: width_off // HEAD_DIM

    rope_c, rope_sa, rope_sb = _rope_tables(seq)
    h = x.reshape(t, d)
    xn = _rmsnorm(h, norm_mix[0], tm_norm, BF16)
    proj = _inproj(xn, w_in[0], tm_in, tn_in)
    attn = _attention(proj, rope_c, rope_sa, rope_sb, batch, seq, n_heads,
                      col(3 * cw), col(3 * cw + aw), col(3 * cw + 2 * aw))
    merged = _merge(proj, attn, conv_w[0], w_conv_out[0].astype(BF16), w_attn_out[0].astype(BF16),
                    seq, cw, aw, d, tm_mid)
    h1, xn2, top_e, top_w, rank, counts = _oproj(merged, h, w_o[0].astype(BF16), norm_ffn[0],
                                                 w_router[0].T, b_router[0], tm_mid)
    blk_start, blk_count, base, fill = _routing_tables(counts[:, 0], tm_moe)
    dest = _dest_rows(base, top_e, rank, min(2048, t))
    xs = _dispatch(_tile_major(dest, tm_disp), fill, xn2, n_blocks, tm_moe, tm_disp)
    up_items = _expert_items(blk_start, blk_count, f // tf_up, n_blocks)
    hid = _expert_up(up_items, xs, w_gate[0], b_gate[0], w_up[0], b_up[0], tm_moe, tf_up)
    same_tiling = f // tf_up == d // tn_down
    down_items = up_items if same_tiling else _expert_items(blk_start, blk_count, d // tn_down, n_blocks)
    ys = _expert_down(down_items, hid, w_down[0], b_down[0], tm_moe, tn_down)
    out = _combine(_tile_major(dest, tm_comb), h1, top_w.T, norm_final, ys, tm_comb)
    return out.reshape(batch, seq, d)
```

```python
import functools
import math

import jax
import jax.numpy as jnp
from jax import lax
from jax.experimental import pallas as pl
from jax.experimental.pallas import tpu as pltpu

HEAD_DIM = 128
ROPE_DIM = HEAD_DIM // 4
ROPE_THETA = 500000.0
MOBA_BLOCK = 256
MOBA_TOPK = 3
CONV_K = 3
TOP_K = 4
SWIGLU_LIMIT = 7.0
SWIGLU_ALPHA = 1.702
NORM_EPS = 1e-5
NEG = -1e30
ATTN_HEADS_PER_STEP = 4

V7X_LANES = 128
V7X_BF16_SUBLANES = 16
V7X_VMEM_BYTES = 64 * 1024 * 1024
VMEM_LIMIT = V7X_VMEM_BYTES - 8 * 1024 * 1024

F32 = jnp.float32
BF16 = jnp.bfloat16
NT_DIMS = (((1,), (1,)), ((), ()))
NN_DIMS = (((1,), (0,)), ((), ()))


def _params(*sem):
    return pltpu.CompilerParams(dimension_semantics=sem, vmem_limit_bytes=VMEM_LIMIT)


def _split_bf16(a):
    hi = a.astype(BF16)
    return hi, (a - hi.astype(F32)).astype(BF16)


def _dot3(a, b, dims):
    ah, al = _split_bf16(a)
    bh, bl = _split_bf16(b)
    rows = a.shape[0]
    dot = lambda u, v: lax.dot_general(u, v, dims, preferred_element_type=F32)
    both = dot(jnp.concatenate([ah, al], axis=0), bh)
    return both[:rows] + (both[rows:] + dot(ah, bl))


def _rmsnorm_kernel(x_ref, g_ref, o_ref):
    x = x_ref[...]
    ms = jnp.mean(x * x, axis=-1, keepdims=True)
    o_ref[...] = (x * lax.rsqrt(ms + NORM_EPS) * g_ref[...]).astype(o_ref.dtype)


def _rmsnorm(x, g, tm, out_dtype):
    t, d = x.shape
    return pl.pallas_call(
        _rmsnorm_kernel,
        grid=(t // tm,),
        in_specs=[pl.BlockSpec((tm, d), lambda i: (i, 0)), pl.BlockSpec((1, d), lambda i: (0, 0))],
        out_specs=pl.BlockSpec((tm, d), lambda i: (i, 0)),
        out_shape=jax.ShapeDtypeStruct((t, d), out_dtype),
        compiler_params=_params("arbitrary"),
        name="rmsnorm",
    )(x, g.reshape(1, d))


def _inproj_kernel(x_ref, w_ref, o_ref, wbf_ref):
    @pl.when(pl.program_id(1) == 0)
    def _():
        wbf_ref[...] = w_ref[...].astype(BF16)

    o_ref[...] = jnp.dot(x_ref[...], wbf_ref[...], preferred_element_type=F32).astype(o_ref.dtype)


def _inproj(xn, w, tm, tn):
    t, d = xn.shape
    n = w.shape[1]
    return pl.pallas_call(
        _inproj_kernel,
        grid=(n // tn, t // tm),
        in_specs=[pl.BlockSpec((tm, d), lambda j, i: (i, 0)), pl.BlockSpec((d, tn), lambda j, i: (0, j))],
        out_specs=pl.BlockSpec((tm, tn), lambda j, i: (i, j)),
        out_shape=jax.ShapeDtypeStruct((t, n), BF16),
        scratch_shapes=[pltpu.VMEM((d, tn), BF16)],
        compiler_params=_params("arbitrary", "arbitrary"),
        name="inproj",
    )(xn, w)


def _rope(t, c, sa, sb):
    half = ROPE_DIM // 2
    return t * c + pltpu.roll(t, HEAD_DIM - half, 1) * sa + pltpu.roll(t, half, 1) * sb


def _attn_kernel(q_ref, k_ref, v_ref, c_ref, sa_ref, sb_ref, o_ref,
                 kaug_ref, vt_ref, kmean_ref, m_ref, l_ref, acc_ref, qaug_ref, s0_ref, s1_ref,
                 *, n_kb, kb_pad, n_hd, exp_scale):
    blk = MOBA_BLOCK
    qi = pl.program_id(1)
    heads = range(n_hd)
    hcols = lambda hd: slice(hd * HEAD_DIM, (hd + 1) * HEAD_DIM)

    @pl.when(qi == 0)
    def _prep():
        kmean_ref[...] = jnp.zeros_like(kmean_ref)
        qaug_ref[...] = jnp.zeros_like(qaug_ref)
        lane = lax.broadcasted_iota(jnp.int32, (blk, HEAD_DIM), 1)

        def body(j, carry):
            rows = pl.ds(pl.multiple_of(j * blk, blk), blk)
            onehot = jnp.where(lane == j, 1.0, 0.0).astype(BF16)
            for hd in heads:
                kr = _rope(k_ref[rows, hcols(hd)].astype(F32), c_ref[rows, :], sa_ref[rows, :], sb_ref[rows, :])
                kaug_ref[hd, rows, 0:HEAD_DIM] = kr.astype(BF16)
                kaug_ref[hd, rows, HEAD_DIM:2 * HEAD_DIM] = onehot
                kmean_ref[hd, pl.ds(j, 1), :] = jnp.mean(kr, axis=0, keepdims=True)
                vt_ref[hd, j] = v_ref[rows, hcols(hd)].astype(F32).T.astype(BF16)
            return carry

        lax.fori_loop(0, n_kb, body, 0)

    rows = pl.ds(pl.multiple_of(qi * blk, blk), blk)
    sub = lax.broadcasted_iota(jnp.int32, (kb_pad, blk), 0)
    valid = sub < qi
    key_i = lax.broadcasted_iota(jnp.int32, (blk, blk), 0)
    qry_i = lax.broadcasted_iota(jnp.int32, (blk, blk), 1)
    own_scores = []
    for hd in heads:
        qr = _rope(q_ref[:, hcols(hd)].astype(F32), c_ref[rows, :], sa_ref[rows, :], sb_ref[rows, :])
        qrt = qr.T
        qrt_bf = (qrt * exp_scale).astype(BF16)
        qaug_ref[hd, 0:HEAD_DIM, :] = qrt_bf
        own_scores.append(jnp.dot(kaug_ref[hd, rows, 0:HEAD_DIM], qrt_bf, preferred_element_type=F32))

        gate = _dot3(kmean_ref[hd], qrt, NN_DIMS)
        g = jnp.where(valid, gate, -jnp.inf)
        bias = jnp.full((kb_pad, blk), NEG, F32)
        for _ in range(MOBA_TOPK):
            top = jnp.max(g, axis=0, keepdims=True)
            idx = jnp.min(jnp.where(g == top, sub, kb_pad), axis=0, keepdims=True)
            pick = sub == idx
            bias = jnp.where(pick, jnp.where(valid, 0.0, bias), bias)
            g = jnp.where(pick, -jnp.inf, g)
        qaug_ref[hd, HEAD_DIM:HEAD_DIM + kb_pad, :] = bias.astype(BF16)

    def scores(trip, s_ref):
        keys = pl.ds(pl.multiple_of(trip * 2 * blk, 2 * blk), 2 * blk)
        for hd in heads:
            s_ref[hd] = jnp.dot(kaug_ref[hd, keys, :], qaug_ref[hd], preferred_element_type=F32)

    def consume(trip, s_ref):
        j0 = 2 * trip
        pbs, alphas = [], []
        for hd in heads:
            sj = s_ref[hd]
            m_prev = m_ref[hd]
            m_new = jnp.maximum(m_prev, jnp.max(sj, axis=0, keepdims=True))
            alpha = jnp.exp2(m_prev - m_new)
            pj = jnp.exp2(sj - m_new)
            l_ref[hd] = alpha * l_ref[hd] + jnp.sum(pj, axis=0, keepdims=True)
            m_ref[hd] = m_new
            pbs.append(pj.astype(BF16))
            alphas.append(alpha)
        for hd in heads:
            pv = (jnp.dot(vt_ref[hd, j0], pbs[hd][0:blk], preferred_element_type=F32)
                  + jnp.dot(vt_ref[hd, j0 + 1], pbs[hd][blk:2 * blk], preferred_element_type=F32))
            acc_ref[hd] = alphas[hd] * acc_ref[hd] + pv

    def trip_ahead(trip, cur_ref, nxt_ref):
        scores(trip + 1, nxt_ref)
        consume(trip, cur_ref)

    scores(0, s0_ref)
    own_ps = []
    for hd in heads:
        s = jnp.where(key_i <= qry_i, own_scores[hd], NEG)
        m0 = jnp.max(s, axis=0, keepdims=True)
        p = jnp.exp2(s - m0)
        m_ref[hd] = m0
        l_ref[hd] = jnp.sum(p, axis=0, keepdims=True)
        own_ps.append(p.astype(BF16))
    for hd in heads:
        acc_ref[hd] = jnp.dot(vt_ref[hd, qi], own_ps[hd], preferred_element_type=F32)

    n_trips = (qi + 1) // 2

    @pl.when(n_trips > 0)
    def _():
        def two_trips(r, carry):
            trip_ahead(2 * r, s0_ref, s1_ref)
            trip_ahead(2 * r + 1, s1_ref, s0_ref)
            return carry

        lax.fori_loop(0, (n_trips - 1) // 2, two_trips, 0)
        last = n_trips - 1

        @pl.when(n_trips % 2 == 1)
        def _():
            consume(last, s0_ref)

        @pl.when(n_trips % 2 == 0)
        def _():
            trip_ahead(last - 1, s0_ref, s1_ref)
            consume(last, s1_ref)

    for hd in heads:
        o_ref[:, hcols(hd)] = (acc_ref[hd] / l_ref[hd]).T.astype(o_ref.dtype)


def _attention(proj, rope_c, rope_sa, rope_sb, batch, seq, n_heads, q_col, k_col, v_col):
    t = proj.shape[0]
    blk = MOBA_BLOCK
    n_kb = seq // blk
    kb_pad = -(-n_kb // V7X_BF16_SUBLANES) * V7X_BF16_SUBLANES
    n_hd = ATTN_HEADS_PER_STEP
    wide = n_hd * HEAD_DIM
    assert n_kb % 2 == 0 and kb_pad <= HEAD_DIM and n_heads % n_hd == 0
    assert q_col % n_hd == 0 and k_col % n_hd == 0 and v_col % n_hd == 0
    groups = n_heads // n_hd
    kernel = functools.partial(_attn_kernel, n_kb=n_kb, kb_pad=kb_pad, n_hd=n_hd,
                               exp_scale=HEAD_DIM ** -0.5 * math.log2(math.e))
    tab = pl.BlockSpec((seq, HEAD_DIM), lambda bg, qi: (0, 0))
    return pl.pallas_call(
        kernel,
        grid=(batch * groups, n_kb),
        in_specs=[
            pl.BlockSpec((blk, wide), lambda bg, qi: ((bg // groups) * n_kb + qi, q_col // n_hd + bg % groups)),
            pl.BlockSpec((seq, wide), lambda bg, qi: (bg // groups, k_col // n_hd + bg % groups)),
            pl.BlockSpec((seq, wide), lambda bg, qi: (bg // groups, v_col // n_hd + bg % groups)),
            tab, tab, tab,
        ],
        out_specs=pl.BlockSpec((blk, wide), lambda bg, qi: ((bg // groups) * n_kb + qi, bg % groups)),
        out_shape=jax.ShapeDtypeStruct((t, n_heads * HEAD_DIM), BF16),
        scratch_shapes=[
            pltpu.VMEM((n_hd, seq, 2 * HEAD_DIM), BF16),
            pltpu.VMEM((n_hd, n_kb, HEAD_DIM, blk), BF16),
            pltpu.VMEM((n_hd, kb_pad, HEAD_DIM), F32),
            pltpu.VMEM((n_hd, 1, blk), F32),
            pltpu.VMEM((n_hd, 1, blk), F32),
            pltpu.VMEM((n_hd, HEAD_DIM, blk), F32),
            pltpu.VMEM((n_hd, 2 * HEAD_DIM, blk), BF16),
            pltpu.VMEM((n_hd, 2 * blk, blk), F32),
            pltpu.VMEM((n_hd, 2 * blk, blk), F32),
        ],
        compiler_params=_params("arbitrary", "arbitrary"),
        name="moba_attention",
    )(proj, proj, proj, rope_c, rope_sa, rope_sb)


def _merge_kernel(b_ref, c_ref, h_ref, cp_ref, hp_ref, gc_ref, ga_ref, at_ref, cw_ref, wc_ref, wa_ref,
                  o_ref, *, tiles_per_seq):
    hal = V7X_BF16_SUBLANES
    u = c_ref[...].astype(F32) * h_ref[...].astype(F32)
    up = cp_ref[...].astype(F32) * hp_ref[...].astype(F32)
    up = jnp.where(pl.program_id(0) % tiles_per_seq == 0, 0.0, up)
    row = lax.broadcasted_iota(jnp.int32, u.shape, 0)
    u1 = jnp.where(row == 0, up[hal - 1:hal, :], pltpu.roll(u, 1, 0))
    u2 = jnp.where(row == 0, up[hal - 2:hal - 1, :],
                   jnp.where(row == 1, up[hal - 1:hal, :], pltpu.roll(u, 2, 0)))
    y = cw_ref[0:1, :] * u2 + cw_ref[1:2, :] * u1 + cw_ref[2:3, :] * u
    cm = (b_ref[...].astype(F32) * y).astype(BF16)
    yc = jnp.dot(cm, wc_ref[...], preferred_element_type=F32)
    ya = jnp.dot(at_ref[...], wa_ref[...], preferred_element_type=F32)
    merged = jax.nn.sigmoid(gc_ref[...].astype(F32)) * yc + jax.nn.sigmoid(ga_ref[...].astype(F32)) * ya
    o_ref[...] = merged.astype(o_ref.dtype)


def _merge(proj, attn, conv_w, wc_bf, wa_bf, seq, cw, aw, d, tm):
    t = proj.shape[0]
    hal = V7X_BF16_SUBLANES
    kernel = functools.partial(_merge_kernel, tiles_per_seq=seq // tm)
    col = lambda cidx: pl.BlockSpec((tm, cw), lambda i: (i, cidx))
    halo = lambda cidx: pl.BlockSpec((hal, cw), lambda i: (jnp.maximum(i * (tm // hal) - 1, 0), cidx))
    const = lambda shape: pl.BlockSpec(shape, lambda i: (0, 0))
    return pl.pallas_call(
        kernel,
        grid=(t // tm,),
        in_specs=[col(0), col(1), col(2), halo(1), halo(2), col(6), col(7),
                  pl.BlockSpec((tm, aw), lambda i: (i, 0)),
                  const((CONV_K, cw)), const((cw, d)), const((aw, d))],
        out_specs=pl.BlockSpec((tm, d), lambda i: (i, 0)),
        out_shape=jax.ShapeDtypeStruct((t, d), BF16),
        compiler_params=_params("arbitrary"),
        name="merge_branches",
    )(proj, proj, proj, proj, proj, proj, proj, attn, conv_w, wc_bf, wa_bf)


def _oproj_kernel(mg_ref, x_ref, wo_ref, g_ref, wrt_ref, br_ref,
                  h1_ref, xn_ref, te_ref, tw_ref, rk_ref, cnt_ref, run_ref):
    h1 = x_ref[...] + jnp.dot(mg_ref[...], wo_ref[...], preferred_element_type=F32)
    h1_ref[...] = h1
    ms = jnp.mean(h1 * h1, axis=-1, keepdims=True)
    xn = h1 * lax.rsqrt(ms + NORM_EPS) * g_ref[...]
    xn_ref[...] = xn
    lg = _dot3(wrt_ref[...], xn, NT_DIMS) + br_ref[...]
    n_e, tm = lg.shape
    sub = lax.broadcasted_iota(jnp.int32, lg.shape, 0)
    vals, idxs = [], []
    for _ in range(TOP_K):
        top = jnp.max(lg, axis=0, keepdims=True)
        idx = jnp.min(jnp.where(lg == top, sub, n_e), axis=0, keepdims=True)
        vals.append(top)
        idxs.append(idx)
        lg = jnp.where(sub == idx, -jnp.inf, lg)
    v = jnp.concatenate(vals, axis=0)
    ex = jnp.exp(v - vals[0])
    te_ref[...] = jnp.concatenate(idxs, axis=0)
    tw_ref[...] = ex / jnp.sum(ex, axis=0, keepdims=True)

    @pl.when(pl.program_id(0) == 0)
    def _():
        run_ref[...] = jnp.zeros_like(run_ref)

    upper = jnp.where(lax.broadcasted_iota(jnp.int32, (tm, tm), 0) <= lax.broadcasted_iota(jnp.int32, (tm, tm), 1),
                      1.0, 0.0).astype(BF16)
    run = run_ref[...]
    ranks = []
    for k in range(TOP_K):
        onehot = jnp.where(sub == idxs[k], 1.0, 0.0)
        incl = jnp.dot(onehot.astype(BF16), upper, preferred_element_type=F32)
        ranks.append(jnp.sum(onehot * (incl - 1.0 + run[:, 0:1]), axis=0, keepdims=True))
        run = run + incl[:, tm - 1:tm]
    run_ref[...] = run
    rk_ref[...] = jnp.concatenate(ranks, axis=0).astype(jnp.int32)
    cnt_ref[...] = run.astype(jnp.int32)


def _oproj(merged, x2, wo_bf, g, wr_t, br, tm):
    t, d = x2.shape
    n_e = wr_t.shape[0]
    const = lambda shape: pl.BlockSpec(shape, lambda i: (0, 0))
    row = pl.BlockSpec((tm, d), lambda i: (i, 0))
    top = pl.BlockSpec((TOP_K, tm), lambda i: (0, i))
    return pl.pallas_call(
        _oproj_kernel,
        grid=(t // tm,),
        in_specs=[row, row, const((d, d)), const((1, d)), const((n_e, d)), const((n_e, 1))],
        out_specs=[row, row, top, top, top, const((n_e, V7X_LANES))],
        out_shape=[jax.ShapeDtypeStruct((t, d), F32), jax.ShapeDtypeStruct((t, d), F32),
                   jax.ShapeDtypeStruct((TOP_K, t), jnp.int32), jax.ShapeDtypeStruct((TOP_K, t), F32),
                   jax.ShapeDtypeStruct((TOP_K, t), jnp.int32), jax.ShapeDtypeStruct((n_e, V7X_LANES), jnp.int32)],
        scratch_shapes=[pltpu.VMEM((n_e, V7X_LANES), F32)],
        compiler_params=_params("arbitrary"),
        name="oproj_router",
    )(merged, x2, wo_bf, g.reshape(1, d), wr_t, br.reshape(n_e, 1))


def _dispatch_kernel(dest_ref, fill_ref, x_hbm, xs_ref, zero_ref, x_buf, sem, lsem, zsem,
                     *, n_e, n_blocks, tm_moe, tm):
    @pl.when(pl.program_id(0) == 0)
    def _():
        zero_ref[...] = jnp.zeros_like(zero_ref)

        def zcopy(b):
            return pltpu.make_async_copy(
                zero_ref, xs_ref.at[pl.ds(pl.multiple_of(b * tm_moe, tm_moe), tm_moe), :], zsem)

        def zstart(b, carry):
            zcopy(b).start()
            return carry

        def zwait(b, carry):
            zcopy(b).wait()
            return carry

        for e in range(n_e):
            @pl.when(fill_ref[e] >= 0)
            def _():
                zcopy(fill_ref[e]).start()
        lax.fori_loop(fill_ref[n_e], n_blocks, zstart, 0)
        for e in range(n_e):
            @pl.when(fill_ref[e] >= 0)
            def _():
                zcopy(fill_ref[e]).wait()
        lax.fori_loop(fill_ref[n_e], n_blocks, zwait, 0)

    i = pl.program_id(0)
    groups = tm // ROW_GROUP

    def load(tile, slot):
        rows = pl.ds(pl.multiple_of(tile * groups, groups), groups)
        return pltpu.make_async_copy(x_hbm.at[rows], x_buf.at[slot], lsem.at[slot])

    def scatters(slot):
        def copy(k, g, j):
            row = dest_ref[k * tm + g * ROW_GROUP + j]
            return pltpu.make_async_copy(x_buf.at[slot, g, pl.ds(j, 1), :], xs_ref.at[pl.ds(row, 1), :],
                                         sem.at[slot])
        return copy

    def step(slot):
        nxt, prv = (slot + 1) % DISPATCH_BUFFERS, (slot - 1) % DISPATCH_BUFFERS

        @pl.when(i == 0)
        def _():
            load(i, slot).start()

        @pl.when(i + 1 < pl.num_programs(0))
        def _():
            load(i + 1, nxt).start()

        load(i, slot).wait()
        _start_row_copies(tm, scatters(slot))

        @pl.when(i > 0)
        def _():
            _wait_row_copies(tm, scatters(prv))

        @pl.when(i + 1 == pl.num_programs(0))
        def _():
            _wait_row_copies(tm, scatters(slot))

    for slot in range(DISPATCH_BUFFERS):
        pl.when(i % DISPATCH_BUFFERS == slot)(functools.partial(step, slot))


DISPATCH_BUFFERS = 3

ROW_GROUP = 8


def _start_row_copies(tm, copy):
    def start(g, carry):
        for j in range(ROW_GROUP):
            for k in range(TOP_K):
                copy(k, g, j).start(priority=(j * TOP_K + k) % 2)
        return carry

    lax.fori_loop(0, tm // ROW_GROUP, start, 0)


def _wait_row_copies(tm, copy):
    def wait(g, carry):
        for j in range(ROW_GROUP):
            for k in range(TOP_K):
                copy(k, g, j).wait()
        return carry

    lax.fori_loop(0, tm // ROW_GROUP, wait, 0)


def _for_each_row_copy(tm, copy):
    _start_row_copies(tm, copy)
    _wait_row_copies(tm, copy)


def _tile_major(dest, tm):
    k, t = dest.shape
    return dest.reshape(k, t // tm, tm).transpose(1, 0, 2).reshape(-1)


def _dispatch(dest_tiles, fill, xn, n_blocks, tm_moe, tm):
    t, d = xn.shape
    n_e = fill.shape[0] - 1
    kernel = functools.partial(_dispatch_kernel, n_e=n_e, n_blocks=n_blocks, tm_moe=tm_moe, tm=tm)
    return pl.pallas_call(
        kernel,
        grid=(t // tm,),
        in_specs=[pl.BlockSpec((TOP_K * tm,), lambda i: (i,), memory_space=pltpu.SMEM),
                  pl.BlockSpec(memory_space=pltpu.SMEM),
                  pl.BlockSpec(memory_space=pl.ANY)],
        out_specs=pl.BlockSpec(memory_space=pl.ANY),
        out_shape=jax.ShapeDtypeStruct((n_blocks * tm_moe, d), F32),
        scratch_shapes=[pltpu.VMEM((tm_moe, d), F32),
                        pltpu.VMEM((DISPATCH_BUFFERS, tm // ROW_GROUP, ROW_GROUP, d), F32),
                        pltpu.SemaphoreType.DMA((DISPATCH_BUFFERS,)), pltpu.SemaphoreType.DMA((DISPATCH_BUFFERS,)),
                        pltpu.SemaphoreType.DMA(())],
        compiler_params=_params("arbitrary"),
        name="moe_dispatch",
    )(dest_tiles, fill, xn.reshape(t // ROW_GROUP, ROW_GROUP, d))


WEIGHT_K_CHUNKS = 4


def _dot_rounding_tile(x, stage_ref, slot, which, bf_ref):
    kc = bf_ref.shape[0] // WEIGHT_K_CHUNKS
    acc = None
    for c in range(WEIGHT_K_CHUNKS):
        rows = slice(c * kc, (c + 1) * kc)
        w = stage_ref[slot, which, rows, :].astype(BF16)
        bf_ref[rows, :] = w
        part = jnp.dot(x[:, rows], w, preferred_element_type=F32)
        acc = part if acc is None else acc + part
    return acc


N_ITEM_ARRAYS = 11


def _expert_items(blk_start, blk_count, n_col, n_blocks):
    n_e = blk_count.shape[0]
    n_items = n_blocks * n_col
    item_end = jnp.cumsum(blk_count * n_col)
    item_start = item_end - blk_count * n_col
    n_used = item_end[-1]
    blocks_used = jnp.sum(blk_count)
    i = jnp.arange(n_items, dtype=jnp.int32)
    ok = i < n_used
    nonempty_before = jnp.cumsum(blk_count > 0) - (blk_count > 0)
    experts = jnp.arange(n_e, dtype=jnp.int32)

    def at(idx):
        ic = jnp.clip(idx, 0, jnp.maximum(n_used - 1, 0))
        e = jnp.sum(ic[:, None] >= item_end[None, :], axis=1).astype(jnp.int32)
        mine = e[:, None] == experts[None, :]
        pick = lambda table: jnp.sum(jnp.where(mine, table[None, :], 0), axis=1)
        local = ic - pick(item_start)
        cnt = jnp.maximum(pick(blk_count), 1)
        first_blk = pick(blk_start)
        return e, local // cnt, first_blk + local % cnt, cnt, first_blk, pick(nonempty_before)

    e, wcol, xblk, cnt, first_blk, groups_before = at(i)
    spare = i - n_used
    oblk = jnp.where(ok, xblk, blocks_used + spare // n_col)
    ocol = jnp.where(ok, wcol, spare % n_col)
    new = ok & (xblk == first_blk)
    slot = (groups_before * n_col + wcol) % 2
    nxt = i + cnt
    nxt_e, nxt_col = at(nxt)[:2]
    has_nxt = new & (nxt < n_used)
    as_i32 = lambda a: a.astype(jnp.int32)
    items = tuple(map(as_i32, (xblk, oblk, wcol, ocol, e, new, ok, slot, nxt_e, nxt_col, has_nxt)))
    assert len(items) == N_ITEM_ARRAYS and n_e == item_end.shape[0]
    return items


def _weight_tile_copy(w_hbm, stage_ref, sem, e, col, slot, which, tn):
    cols = pl.ds(pl.multiple_of(col * tn, tn), tn)
    return pltpu.make_async_copy(w_hbm.at[e, :, cols], stage_ref.at[slot, which], sem.at[slot, which])


def _stream_weight_tiles(i, items, w_hbms, stage_ref, sem, tn):
    _, _, wcol, _, exp, new, _, slot, nxt_e, nxt_col, has_nxt = items

    @pl.when(new[i] == 1)
    def _():
        s = slot[i]

        @pl.when(i == 0)
        def _():
            for which, w in enumerate(w_hbms):
                _weight_tile_copy(w, stage_ref, sem, exp[i], wcol[i], s, which, tn).start(priority=1)

        for which, w in enumerate(w_hbms):
            _weight_tile_copy(w, stage_ref, sem, exp[i], wcol[i], s, which, tn).wait()

        @pl.when(has_nxt[i] == 1)
        def _():
            for which, w in enumerate(w_hbms):
                _weight_tile_copy(w, stage_ref, sem, nxt_e[i], nxt_col[i], 1 - s, which, tn).start(priority=1)


def _up_kernel(*refs, n_col):
    items = refs[:N_ITEM_ARRAYS]
    x_ref, wg_hbm, wu_hbm, bg_ref, bu_ref, o_ref, stage_ref, wgb_ref, wub_ref, sem = refs[N_ITEM_ARRAYS:]
    wcol, exp, new, ok, slot = items[2], items[4], items[5], items[6], items[7]
    i = pl.program_id(0)
    _stream_weight_tiles(i, items, (wg_hbm, wu_hbm), stage_ref, sem, wgb_ref.shape[1])
    tile = exp[i] * n_col + wcol[i]

    def finish(g, u):
        g = jnp.minimum(g + bg_ref[pl.ds(tile, 1), :], SWIGLU_LIMIT)
        u = jnp.clip(u + bu_ref[pl.ds(tile, 1), :], -SWIGLU_LIMIT, SWIGLU_LIMIT)
        o_ref[...] = ((u + 1.0) * (g * jax.nn.sigmoid(SWIGLU_ALPHA * g))).astype(o_ref.dtype)

    @pl.when(new[i] == 1)
    def _():
        x = x_ref[...].astype(BF16)
        finish(_dot_rounding_tile(x, stage_ref, slot[i], 0, wgb_ref),
               _dot_rounding_tile(x, stage_ref, slot[i], 1, wub_ref))

    @pl.when(ok[i] - new[i] == 1)
    def _():
        x = x_ref[...].astype(BF16)
        finish(jnp.dot(x, wgb_ref[...], preferred_element_type=F32),
               jnp.dot(x, wub_ref[...], preferred_element_type=F32))

    @pl.when(ok[i] == 0)
    def _():
        o_ref[...] = jnp.zeros_like(o_ref)


def _down_kernel(*refs, n_col):
    items = refs[:N_ITEM_ARRAYS]
    h_ref, wd_hbm, bd_ref, o_ref, stage_ref, wdb_ref, sem = refs[N_ITEM_ARRAYS:]
    wcol, exp, new, ok, slot = items[2], items[4], items[5], items[6], items[7]
    i = pl.program_id(0)
    _stream_weight_tiles(i, items, (wd_hbm,), stage_ref, sem, wdb_ref.shape[1])
    tile = exp[i] * n_col + wcol[i]

    @pl.when(new[i] == 1)
    def _():
        o_ref[...] = _dot_rounding_tile(h_ref[...], stage_ref, slot[i], 0, wdb_ref) + bd_ref[pl.ds(tile, 1), :]

    @pl.when(ok[i] - new[i] == 1)
    def _():
        o_ref[...] = jnp.dot(h_ref[...], wdb_ref[...], preferred_element_type=F32) + bd_ref[pl.ds(tile, 1), :]

    @pl.when(ok[i] == 0)
    def _():
        o_ref[...] = jnp.zeros_like(o_ref)


def _item_map(fn):
    return lambda i, *items: fn(i, *items[:5])


def _expert_up(items, xs, w_gate, b_gate, w_up, b_up, tm, tf):
    rows, d = xs.shape
    n_e, _, f = w_gate.shape
    n_items = items[0].shape[0]
    n_col = f // tf
    bspec = pl.BlockSpec((n_e * n_col, tf), lambda i, *_: (0, 0))
    hbm = pl.BlockSpec(memory_space=pl.ANY)
    grid_spec = pltpu.PrefetchScalarGridSpec(
        num_scalar_prefetch=N_ITEM_ARRAYS,
        grid=(n_items,),
        in_specs=[pl.BlockSpec((tm, d), _item_map(lambda i, xb, ob, wc, oc, e: (xb[i], 0))),
                  hbm, hbm, bspec, bspec],
        out_specs=pl.BlockSpec((tm, tf), _item_map(lambda i, xb, ob, wc, oc, e: (ob[i], oc[i]))),
        scratch_shapes=[pltpu.VMEM((2, 2, d, tf), F32), pltpu.VMEM((d, tf), BF16), pltpu.VMEM((d, tf), BF16),
                        pltpu.SemaphoreType.DMA((2, 2))],
    )
    return pl.pallas_call(
        functools.partial(_up_kernel, n_col=n_col),
        grid_spec=grid_spec,
        out_shape=jax.ShapeDtypeStruct((rows, f), BF16),
        compiler_params=_params("arbitrary"),
        name="moe_up",
    )(*items, xs, w_gate, w_up, b_gate.reshape(n_e * n_col, tf), b_up.reshape(n_e * n_col, tf))


def _expert_down(items, hid, w_down, b_down, tm, tn):
    rows, f = hid.shape
    n_e, _, d = w_down.shape
    n_items = items[0].shape[0]
    n_col = d // tn
    grid_spec = pltpu.PrefetchScalarGridSpec(
        num_scalar_prefetch=N_ITEM_ARRAYS,
        grid=(n_items,),
        in_specs=[pl.BlockSpec((tm, f), _item_map(lambda i, xb, ob, wc, oc, e: (xb[i], 0))),
                  pl.BlockSpec(memory_space=pl.ANY),
                  pl.BlockSpec((n_e * n_col, tn), lambda i, *_: (0, 0))],
        out_specs=pl.BlockSpec((tm, tn), _item_map(lambda i, xb, ob, wc, oc, e: (ob[i], oc[i]))),
        scratch_shapes=[pltpu.VMEM((2, 1, f, tn), F32), pltpu.VMEM((f, tn), BF16),
                        pltpu.SemaphoreType.DMA((2, 1))],
    )
    return pl.pallas_call(
        functools.partial(_down_kernel, n_col=n_col),
        grid_spec=grid_spec,
        out_shape=jax.ShapeDtypeStruct((rows, d), F32),
        compiler_params=_params("arbitrary"),
        name="moe_down",
    )(*items, hid, w_down, b_down.reshape(n_e * n_col, tn))


def _combine_kernel(dest_ref, next_ref, h1_ref, w_ref, g_ref, ys_ref, o_ref, buf_ref, sem):
    tm, d = h1_ref.shape
    i = pl.program_id(0)

    def copies(idx_ref, slot):
        def copy(k, g, j):
            row = idx_ref[k * tm + g * ROW_GROUP + j]
            return pltpu.make_async_copy(ys_ref.at[pl.ds(row, 1), :], buf_ref.at[slot, k, g, pl.ds(j, 1), :],
                                         sem.at[slot])
        return copy

    def step(slot):
        @pl.when(i == 0)
        def _():
            _start_row_copies(tm, copies(dest_ref, slot))

        @pl.when(i + 1 < pl.num_programs(0))
        def _():
            _start_row_copies(tm, copies(next_ref, 1 - slot))

        _wait_row_copies(tm, copies(dest_ref, slot))
        h2 = h1_ref[...]
        for k in range(TOP_K):
            h2 = h2 + w_ref[:, k:k + 1] * buf_ref[slot, k].reshape(tm, d)
        ms = jnp.mean(h2 * h2, axis=-1, keepdims=True)
        o_ref[...] = h2 * lax.rsqrt(ms + NORM_EPS) * g_ref[...]

    for slot in range(2):
        pl.when(i % 2 == slot)(functools.partial(step, slot))


def _combine(dest_tiles, h1, w_tk, g, ys, tm):
    t, d = h1.shape
    last = t // tm - 1
    return pl.pallas_call(
        _combine_kernel,
        grid=(t // tm,),
        in_specs=[pl.BlockSpec((TOP_K * tm,), lambda i: (i,), memory_space=pltpu.SMEM),
                  pl.BlockSpec((TOP_K * tm,), lambda i: (jnp.minimum(i + 1, last),), memory_space=pltpu.SMEM),
                  pl.BlockSpec((tm, d), lambda i: (i, 0)),
                  pl.BlockSpec((tm, TOP_K), lambda i: (i, 0)),
                  pl.BlockSpec((1, d), lambda i: (0, 0)),
                  pl.BlockSpec(memory_space=pl.ANY)],
        out_specs=pl.BlockSpec((tm, d), lambda i: (i, 0)),
        out_shape=jax.ShapeDtypeStruct((t, d), F32),
        scratch_shapes=[pltpu.VMEM((2, TOP_K, tm // ROW_GROUP, ROW_GROUP, d), F32),
                        pltpu.SemaphoreType.DMA((2,))],
        compiler_params=_params("arbitrary"),
        name="moe_combine",
    )(dest_tiles, dest_tiles, h1, w_tk, g.reshape(1, d), ys)


def _rope_tables(seq):
    half = ROPE_DIM // 2
    inv_freq = jnp.exp(-math.log(ROPE_THETA) * jnp.arange(half, dtype=F32) / half)
    ang = jnp.arange(seq, dtype=jnp.int32).astype(F32)[:, None] * inv_freq[None, :]
    cos, sin = jnp.cos(ang), jnp.sin(ang)
    zeros = jnp.zeros((seq, HEAD_DIM - ROPE_DIM), F32)
    zh = jnp.zeros((seq, half), F32)
    c = jnp.concatenate([cos, cos, jnp.ones_like(zeros)], axis=1)
    sa = jnp.concatenate([-sin, zh, zeros], axis=1)
    sb = jnp.concatenate([zh, sin, zeros], axis=1)
    return c, sa, sb


def _dest_kernel(base_ref, te_ref, rk_ref, o_ref):
    te = te_ref[...]
    rows = rk_ref[...]
    for e in range(base_ref.shape[0]):
        rows = rows + jnp.where(te == e, base_ref[e], 0)
    o_ref[...] = rows


def _dest_rows(base, top_e, rank, tl):
    k, t = top_e.shape
    blk = pl.BlockSpec((k, tl), lambda i: (0, i))
    return pl.pallas_call(
        _dest_kernel,
        grid=(t // tl,),
        in_specs=[pl.BlockSpec(memory_space=pltpu.SMEM), blk, blk],
        out_specs=blk,
        out_shape=jax.ShapeDtypeStruct((k, t), jnp.int32),
        compiler_params=_params("arbitrary"),
        name="moe_dest_rows",
    )(base, top_e, rank)


def _routing_tables(counts, tm):
    blk_count = (counts + tm - 1) // tm
    blk_end = jnp.cumsum(blk_count)
    blk_start = blk_end - blk_count
    fill = jnp.concatenate([jnp.where(blk_count > 0, blk_end - 1, -1), blk_end[-1:]])
    as_i32 = lambda a: a.astype(jnp.int32)
    return as_i32(blk_start), as_i32(blk_count), as_i32(blk_start * tm), as_i32(fill)


def kernel(x, norm_mix, w_in, conv_w, w_conv_out, w_attn_out, w_o, norm_ffn, w_router, b_router,
           w_gate, b_gate, w_up, b_up, w_down, b_down, norm_final):
    batch, seq, d = x.shape
    cw = w_conv_out.shape[1]
    aw = w_attn_out.shape[1]
    n_heads = aw // HEAD_DIM
    n_e = w_router.shape[-1]
    f = w_gate.shape[-1]
    t = batch * seq
    assert w_in.shape[0] == 1, "single layer: the final rmsnorm is fused into the MoE combine"
    assert cw == d and aw == d and seq % MOBA_BLOCK == 0

    tm_norm = min(512, t)
    tm_in, tn_in = min(1024, t), min(1024, w_in.shape[-1])
    tm_mid = min(256, seq)
    tm_moe = min(256, t)
    tf_up = min(1024, f)
    tn_down = min(2048, d)
    tm_disp = min(256, t)
    tm_comb = min(256, t)
    n_blocks = (t * TOP_K + n_e * (tm_moe - 1)) // tm_moe
    col = lambda width_off: width_off // HEAD_DIM

    rope_c, rope_sa, rope_sb = _rope_tables(seq)
    h = x.reshape(t, d)
    xn = _rmsnorm(h, norm_mix[0], tm_norm, BF16)
    proj = _inproj(xn, w_in[0], tm_in, tn_in)
    attn = _attention(proj, rope_c, rope_sa, rope_sb, batch, seq, n_heads,
                      col(3 * cw), col(3 * cw + aw), col(3 * cw + 2 * aw))
    merged = _merge(proj, attn, conv_w[0], w_conv_out[0].astype(BF16), w_attn_out[0].astype(BF16),
                    seq, cw, aw, d, tm_mid)
    h1, xn2, top_e, top_w, rank, counts = _oproj(merged, h, w_o[0].astype(BF16), norm_ffn[0],
                                                 w_router[0].T, b_router[0], tm_mid)
    blk_start, blk_count, base, fill = _routing_tables(counts[:, 0], tm_moe)
    dest = _dest_rows(base, top_e, rank, min(2048, t))
    xs = _dispatch(_tile_major(dest, tm_disp), fill, xn2, n_blocks, tm_moe, tm_disp)
    up_items = _expert_items(blk_start, blk_count, f // tf_up, n_blocks)
    hid = _expert_up(up_items, xs, w_gate[0], b_gate[0], w_up[0], b_up[0], tm_moe, tf_up)
    same_tiling = f // tf_up == d // tn_down
    down_items = up_items if same_tiling else _expert_items(blk_start, blk_count, d // tn_down, n_blocks)
    ys = _expert_down(down_items, hid, w_down[0], b_down[0], tm_moe, tn_down)
    out = _combine(_tile_major(dest, tm_comb), h1, top_w.T, norm_final, ys, tm_comb)
    return out.reshape(batch, seq, d)
```

```python
import functools
import math

import jax
import jax.numpy as jnp
from jax import lax
from jax.experimental import pallas as pl
from jax.experimental.pallas import tpu as pltpu

HEAD_DIM = 128
ROPE_DIM = HEAD_DIM // 4
ROPE_THETA = 500000.0
MOBA_BLOCK = 256
MOBA_TOPK = 3
CONV_K = 3
TOP_K = 4
SWIGLU_LIMIT = 7.0
SWIGLU_ALPHA = 1.702
NORM_EPS = 1e-5
NEG = -1e30
ATTN_HEADS_PER_STEP = 4

V7X_LANES = 128
V7X_BF16_SUBLANES = 16
V7X_VMEM_BYTES = 64 * 1024 * 1024
VMEM_LIMIT = V7X_VMEM_BYTES - 8 * 1024 * 1024

F32 = jnp.float32
BF16 = jnp.bfloat16
NT_DIMS = (((1,), (1,)), ((), ()))
NN_DIMS = (((1,), (0,)), ((), ()))


def _params(*sem):
    return pltpu.CompilerParams(dimension_semantics=sem, vmem_limit_bytes=VMEM_LIMIT)


def _split_bf16(a):
    hi = a.astype(BF16)
    return hi, (a - hi.astype(F32)).astype(BF16)


def _dot3(a, b, dims):
    ah, al = _split_bf16(a)
    bh, bl = _split_bf16(b)
    rows = a.shape[0]
    dot = lambda u, v: lax.dot_general(u, v, dims, preferred_element_type=F32)
    both = dot(jnp.concatenate([ah, al], axis=0), bh)
    return both[:rows] + (both[rows:] + dot(ah, bl))


def _rmsnorm_kernel(x_ref, g_ref, o_ref):
    x = x_ref[...]
    ms = jnp.mean(x * x, axis=-1, keepdims=True)
    o_ref[...] = (x * lax.rsqrt(ms + NORM_EPS) * g_ref[...]).astype(o_ref.dtype)


def _rmsnorm(x, g, tm, out_dtype):
    t, d = x.shape
    return pl.pallas_call(
        _rmsnorm_kernel,
        grid=(t // tm,),
        in_specs=[pl.BlockSpec((tm, d), lambda i: (i, 0)), pl.BlockSpec((1, d), lambda i: (0, 0))],
        out_specs=pl.BlockSpec((tm, d), lambda i: (i, 0)),
        out_shape=jax.ShapeDtypeStruct((t, d), out_dtype),
        compiler_params=_params("arbitrary"),
        name="rmsnorm",
    )(x, g.reshape(1, d))


def _inproj_kernel(x_ref, w_ref, o_ref, wbf_ref):
    @pl.when(pl.program_id(1) == 0)
    def _():
        wbf_ref[...] = w_ref[...].astype(BF16)

    o_ref[...] = jnp.dot(x_ref[...], wbf_ref[...], preferred_element_type=F32).astype(o_ref.dtype)


def _inproj(xn, w, tm, tn):
    t, d = xn.shape
    n = w.shape[1]
    return pl.pallas_call(
        _inproj_kernel,
        grid=(n // tn, t // tm),
        in_specs=[pl.BlockSpec((tm, d), lambda j, i: (i, 0)), pl.BlockSpec((d, tn), lambda j, i: (0, j))],
        out_specs=pl.BlockSpec((tm, tn), lambda j, i: (i, j)),
        out_shape=jax.ShapeDtypeStruct((t, n), BF16),
        scratch_shapes=[pltpu.VMEM((d, tn), BF16)],
        compiler_params=_params("arbitrary", "arbitrary"),
        name="inproj",
    )(xn, w)


def _rope(t, c, sa, sb):
    half = ROPE_DIM // 2
    return t * c + pltpu.roll(t, HEAD_DIM - half, 1) * sa + pltpu.roll(t, half, 1) * sb


def _attn_kernel(q_ref, k_ref, v_ref, c_ref, sa_ref, sb_ref, o_ref,
                 kaug_ref, vt_ref, kmean_ref, m_ref, l_ref, acc_ref, qaug_ref, s0_ref, s1_ref,
                 *, n_kb, kb_pad, n_hd, exp_scale):
    blk = MOBA_BLOCK
    qi = pl.program_id(1)
    heads = range(n_hd)
    hcols = lambda hd: slice(hd * HEAD_DIM, (hd + 1) * HEAD_DIM)

    @pl.when(qi == 0)
    def _prep():
        kmean_ref[...] = jnp.zeros_like(kmean_ref)
        qaug_ref[...] = jnp.zeros_like(qaug_ref)
        lane = lax.broadcasted_iota(jnp.int32, (blk, HEAD_DIM), 1)

        def body(j, carry):
            rows = pl.ds(pl.multiple_of(j * blk, blk), blk)
            onehot = jnp.where(lane == j, 1.0, 0.0).astype(BF16)
            for hd in heads:
                kr = _rope(k_ref[rows, hcols(hd)].astype(F32), c_ref[rows, :], sa_ref[rows, :], sb_ref[rows, :])
                kaug_ref[hd, rows, 0:HEAD_DIM] = kr.astype(BF16)
                kaug_ref[hd, rows, HEAD_DIM:2 * HEAD_DIM] = onehot
                kmean_ref[hd, pl.ds(j, 1), :] = jnp.mean(kr, axis=0, keepdims=True)
                vt_ref[hd, j] = v_ref[rows, hcols(hd)].astype(F32).T.astype(BF16)
            return carry

        lax.fori_loop(0, n_kb, body, 0)

    rows = pl.ds(pl.multiple_of(qi * blk, blk), blk)
    sub = lax.broadcasted_iota(jnp.int32, (kb_pad, blk), 0)
    valid = sub < qi
    key_i = lax.broadcasted_iota(jnp.int32, (blk, blk), 0)
    qry_i = lax.broadcasted_iota(jnp.int32, (blk, blk), 1)
    own_scores = []
    for hd in heads:
        qr = _rope(q_ref[:, hcols(hd)].astype(F32), c_ref[rows, :], sa_ref[rows, :], sb_ref[rows, :])
        qrt = qr.T
        qrt_bf = (qrt * exp_scale).astype(BF16)
        qaug_ref[hd, 0:HEAD_DIM, :] = qrt_bf
        own_scores.append(jnp.dot(kaug_ref[hd, rows, 0:HEAD_DIM], qrt_bf, preferred_element_type=F32))

        gate = _dot3(kmean_ref[hd], qrt, NN_DIMS)
        g = jnp.where(valid, gate, -jnp.inf)
        bias = jnp.full((kb_pad, blk), NEG, F32)
        for _ in range(MOBA_TOPK):
            top = jnp.max(g, axis=0, keepdims=True)
            idx = jnp.min(jnp.where(g == top, sub, kb_pad), axis=0, keepdims=True)
            pick = sub == idx
            bias = jnp.where(pick, jnp.where(valid, 0.0, bias), bias)
            g = jnp.where(pick, -jnp.inf, g)
        qaug_ref[hd, HEAD_DIM:HEAD_DIM + kb_pad, :] = bias.astype(BF16)

    def scores(trip, s_ref):
        keys = pl.ds(pl.multiple_of(trip * 2 * blk, 2 * blk), 2 * blk)
        for hd in heads:
            s_ref[hd] = jnp.dot(kaug_ref[hd, keys, :], qaug_ref[hd], preferred_element_type=F32)

    def consume(trip, s_ref):
        j0 = 2 * trip
        pbs, alphas = [], []
        for hd in heads:
            sj = s_ref[hd]
            m_prev = m_ref[hd]
            m_new = jnp.maximum(m_prev, jnp.max(sj, axis=0, keepdims=True))
            alpha = jnp.exp2(m_prev - m_new)
            pj = jnp.exp2(sj - m_new)
            l_ref[hd] = alpha * l_ref[hd] + jnp.sum(pj, axis=0, keepdims=True)
            m_ref[hd] = m_new
            pbs.append(pj.astype(BF16))
            alphas.append(alpha)
        for hd in heads:
            pv = (jnp.dot(vt_ref[hd, j0], pbs[hd][0:blk], preferred_element_type=F32)
                  + jnp.dot(vt_ref[hd, j0 + 1], pbs[hd][blk:2 * blk], preferred_element_type=F32))
            acc_ref[hd] = alphas[hd] * acc_ref[hd] + pv

    def trip_ahead(trip, cur_ref, nxt_ref):
        scores(trip + 1, nxt_ref)
        consume(trip, cur_ref)

    scores(0, s0_ref)
    own_ps = []
    for hd in heads:
        s = jnp.where(key_i <= qry_i, own_scores[hd], NEG)
        m0 = jnp.max(s, axis=0, keepdims=True)
        p = jnp.exp2(s - m0)
        m_ref[hd] = m0
        l_ref[hd] = jnp.sum(p, axis=0, keepdims=True)
        own_ps.append(p.astype(BF16))
    for hd in heads:
        acc_ref[hd] = jnp.dot(vt_ref[hd, qi], own_ps[hd], preferred_element_type=F32)

    n_trips = (qi + 1) // 2

    @pl.when(n_trips > 0)
    def _():
        def two_trips(r, carry):
            trip_ahead(2 * r, s0_ref, s1_ref)
            trip_ahead(2 * r + 1, s1_ref, s0_ref)
            return carry

        lax.fori_loop(0, (n_trips - 1) // 2, two_trips, 0)
        last = n_trips - 1

        @pl.when(n_trips % 2 == 1)
        def _():
            consume(last, s0_ref)

        @pl.when(n_trips % 2 == 0)
        def _():
            trip_ahead(last - 1, s0_ref, s1_ref)
            consume(last, s1_ref)

    for hd in heads:
        o_ref[:, hcols(hd)] = (acc_ref[hd] / l_ref[hd]).T.astype(o_ref.dtype)


def _attention(proj, rope_c, rope_sa, rope_sb, batch, seq, n_heads, q_col, k_col, v_col):
    t = proj.shape[0]
    blk = MOBA_BLOCK
    n_kb = seq // blk
    kb_pad = -(-n_kb // V7X_BF16_SUBLANES) * V7X_BF16_SUBLANES
    n_hd = ATTN_HEADS_PER_STEP
    wide = n_hd * HEAD_DIM
    assert n_kb % 2 == 0 and kb_pad <= HEAD_DIM and n_heads % n_hd == 0
    assert q_col % n_hd == 0 and k_col % n_hd == 0 and v_col % n_hd == 0
    groups = n_heads // n_hd
    kernel = functools.partial(_attn_kernel, n_kb=n_kb, kb_pad=kb_pad, n_hd=n_hd,
                               exp_scale=HEAD_DIM ** -0.5 * math.log2(math.e))
    tab = pl.BlockSpec((seq, HEAD_DIM), lambda bg, qi: (0, 0))
    return pl.pallas_call(
        kernel,
        grid=(batch * groups, n_kb),
        in_specs=[
            pl.BlockSpec((blk, wide), lambda bg, qi: ((bg // groups) * n_kb + qi, q_col // n_hd + bg % groups)),
            pl.BlockSpec((seq, wide), lambda bg, qi: (bg // groups, k_col // n_hd + bg % groups)),
            pl.BlockSpec((seq, wide), lambda bg, qi: (bg // groups, v_col // n_hd + bg % groups)),
            tab, tab, tab,
        ],
        out_specs=pl.BlockSpec((blk, wide), lambda bg, qi: ((bg // groups) * n_kb + qi, bg % groups)),
        out_shape=jax.ShapeDtypeStruct((t, n_heads * HEAD_DIM), BF16),
        scratch_shapes=[
            pltpu.VMEM((n_hd, seq, 2 * HEAD_DIM), BF16),
            pltpu.VMEM((n_hd, n_kb, HEAD_DIM, blk), BF16),
            pltpu.VMEM((n_hd, kb_pad, HEAD_DIM), F32),
            pltpu.VMEM((n_hd, 1, blk), F32),
            pltpu.VMEM((n_hd, 1, blk), F32),
            pltpu.VMEM((n_hd, HEAD_DIM, blk), F32),
            pltpu.VMEM((n_hd, 2 * HEAD_DIM, blk), BF16),
            pltpu.VMEM((n_hd, 2 * blk, blk), F32),
            pltpu.VMEM((n_hd, 2 * blk, blk), F32),
        ],
        compiler_params=_params("arbitrary", "arbitrary"),
        name="moba_attention",
    )(proj, proj, proj, rope_c, rope_sa, rope_sb)


def _merge_kernel(b_ref, c_ref, h_ref, cp_ref, hp_ref, gc_ref, ga_ref, at_ref, cw_ref, wc_ref, wa_ref,
                  o_ref, *, tiles_per_seq):
    hal = V7X_BF16_SUBLANES
    u = c_ref[...].astype(F32) * h_ref[...].astype(F32)
    up = cp_ref[...].astype(F32) * hp_ref[...].astype(F32)
    up = jnp.where(pl.program_id(0) % tiles_per_seq == 0, 0.0, up)
    row = lax.broadcasted_iota(jnp.int32, u.shape, 0)
    u1 = jnp.where(row == 0, up[hal - 1:hal, :], pltpu.roll(u, 1, 0))
    u2 = jnp.where(row == 0, up[hal - 2:hal - 1, :],
                   jnp.where(row == 1, up[hal - 1:hal, :], pltpu.roll(u, 2, 0)))
    y = cw_ref[0:1, :] * u2 + cw_ref[1:2, :] * u1 + cw_ref[2:3, :] * u
    cm = (b_ref[...].astype(F32) * y).astype(BF16)
    yc = jnp.dot(cm, wc_ref[...], preferred_element_type=F32)
    ya = jnp.dot(at_ref[...], wa_ref[...], preferred_element_type=F32)
    merged = jax.nn.sigmoid(gc_ref[...].astype(F32)) * yc + jax.nn.sigmoid(ga_ref[...].astype(F32)) * ya
    o_ref[...] = merged.astype(o_ref.dtype)


def _merge(proj, attn, conv_w, wc_bf, wa_bf, seq, cw, aw, d, tm):
    t = proj.shape[0]
    hal = V7X_BF16_SUBLANES
    kernel = functools.partial(_merge_kernel, tiles_per_seq=seq // tm)
    col = lambda cidx: pl.BlockSpec((tm, cw), lambda i: (i, cidx))
    halo = lambda cidx: pl.BlockSpec((hal, cw), lambda i: (jnp.maximum(i * (tm // hal) - 1, 0), cidx))
    const = lambda shape: pl.BlockSpec(shape, lambda i: (0, 0))
    return pl.pallas_call(
        kernel,
        grid=(t // tm,),
        in_specs=[col(0), col(1), col(2), halo(1), halo(2), col(6), col(7),
                  pl.BlockSpec((tm, aw), lambda i: (i, 0)),
                  const((CONV_K, cw)), const((cw, d)), const((aw, d))],
        out_specs=pl.BlockSpec((tm, d), lambda i: (i, 0)),
        out_shape=jax.ShapeDtypeStruct((t, d), BF16),
        compiler_params=_params("arbitrary"),
        name="merge_branches",
    )(proj, proj, proj, proj, proj, proj, proj, attn, conv_w, wc_bf, wa_bf)


def _oproj_kernel(mg_ref, x_ref, wo_ref, g_ref, wrt_ref, br_ref,
                  h1_ref, xn_ref, te_ref, tw_ref, rk_ref, cnt_ref, run_ref):
    h1 = x_ref[...] + jnp.dot(mg_ref[...], wo_ref[...], preferred_element_type=F32)
    h1_ref[...] = h1
    ms = jnp.mean(h1 * h1, axis=-1, keepdims=True)
    xn = h1 * lax.rsqrt(ms + NORM_EPS) * g_ref[...]
    xn_ref[...] = xn
    lg = _dot3(wrt_ref[...], xn, NT_DIMS) + br_ref[...]
    n_e, tm = lg.shape
    sub = lax.broadcasted_iota(jnp.int32, lg.shape, 0)
    vals, idxs = [], []
    for _ in range(TOP_K):
        top = jnp.max(lg, axis=0, keepdims=True)
        idx = jnp.min(jnp.where(lg == top, sub, n_e), axis=0, keepdims=True)
        vals.append(top)
        idxs.append(idx)
        lg = jnp.where(sub == idx, -jnp.inf, lg)
    v = jnp.concatenate(vals, axis=0)
    ex = jnp.exp(v - vals[0])
    te_ref[...] = jnp.concatenate(idxs, axis=0)
    tw_ref[...] = ex / jnp.sum(ex, axis=0, keepdims=True)

    @pl.when(pl.program_id(0) == 0)
    def _():
        run_ref[...] = jnp.zeros_like(run_ref)

    upper = jnp.where(lax.broadcasted_iota(jnp.int32, (tm, tm), 0) <= lax.broadcasted_iota(jnp.int32, (tm, tm), 1),
                      1.0, 0.0).astype(BF16)
    run = run_ref[...]
    ranks = []
    for k in range(TOP_K):
        onehot = jnp.where(sub == idxs[k], 1.0, 0.0)
        incl = jnp.dot(onehot.astype(BF16), upper, preferred_element_type=F32)
        ranks.append(jnp.sum(onehot * (incl - 1.0 + run[:, 0:1]), axis=0, keepdims=True))
        run = run + incl[:, tm - 1:tm]
    run_ref[...] = run
    rk_ref[...] = jnp.concatenate(ranks, axis=0).astype(jnp.int32)
    cnt_ref[...] = run.astype(jnp.int32)


def _oproj(merged, x2, wo_bf, g, wr_t, br, tm):
    t, d = x2.shape
    n_e = wr_t.shape[0]
    const = lambda shape: pl.BlockSpec(shape, lambda i: (0, 0))
    row = pl.BlockSpec((tm, d), lambda i: (i, 0))
    top = pl.BlockSpec((TOP_K, tm), lambda i: (0, i))
    return pl.pallas_call(
        _oproj_kernel,
        grid=(t // tm,),
        in_specs=[row, row, const((d, d)), const((1, d)), const((n_e, d)), const((n_e, 1))],
        out_specs=[row, row, top, top, top, const((n_e, V7X_LANES))],
        out_shape=[jax.ShapeDtypeStruct((t, d), F32), jax.ShapeDtypeStruct((t, d), F32),
                   jax.ShapeDtypeStruct((TOP_K, t), jnp.int32), jax.ShapeDtypeStruct((TOP_K, t), F32),
                   jax.ShapeDtypeStruct((TOP_K, t), jnp.int32), jax.ShapeDtypeStruct((n_e, V7X_LANES), jnp.int32)],
        scratch_shapes=[pltpu.VMEM((n_e, V7X_LANES), F32)],
        compiler_params=_params("arbitrary"),
        name="oproj_router",
    )(merged, x2, wo_bf, g.reshape(1, d), wr_t, br.reshape(n_e, 1))


def _dispatch_kernel(dest_ref, fill_ref, x_hbm, xs_ref, zero_ref, x_buf, sem, lsem, zsem,
                     *, n_e, n_blocks, tm_moe, tm):
    @pl.when(pl.program_id(0) == 0)
    def _():
        zero_ref[...] = jnp.zeros_like(zero_ref)

        def zcopy(b):
            return pltpu.make_async_copy(
                zero_ref, xs_ref.at[pl.ds(pl.multiple_of(b * tm_moe, tm_moe), tm_moe), :], zsem)

        def zstart(b, carry):
            zcopy(b).start()
            return carry

        def zwait(b, carry):
            zcopy(b).wait()
            return carry

        for e in range(n_e):
            @pl.when(fill_ref[e] >= 0)
            def _():
                zcopy(fill_ref[e]).start()
        lax.fori_loop(fill_ref[n_e], n_blocks, zstart, 0)
        for e in range(n_e):
            @pl.when(fill_ref[e] >= 0)
            def _():
                zcopy(fill_ref[e]).wait()
        lax.fori_loop(fill_ref[n_e], n_blocks, zwait, 0)

    i = pl.program_id(0)
    groups = tm // ROW_GROUP

    def load(tile, slot):
        rows = pl.ds(pl.multiple_of(tile * groups, groups), groups)
        return pltpu.make_async_copy(x_hbm.at[rows], x_buf.at[slot], lsem.at[slot])

    def scatters(slot):
        def copy(k, g, j):
            row = dest_ref[k * tm + g * ROW_GROUP + j]
            return pltpu.make_async_copy(x_buf.at[slot, g, pl.ds(j, 1), :], xs_ref.at[pl.ds(row, 1), :],
                                         sem.at[slot])
        return copy

    def step(slot):
        nxt, prv = (slot + 1) % DISPATCH_BUFFERS, (slot - 1) % DISPATCH_BUFFERS

        @pl.when(i == 0)
        def _():
            load(i, slot).start()

        @pl.when(i + 1 < pl.num_programs(0))
        def _():
            load(i + 1, nxt).start()

        load(i, slot).wait()
        _start_row_copies(tm, scatters(slot))

        @pl.when(i > 0)
        def _():
            _wait_row_copies(tm, scatters(prv))

        @pl.when(i + 1 == pl.num_programs(0))
        def _():
            _wait_row_copies(tm, scatters(slot))

    for slot in range(DISPATCH_BUFFERS):
        pl.when(i % DISPATCH_BUFFERS == slot)(functools.partial(step, slot))


DISPATCH_BUFFERS = 3

ROW_GROUP = 8


def _start_row_copies(tm, copy):
    def start(g, carry):
        for j in range(ROW_GROUP):
            for k in range(TOP_K):
                copy(k, g, j).start(priority=(j * TOP_K + k) % 2)
        return carry

    lax.fori_loop(0, tm // ROW_GROUP, start, 0)


def _wait_row_copies(tm, copy):
    def wait(g, carry):
        for j in range(ROW_GROUP):
            for k in range(TOP_K):
                copy(k, g, j).wait()
        return carry

    lax.fori_loop(0, tm // ROW_GROUP, wait, 0)


def _for_each_row_copy(tm, copy):
    _start_row_copies(tm, copy)
    _wait_row_copies(tm, copy)


def _tile_major(dest, tm):
    k, t = dest.shape
    return dest.reshape(k, t // tm, tm).transpose(1, 0, 2).reshape(-1)


def _dispatch(dest_tiles, fill, xn, n_blocks, tm_moe, tm):
    t, d = xn.shape
    n_e = fill.shape[0] - 1
    kernel = functools.partial(_dispatch_kernel, n_e=n_e, n_blocks=n_blocks, tm_moe=tm_moe, tm=tm)
    return pl.pallas_call(
        kernel,
        grid=(t // tm,),
        in_specs=[pl.BlockSpec((TOP_K * tm,), lambda i: (i,), memory_space=pltpu.SMEM),
                  pl.BlockSpec(memory_space=pltpu.SMEM),
                  pl.BlockSpec(memory_space=pl.ANY)],
        out_specs=pl.BlockSpec(memory_space=pl.ANY),
        out_shape=jax.ShapeDtypeStruct((n_blocks * tm_moe, d), F32),
        scratch_shapes=[pltpu.VMEM((tm_moe, d), F32),
                        pltpu.VMEM((DISPATCH_BUFFERS, tm // ROW_GROUP, ROW_GROUP, d), F32),
                        pltpu.SemaphoreType.DMA((DISPATCH_BUFFERS,)), pltpu.SemaphoreType.DMA((DISPATCH_BUFFERS,)),
                        pltpu.SemaphoreType.DMA(())],
        compiler_params=_params("arbitrary"),
        name="moe_dispatch",
    )(dest_tiles, fill, xn.reshape(t // ROW_GROUP, ROW_GROUP, d))


WEIGHT_K_CHUNKS = 4


def _dot_rounding_tile(x, stage_ref, slot, which, bf_ref):
    kc = bf_ref.shape[0] // WEIGHT_K_CHUNKS
    acc = None
    for c in range(WEIGHT_K_CHUNKS):
        rows = slice(c * kc, (c + 1) * kc)
        w = stage_ref[slot, which, rows, :].astype(BF16)
        bf_ref[rows, :] = w
        part = jnp.dot(x[:, rows], w, preferred_element_type=F32)
        acc = part if acc is None else acc + part
    return acc


N_ITEM_ARRAYS = 12
MOE_HALVES = 2


def _expert_items(blk_start, blk_count, half_count, n_col, n_blocks):
    n_e = blk_count.shape[0]
    n_items = n_blocks * n_col
    item_end = jnp.cumsum(blk_count * n_col)
    item_start = item_end - blk_count * n_col
    n_used = item_end[-1]
    blocks_used = jnp.sum(blk_count)
    i = jnp.arange(n_items, dtype=jnp.int32)
    ok = i < n_used
    nonempty_before = jnp.cumsum(blk_count > 0) - (blk_count > 0)
    experts = jnp.arange(n_e, dtype=jnp.int32)

    def at(idx):
        ic = jnp.clip(idx, 0, jnp.maximum(n_used - 1, 0))
        e = jnp.sum(ic[:, None] >= item_end[None, :], axis=1).astype(jnp.int32)
        mine = e[:, None] == experts[None, :]
        pick = lambda table: jnp.sum(jnp.where(mine, table[None, :], 0), axis=1)
        local = ic - pick(item_start)
        cnt = jnp.maximum(pick(blk_count), 1)
        first_blk = pick(blk_start)
        second_half = (local % cnt) * MOE_HALVES + 1 < pick(half_count)
        return e, local // cnt, first_blk + local % cnt, cnt, first_blk, pick(nonempty_before), second_half

    e, wcol, xblk, cnt, first_blk, groups_before, second_half = at(i)
    spare = i - n_used
    oblk = jnp.where(ok, xblk, blocks_used + spare // n_col)
    ocol = jnp.where(ok, wcol, spare % n_col)
    new = ok & (xblk == first_blk)
    slot = (groups_before * n_col + wcol) % 2
    nxt = i + cnt
    nxt_e, nxt_col = at(nxt)[:2]
    has_nxt = new & (nxt < n_used)
    as_i32 = lambda a: a.astype(jnp.int32)
    items = tuple(map(as_i32, (xblk, oblk, wcol, ocol, e, new, ok, slot, nxt_e, nxt_col, has_nxt,
                               ok & second_half)))
    assert len(items) == N_ITEM_ARRAYS and n_e == item_end.shape[0]
    return items


def _weight_tile_copy(w_hbm, stage_ref, sem, e, col, slot, which, tn):
    cols = pl.ds(pl.multiple_of(col * tn, tn), tn)
    return pltpu.make_async_copy(w_hbm.at[e, :, cols], stage_ref.at[slot, which], sem.at[slot, which])


def _stream_weight_tiles(i, items, w_hbms, stage_ref, sem, tn):
    _, _, wcol, _, exp, new, _, slot, nxt_e, nxt_col, has_nxt, _ = items

    @pl.when(new[i] == 1)
    def _():
        s = slot[i]

        @pl.when(i == 0)
        def _():
            for which, w in enumerate(w_hbms):
                _weight_tile_copy(w, stage_ref, sem, exp[i], wcol[i], s, which, tn).start(priority=1)

        for which, w in enumerate(w_hbms):
            _weight_tile_copy(w, stage_ref, sem, exp[i], wcol[i], s, which, tn).wait()

        @pl.when(has_nxt[i] == 1)
        def _():
            for which, w in enumerate(w_hbms):
                _weight_tile_copy(w, stage_ref, sem, nxt_e[i], nxt_col[i], 1 - s, which, tn).start(priority=1)


def _up_kernel(*refs, n_col):
    items = refs[:N_ITEM_ARRAYS]
    x_ref, wg_hbm, wu_hbm, bg_ref, bu_ref, o_ref, stage_ref, wgb_ref, wub_ref, sem = refs[N_ITEM_ARRAYS:]
    wcol, exp, new, ok, slot, second = items[2], items[4], items[5], items[6], items[7], items[11]
    i = pl.program_id(0)
    _stream_weight_tiles(i, items, (wg_hbm, wu_hbm), stage_ref, sem, wgb_ref.shape[1])
    tile = exp[i] * n_col + wcol[i]
    half = x_ref.shape[0] // MOE_HALVES
    lo, hi = slice(0, half), slice(half, MOE_HALVES * half)

    def finish(rows, g, u):
        g = jnp.minimum(g + bg_ref[pl.ds(tile, 1), :], SWIGLU_LIMIT)
        u = jnp.clip(u + bu_ref[pl.ds(tile, 1), :], -SWIGLU_LIMIT, SWIGLU_LIMIT)
        o_ref[rows, :] = ((u + 1.0) * (g * jax.nn.sigmoid(SWIGLU_ALPHA * g))).astype(o_ref.dtype)

    def plain(rows):
        x = x_ref[rows, :].astype(BF16)
        finish(rows, jnp.dot(x, wgb_ref[...], preferred_element_type=F32),
               jnp.dot(x, wub_ref[...], preferred_element_type=F32))

    @pl.when(new[i] == 1)
    def _():
        x = x_ref[lo, :].astype(BF16)
        finish(lo, _dot_rounding_tile(x, stage_ref, slot[i], 0, wgb_ref),
               _dot_rounding_tile(x, stage_ref, slot[i], 1, wub_ref))

    pl.when(ok[i] - new[i] == 1)(functools.partial(plain, lo))
    pl.when(second[i] == 1)(functools.partial(plain, hi))

    @pl.when(ok[i] - second[i] == 1)
    def _():
        o_ref[hi, :] = jnp.zeros((half, o_ref.shape[1]), o_ref.dtype)

    @pl.when(ok[i] == 0)
    def _():
        o_ref[...] = jnp.zeros_like(o_ref)


def _down_kernel(*refs, n_col):
    items = refs[:N_ITEM_ARRAYS]
    h_ref, wd_hbm, bd_ref, o_ref, stage_ref, wdb_ref, sem = refs[N_ITEM_ARRAYS:]
    wcol, exp, new, ok, slot, second = items[2], items[4], items[5], items[6], items[7], items[11]
    i = pl.program_id(0)
    _stream_weight_tiles(i, items, (wd_hbm,), stage_ref, sem, wdb_ref.shape[1])
    tile = exp[i] * n_col + wcol[i]
    half = h_ref.shape[0] // MOE_HALVES
    lo, hi = slice(0, half), slice(half, MOE_HALVES * half)

    def plain(rows):
        o_ref[rows, :] = (jnp.dot(h_ref[rows, :], wdb_ref[...], preferred_element_type=F32)
                          + bd_ref[pl.ds(tile, 1), :])

    @pl.when(new[i] == 1)
    def _():
        o_ref[lo, :] = (_dot_rounding_tile(h_ref[lo, :], stage_ref, slot[i], 0, wdb_ref)
                        + bd_ref[pl.ds(tile, 1), :])

    pl.when(ok[i] - new[i] == 1)(functools.partial(plain, lo))
    pl.when(second[i] == 1)(functools.partial(plain, hi))

    @pl.when(ok[i] - second[i] == 1)
    def _():
        o_ref[hi, :] = jnp.zeros((half, o_ref.shape[1]), o_ref.dtype)

    @pl.when(ok[i] == 0)
    def _():
        o_ref[...] = jnp.zeros_like(o_ref)


def _item_map(fn):
    return lambda i, *items: fn(i, *items[:5])


def _expert_up(items, xs, w_gate, b_gate, w_up, b_up, tm, tf):
    rows, d = xs.shape
    n_e, _, f = w_gate.shape
    n_items = items[0].shape[0]
    n_col = f // tf
    bspec = pl.BlockSpec((n_e * n_col, tf), lambda i, *_: (0, 0))
    hbm = pl.BlockSpec(memory_space=pl.ANY)
    grid_spec = pltpu.PrefetchScalarGridSpec(
        num_scalar_prefetch=N_ITEM_ARRAYS,
        grid=(n_items,),
        in_specs=[pl.BlockSpec((tm, d), _item_map(lambda i, xb, ob, wc, oc, e: (xb[i], 0))),
                  hbm, hbm, bspec, bspec],
        out_specs=pl.BlockSpec((tm, tf), _item_map(lambda i, xb, ob, wc, oc, e: (ob[i], oc[i]))),
        scratch_shapes=[pltpu.VMEM((2, 2, d, tf), F32), pltpu.VMEM((d, tf), BF16), pltpu.VMEM((d, tf), BF16),
                        pltpu.SemaphoreType.DMA((2, 2))],
    )
    return pl.pallas_call(
        functools.partial(_up_kernel, n_col=n_col),
        grid_spec=grid_spec,
        out_shape=jax.ShapeDtypeStruct((rows, f), BF16),
        compiler_params=_params("arbitrary"),
        name="moe_up",
    )(*items, xs, w_gate, w_up, b_gate.reshape(n_e * n_col, tf), b_up.reshape(n_e * n_col, tf))


def _expert_down(items, hid, w_down, b_down, tm, tn):
    rows, f = hid.shape
    n_e, _, d = w_down.shape
    n_items = items[0].shape[0]
    n_col = d // tn
    grid_spec = pltpu.PrefetchScalarGridSpec(
        num_scalar_prefetch=N_ITEM_ARRAYS,
        grid=(n_items,),
        in_specs=[pl.BlockSpec((tm, f), _item_map(lambda i, xb, ob, wc, oc, e: (xb[i], 0))),
                  pl.BlockSpec(memory_space=pl.ANY),
                  pl.BlockSpec((n_e * n_col, tn), lambda i, *_: (0, 0))],
        out_specs=pl.BlockSpec((tm, tn), _item_map(lambda i, xb, ob, wc, oc, e: (ob[i], oc[i]))),
        scratch_shapes=[pltpu.VMEM((2, 1, f, tn), F32), pltpu.VMEM((f, tn), BF16),
                        pltpu.SemaphoreType.DMA((2, 1))],
    )
    return pl.pallas_call(
        functools.partial(_down_kernel, n_col=n_col),
        grid_spec=grid_spec,
        out_shape=jax.ShapeDtypeStruct((rows, d), F32),
        compiler_params=_params("arbitrary"),
        name="moe_down",
    )(*items, hid, w_down, b_down.reshape(n_e * n_col, tn))


def _combine_kernel(dest_ref, next_ref, h1_ref, w_ref, g_ref, ys_ref, o_ref, buf_ref, sem):
    tm, d = h1_ref.shape
    i = pl.program_id(0)

    def copies(idx_ref, slot):
        def copy(k, g, j):
            row = idx_ref[k * tm + g * ROW_GROUP + j]
            return pltpu.make_async_copy(ys_ref.at[pl.ds(row, 1), :], buf_ref.at[slot, k, g, pl.ds(j, 1), :],
                                         sem.at[slot])
        return copy

    def step(slot):
        @pl.when(i == 0)
        def _():
            _start_row_copies(tm, copies(dest_ref, slot))

        @pl.when(i + 1 < pl.num_programs(0))
        def _():
            _start_row_copies(tm, copies(next_ref, 1 - slot))

        _wait_row_copies(tm, copies(dest_ref, slot))
        h2 = h1_ref[...]
        for k in range(TOP_K):
            h2 = h2 + w_ref[:, k:k + 1] * buf_ref[slot, k].reshape(tm, d)
        ms = jnp.mean(h2 * h2, axis=-1, keepdims=True)
        o_ref[...] = h2 * lax.rsqrt(ms + NORM_EPS) * g_ref[...]

    for slot in range(2):
        pl.when(i % 2 == slot)(functools.partial(step, slot))


def _combine(dest_tiles, h1, w_tk, g, ys, tm):
    t, d = h1.shape
    last = t // tm - 1
    return pl.pallas_call(
        _combine_kernel,
        grid=(t // tm,),
        in_specs=[pl.BlockSpec((TOP_K * tm,), lambda i: (i,), memory_space=pltpu.SMEM),
                  pl.BlockSpec((TOP_K * tm,), lambda i: (jnp.minimum(i + 1, last),), memory_space=pltpu.SMEM),
                  pl.BlockSpec((tm, d), lambda i: (i, 0)),
                  pl.BlockSpec((tm, TOP_K), lambda i: (i, 0)),
                  pl.BlockSpec((1, d), lambda i: (0, 0)),
                  pl.BlockSpec(memory_space=pl.ANY)],
        out_specs=pl.BlockSpec((tm, d), lambda i: (i, 0)),
        out_shape=jax.ShapeDtypeStruct((t, d), F32),
        scratch_shapes=[pltpu.VMEM((2, TOP_K, tm // ROW_GROUP, ROW_GROUP, d), F32),
                        pltpu.SemaphoreType.DMA((2,))],
        compiler_params=_params("arbitrary"),
        name="moe_combine",
    )(dest_tiles, dest_tiles, h1, w_tk, g.reshape(1, d), ys)


def _rope_tables(seq):
    half = ROPE_DIM // 2
    inv_freq = jnp.exp(-math.log(ROPE_THETA) * jnp.arange(half, dtype=F32) / half)
    ang = jnp.arange(seq, dtype=jnp.int32).astype(F32)[:, None] * inv_freq[None, :]
    cos, sin = jnp.cos(ang), jnp.sin(ang)
    zeros = jnp.zeros((seq, HEAD_DIM - ROPE_DIM), F32)
    zh = jnp.zeros((seq, half), F32)
    c = jnp.concatenate([cos, cos, jnp.ones_like(zeros)], axis=1)
    sa = jnp.concatenate([-sin, zh, zeros], axis=1)
    sb = jnp.concatenate([zh, sin, zeros], axis=1)
    return c, sa, sb


def _dest_kernel(base_ref, te_ref, rk_ref, o_ref):
    te = te_ref[...]
    rows = rk_ref[...]
    for e in range(base_ref.shape[0]):
        rows = rows + jnp.where(te == e, base_ref[e], 0)
    o_ref[...] = rows


def _dest_rows(base, top_e, rank, tl):
    k, t = top_e.shape
    blk = pl.BlockSpec((k, tl), lambda i: (0, i))
    return pl.pallas_call(
        _dest_kernel,
        grid=(t // tl,),
        in_specs=[pl.BlockSpec(memory_space=pltpu.SMEM), blk, blk],
        out_specs=blk,
        out_shape=jax.ShapeDtypeStruct((k, t), jnp.int32),
        compiler_params=_params("arbitrary"),
        name="moe_dest_rows",
    )(base, top_e, rank)


def _routing_tables(counts, tm):
    half_count = (counts + tm - 1) // tm
    blk_count = (half_count + MOE_HALVES - 1) // MOE_HALVES
    blk_end = jnp.cumsum(blk_count)
    blk_start = blk_end - blk_count
    first_half = blk_start * MOE_HALVES
    padded = jnp.where(half_count > 0, first_half + half_count - 1, -1)
    unused = jnp.where(half_count < blk_count * MOE_HALVES, first_half + half_count, -1)
    fill = jnp.concatenate([padded, unused, blk_end[-1:] * MOE_HALVES])
    as_i32 = lambda a: a.astype(jnp.int32)
    return as_i32(blk_start), as_i32(blk_count), as_i32(half_count), as_i32(first_half * tm), as_i32(fill)


def kernel(x, norm_mix, w_in, conv_w, w_conv_out, w_attn_out, w_o, norm_ffn, w_router, b_router,
           w_gate, b_gate, w_up, b_up, w_down, b_down, norm_final):
    batch, seq, d = x.shape
    cw = w_conv_out.shape[1]
    aw = w_attn_out.shape[1]
    n_heads = aw // HEAD_DIM
    n_e = w_router.shape[-1]
    f = w_gate.shape[-1]
    t = batch * seq
    assert w_in.shape[0] == 1, "single layer: the final rmsnorm is fused into the MoE combine"
    assert cw == d and aw == d and seq % MOBA_BLOCK == 0

    tm_norm = min(512, t)
    tm_in, tn_in = min(1024, t), min(1024, w_in.shape[-1])
    tm_mid = min(256, seq)
    tm_moe = min(256, t)
    tf_up = min(1024, f)
    tn_down = min(2048, d)
    tm_disp = min(256, t)
    tm_comb = min(256, t)
    tm_item = MOE_HALVES * tm_moe
    n_blocks = (t * TOP_K + n_e * (tm_item - 1)) // tm_item
    col = lambda width_off: width_off // HEAD_DIM

    rope_c, rope_sa, rope_sb = _rope_tables(seq)
    h = x.reshape(t, d)
    xn = _rmsnorm(h, norm_mix[0], tm_norm, BF16)
    proj = _inproj(xn, w_in[0], tm_in, tn_in)
    attn = _attention(proj, rope_c, rope_sa, rope_sb, batch, seq, n_heads,
                      col(3 * cw), col(3 * cw + aw), col(3 * cw + 2 * aw))
    merged = _merge(proj, attn, conv_w[0], w_conv_out[0].astype(BF16), w_attn_out[0].astype(BF16),
                    seq, cw, aw, d, tm_mid)
    h1, xn2, top_e, top_w, rank, counts = _oproj(merged, h, w_o[0].astype(BF16), norm_ffn[0],
                                                 w_router[0].T, b_router[0], tm_mid)
    blk_start, blk_count, half_count, base, fill = _routing_tables(counts[:, 0], tm_moe)
    dest = _dest_rows(base, top_e, rank, min(2048, t))
    xs = _dispatch(_tile_major(dest, tm_disp), fill, xn2, n_blocks * MOE_HALVES, tm_moe, tm_disp)
    up_items = _expert_items(blk_start, blk_count, half_count, f // tf_up, n_blocks)
    hid = _expert_up(up_items, xs, w_gate[0], b_gate[0], w_up[0], b_up[0], tm_item, tf_up)
    same_tiling = f // tf_up == d // tn_down
    down_items = (up_items if same_tiling else
                  _expert_items(blk_start, blk_count, half_count, d // tn_down, n_blocks))
    ys = _expert_down(down_items, hid, w_down[0], b_down[0], tm_item, tn_down)
    out = _combine(_tile_major(dest, tm_comb), h1, top_w.T, norm_final, ys, tm_comb)
    return out.reshape(batch, seq, d)
```

```python
import functools
import math

import jax
import jax.numpy as jnp
from jax import lax
from jax.experimental import pallas as pl
from jax.experimental.pallas import tpu as pltpu

HEAD_DIM = 128
ROPE_DIM = HEAD_DIM // 4
ROPE_THETA = 500000.0
MOBA_BLOCK = 256
MOBA_TOPK = 3
CONV_K = 3
TOP_K = 4
SWIGLU_LIMIT = 7.0
SWIGLU_ALPHA = 1.702
NORM_EPS = 1e-5
NEG = -1e30
ATTN_HEADS_PER_STEP = 4

V7X_LANES = 128
V7X_BF16_SUBLANES = 16
V7X_VMEM_BYTES = 64 * 1024 * 1024
VMEM_LIMIT = V7X_VMEM_BYTES - 4 * 1024 * 1024

F32 = jnp.float32
BF16 = jnp.bfloat16
NT_DIMS = (((1,), (1,)), ((), ()))
NN_DIMS = (((1,), (0,)), ((), ()))


def _params(*sem):
    return pltpu.CompilerParams(dimension_semantics=sem, vmem_limit_bytes=VMEM_LIMIT)


def _split_bf16(a):
    hi = a.astype(BF16)
    return hi, (a - hi.astype(F32)).astype(BF16)


def _dot3(a, b, dims):
    ah, al = _split_bf16(a)
    bh, bl = _split_bf16(b)
    rows = a.shape[0]
    dot = lambda u, v: lax.dot_general(u, v, dims, preferred_element_type=F32)
    both = dot(jnp.concatenate([ah, al], axis=0), bh)
    return both[:rows] + (both[rows:] + dot(ah, bl))


def _rmsnorm_kernel(x_ref, g_ref, o_ref):
    x = x_ref[...]
    ms = jnp.mean(x * x, axis=-1, keepdims=True)
    o_ref[...] = (x * lax.rsqrt(ms + NORM_EPS) * g_ref[...]).astype(o_ref.dtype)


def _rmsnorm(x, g, tm, out_dtype):
    t, d = x.shape
    return pl.pallas_call(
        _rmsnorm_kernel,
        grid=(t // tm,),
        in_specs=[pl.BlockSpec((tm, d), lambda i: (i, 0)), pl.BlockSpec((1, d), lambda i: (0, 0))],
        out_specs=pl.BlockSpec((tm, d), lambda i: (i, 0)),
        out_shape=jax.ShapeDtypeStruct((t, d), out_dtype),
        compiler_params=_params("arbitrary"),
        name="rmsnorm",
    )(x, g.reshape(1, d))


def _inproj_kernel(x_ref, w_ref, o_ref, wbf_ref):
    @pl.when(pl.program_id(1) == 0)
    def _():
        wbf_ref[...] = w_ref[...].astype(BF16)

    o_ref[...] = jnp.dot(x_ref[...], wbf_ref[...], preferred_element_type=F32).astype(o_ref.dtype)


def _inproj(xn, w, tm, tn):
    t, d = xn.shape
    n = w.shape[1]
    return pl.pallas_call(
        _inproj_kernel,
        grid=(n // tn, t // tm),
        in_specs=[pl.BlockSpec((tm, d), lambda j, i: (i, 0)), pl.BlockSpec((d, tn), lambda j, i: (0, j))],
        out_specs=pl.BlockSpec((tm, tn), lambda j, i: (i, j)),
        out_shape=jax.ShapeDtypeStruct((t, n), BF16),
        scratch_shapes=[pltpu.VMEM((d, tn), BF16)],
        compiler_params=_params("arbitrary", "arbitrary"),
        name="inproj",
    )(xn, w)


def _rope(t, c, sa, sb):
    half = ROPE_DIM // 2
    return t * c + pltpu.roll(t, HEAD_DIM - half, 1) * sa + pltpu.roll(t, half, 1) * sb


def _attn_kernel(q_ref, k_ref, v_ref, c_ref, sa_ref, sb_ref, o_ref,
                 kaug_ref, vt_ref, kmean_ref, m_ref, l_ref, acc_ref, qaug_ref, s0_ref, s1_ref,
                 *, n_kb, kb_pad, n_hd, exp_scale):
    blk = MOBA_BLOCK
    qi = pl.program_id(1)
    heads = range(n_hd)
    hcols = lambda hd: slice(hd * HEAD_DIM, (hd + 1) * HEAD_DIM)

    @pl.when(qi == 0)
    def _prep():
        kmean_ref[...] = jnp.zeros_like(kmean_ref)
        qaug_ref[...] = jnp.zeros_like(qaug_ref)
        lane = lax.broadcasted_iota(jnp.int32, (blk, HEAD_DIM), 1)

        def body(j, carry):
            rows = pl.ds(pl.multiple_of(j * blk, blk), blk)
            onehot = jnp.where(lane == j, 1.0, 0.0).astype(BF16)
            for hd in heads:
                kr = _rope(k_ref[rows, hcols(hd)].astype(F32), c_ref[rows, :], sa_ref[rows, :], sb_ref[rows, :])
                kaug_ref[hd, rows, 0:HEAD_DIM] = kr.astype(BF16)
                kaug_ref[hd, rows, HEAD_DIM:2 * HEAD_DIM] = onehot
                kmean_ref[hd, pl.ds(j, 1), :] = jnp.mean(kr, axis=0, keepdims=True)
                vt_ref[hd, j] = v_ref[rows, hcols(hd)].astype(F32).T.astype(BF16)
            return carry

        lax.fori_loop(0, n_kb, body, 0)

    rows = pl.ds(pl.multiple_of(qi * blk, blk), blk)
    sub = lax.broadcasted_iota(jnp.int32, (kb_pad, blk), 0)
    valid = sub < qi
    key_i = lax.broadcasted_iota(jnp.int32, (blk, blk), 0)
    qry_i = lax.broadcasted_iota(jnp.int32, (blk, blk), 1)
    own_scores = []
    for hd in heads:
        qr = _rope(q_ref[:, hcols(hd)].astype(F32), c_ref[rows, :], sa_ref[rows, :], sb_ref[rows, :])
        qrt = qr.T
        qrt_bf = (qrt * exp_scale).astype(BF16)
        qaug_ref[hd, 0:HEAD_DIM, :] = qrt_bf
        own_scores.append(jnp.dot(kaug_ref[hd, rows, 0:HEAD_DIM], qrt_bf, preferred_element_type=F32))

        gate = _dot3(kmean_ref[hd], qrt, NN_DIMS)
        g = jnp.where(valid, gate, -jnp.inf)
        bias = jnp.full((kb_pad, blk), NEG, F32)
        for _ in range(MOBA_TOPK):
            top = jnp.max(g, axis=0, keepdims=True)
            idx = jnp.min(jnp.where(g == top, sub, kb_pad), axis=0, keepdims=True)
            pick = sub == idx
            bias = jnp.where(pick, jnp.where(valid, 0.0, bias), bias)
            g = jnp.where(pick, -jnp.inf, g)
        qaug_ref[hd, HEAD_DIM:HEAD_DIM + kb_pad, :] = bias.astype(BF16)

    def scores(trip, s_ref):
        keys = pl.ds(pl.multiple_of(trip * 2 * blk, 2 * blk), 2 * blk)
        for hd in heads:
            s_ref[hd] = jnp.dot(kaug_ref[hd, keys, :], qaug_ref[hd], preferred_element_type=F32)

    def consume(trip, s_ref):
        j0 = 2 * trip
        pbs, alphas = [], []
        for hd in heads:
            sj = s_ref[hd]
            m_prev = m_ref[hd]
            m_new = jnp.maximum(m_prev, jnp.max(sj, axis=0, keepdims=True))
            alpha = jnp.exp2(m_prev - m_new)
            pj = jnp.exp2(sj - m_new)
            l_ref[hd] = alpha * l_ref[hd] + jnp.sum(pj, axis=0, keepdims=True)
            m_ref[hd] = m_new
            pbs.append(pj.astype(BF16))
            alphas.append(alpha)
        for hd in heads:
            pv = (jnp.dot(vt_ref[hd, j0], pbs[hd][0:blk], preferred_element_type=F32)
                  + jnp.dot(vt_ref[hd, j0 + 1], pbs[hd][blk:2 * blk], preferred_element_type=F32))
            acc_ref[hd] = alphas[hd] * acc_ref[hd] + pv

    def trip_ahead(trip, cur_ref, nxt_ref):
        scores(trip + 1, nxt_ref)
        consume(trip, cur_ref)

    scores(0, s0_ref)
    own_ps = []
    for hd in heads:
        s = jnp.where(key_i <= qry_i, own_scores[hd], NEG)
        m0 = jnp.max(s, axis=0, keepdims=True)
        p = jnp.exp2(s - m0)
        m_ref[hd] = m0
        l_ref[hd] = jnp.sum(p, axis=0, keepdims=True)
        own_ps.append(p.astype(BF16))
    for hd in heads:
        acc_ref[hd] = jnp.dot(vt_ref[hd, qi], own_ps[hd], preferred_element_type=F32)

    n_trips = (qi + 1) // 2

    @pl.when(n_trips > 0)
    def _():
        def two_trips(r, carry):
            trip_ahead(2 * r, s0_ref, s1_ref)
            trip_ahead(2 * r + 1, s1_ref, s0_ref)
            return carry

        lax.fori_loop(0, (n_trips - 1) // 2, two_trips, 0)
        last = n_trips - 1

        @pl.when(n_trips % 2 == 1)
        def _():
            consume(last, s0_ref)

        @pl.when(n_trips % 2 == 0)
        def _():
            trip_ahead(last - 1, s0_ref, s1_ref)
            consume(last, s1_ref)

    for hd in heads:
        o_ref[:, hcols(hd)] = (acc_ref[hd] / l_ref[hd]).T.astype(o_ref.dtype)


def _attention(proj, rope_c, rope_sa, rope_sb, batch, seq, n_heads, q_col, k_col, v_col):
    t = proj.shape[0]
    blk = MOBA_BLOCK
    n_kb = seq // blk
    kb_pad = -(-n_kb // V7X_BF16_SUBLANES) * V7X_BF16_SUBLANES
    n_hd = ATTN_HEADS_PER_STEP
    wide = n_hd * HEAD_DIM
    assert n_kb % 2 == 0 and kb_pad <= HEAD_DIM and n_heads % n_hd == 0
    assert q_col % n_hd == 0 and k_col % n_hd == 0 and v_col % n_hd == 0
    groups = n_heads // n_hd
    kernel = functools.partial(_attn_kernel, n_kb=n_kb, kb_pad=kb_pad, n_hd=n_hd,
                               exp_scale=HEAD_DIM ** -0.5 * math.log2(math.e))
    tab = pl.BlockSpec((seq, HEAD_DIM), lambda bg, qi: (0, 0))
    return pl.pallas_call(
        kernel,
        grid=(batch * groups, n_kb),
        in_specs=[
            pl.BlockSpec((blk, wide), lambda bg, qi: ((bg // groups) * n_kb + qi, q_col // n_hd + bg % groups)),
            pl.BlockSpec((seq, wide), lambda bg, qi: (bg // groups, k_col // n_hd + bg % groups)),
            pl.BlockSpec((seq, wide), lambda bg, qi: (bg // groups, v_col // n_hd + bg % groups)),
            tab, tab, tab,
        ],
        out_specs=pl.BlockSpec((blk, wide), lambda bg, qi: ((bg // groups) * n_kb + qi, bg % groups)),
        out_shape=jax.ShapeDtypeStruct((t, n_heads * HEAD_DIM), BF16),
        scratch_shapes=[
            pltpu.VMEM((n_hd, seq, 2 * HEAD_DIM), BF16),
            pltpu.VMEM((n_hd, n_kb, HEAD_DIM, blk), BF16),
            pltpu.VMEM((n_hd, kb_pad, HEAD_DIM), F32),
            pltpu.VMEM((n_hd, 1, blk), F32),
            pltpu.VMEM((n_hd, 1, blk), F32),
            pltpu.VMEM((n_hd, HEAD_DIM, blk), F32),
            pltpu.VMEM((n_hd, 2 * HEAD_DIM, blk), BF16),
            pltpu.VMEM((n_hd, 2 * blk, blk), F32),
            pltpu.VMEM((n_hd, 2 * blk, blk), F32),
        ],
        compiler_params=_params("arbitrary", "arbitrary"),
        name="moba_attention",
    )(proj, proj, proj, rope_c, rope_sa, rope_sb)


def _merge_kernel(b_ref, c_ref, h_ref, cp_ref, hp_ref, gc_ref, ga_ref, at_ref, cw_ref, wc_ref, wa_ref,
                  o_ref, *, tiles_per_seq):
    hal = V7X_BF16_SUBLANES
    u = c_ref[...].astype(F32) * h_ref[...].astype(F32)
    up = cp_ref[...].astype(F32) * hp_ref[...].astype(F32)
    up = jnp.where(pl.program_id(0) % tiles_per_seq == 0, 0.0, up)
    row = lax.broadcasted_iota(jnp.int32, u.shape, 0)
    u1 = jnp.where(row == 0, up[hal - 1:hal, :], pltpu.roll(u, 1, 0))
    u2 = jnp.where(row == 0, up[hal - 2:hal - 1, :],
                   jnp.where(row == 1, up[hal - 1:hal, :], pltpu.roll(u, 2, 0)))
    y = cw_ref[0:1, :] * u2 + cw_ref[1:2, :] * u1 + cw_ref[2:3, :] * u
    cm = (b_ref[...].astype(F32) * y).astype(BF16)
    yc = jnp.dot(cm, wc_ref[...], preferred_element_type=F32)
    ya = jnp.dot(at_ref[...], wa_ref[...], preferred_element_type=F32)
    merged = jax.nn.sigmoid(gc_ref[...].astype(F32)) * yc + jax.nn.sigmoid(ga_ref[...].astype(F32)) * ya
    o_ref[...] = merged.astype(o_ref.dtype)


def _merge(proj, attn, conv_w, wc_bf, wa_bf, seq, cw, aw, d, tm):
    t = proj.shape[0]
    hal = V7X_BF16_SUBLANES
    kernel = functools.partial(_merge_kernel, tiles_per_seq=seq // tm)
    col = lambda cidx: pl.BlockSpec((tm, cw), lambda i: (i, cidx))
    halo = lambda cidx: pl.BlockSpec((hal, cw), lambda i: (jnp.maximum(i * (tm // hal) - 1, 0), cidx))
    const = lambda shape: pl.BlockSpec(shape, lambda i: (0, 0))
    return pl.pallas_call(
        kernel,
        grid=(t // tm,),
        in_specs=[col(0), col(1), col(2), halo(1), halo(2), col(6), col(7),
                  pl.BlockSpec((tm, aw), lambda i: (i, 0)),
                  const((CONV_K, cw)), const((cw, d)), const((aw, d))],
        out_specs=pl.BlockSpec((tm, d), lambda i: (i, 0)),
        out_shape=jax.ShapeDtypeStruct((t, d), BF16),
        compiler_params=_params("arbitrary"),
        name="merge_branches",
    )(proj, proj, proj, proj, proj, proj, proj, attn, conv_w, wc_bf, wa_bf)


def _oproj_kernel(mg_ref, x_ref, wo_ref, g_ref, wrt_ref, br_ref,
                  h1_ref, xn_ref, te_ref, tw_ref, rk_ref, cnt_ref, run_ref):
    h1 = x_ref[...] + jnp.dot(mg_ref[...], wo_ref[...], preferred_element_type=F32)
    h1_ref[...] = h1
    ms = jnp.mean(h1 * h1, axis=-1, keepdims=True)
    xn = h1 * lax.rsqrt(ms + NORM_EPS) * g_ref[...]
    xn_ref[...] = xn
    lg = _dot3(wrt_ref[...], xn, NT_DIMS) + br_ref[...]
    n_e, tm = lg.shape
    sub = lax.broadcasted_iota(jnp.int32, lg.shape, 0)
    vals, idxs = [], []
    for _ in range(TOP_K):
        top = jnp.max(lg, axis=0, keepdims=True)
        idx = jnp.min(jnp.where(lg == top, sub, n_e), axis=0, keepdims=True)
        vals.append(top)
        idxs.append(idx)
        lg = jnp.where(sub == idx, -jnp.inf, lg)
    v = jnp.concatenate(vals, axis=0)
    ex = jnp.exp(v - vals[0])
    te_ref[...] = jnp.concatenate(idxs, axis=0)
    tw_ref[...] = ex / jnp.sum(ex, axis=0, keepdims=True)

    @pl.when(pl.program_id(0) == 0)
    def _():
        run_ref[...] = jnp.zeros_like(run_ref)

    upper = jnp.where(lax.broadcasted_iota(jnp.int32, (tm, tm), 0) <= lax.broadcasted_iota(jnp.int32, (tm, tm), 1),
                      1.0, 0.0).astype(BF16)
    run = run_ref[...]
    ranks = []
    for k in range(TOP_K):
        onehot = jnp.where(sub == idxs[k], 1.0, 0.0)
        incl = jnp.dot(onehot.astype(BF16), upper, preferred_element_type=F32)
        ranks.append(jnp.sum(onehot * (incl - 1.0 + run[:, 0:1]), axis=0, keepdims=True))
        run = run + incl[:, tm - 1:tm]
    run_ref[...] = run
    rk_ref[...] = jnp.concatenate(ranks, axis=0).astype(jnp.int32)
    cnt_ref[...] = run.astype(jnp.int32)


def _oproj(merged, x2, wo_bf, g, wr_t, br, tm):
    t, d = x2.shape
    n_e = wr_t.shape[0]
    const = lambda shape: pl.BlockSpec(shape, lambda i: (0, 0))
    row = pl.BlockSpec((tm, d), lambda i: (i, 0))
    top = pl.BlockSpec((TOP_K, tm), lambda i: (0, i))
    return pl.pallas_call(
        _oproj_kernel,
        grid=(t // tm,),
        in_specs=[row, row, const((d, d)), const((1, d)), const((n_e, d)), const((n_e, 1))],
        out_specs=[row, row, top, top, top, const((n_e, V7X_LANES))],
        out_shape=[jax.ShapeDtypeStruct((t, d), F32), jax.ShapeDtypeStruct((t, d), F32),
                   jax.ShapeDtypeStruct((TOP_K, t), jnp.int32), jax.ShapeDtypeStruct((TOP_K, t), F32),
                   jax.ShapeDtypeStruct((TOP_K, t), jnp.int32), jax.ShapeDtypeStruct((n_e, V7X_LANES), jnp.int32)],
        scratch_shapes=[pltpu.VMEM((n_e, V7X_LANES), F32)],
        compiler_params=_params("arbitrary"),
        name="oproj_router",
    )(merged, x2, wo_bf, g.reshape(1, d), wr_t, br.reshape(n_e, 1))


def _dispatch_kernel(dest_ref, fill_ref, x_hbm, xs_ref, zero_ref, x_buf, sem, lsem, zsem,
                     *, n_e, n_blocks, tm_moe, tm):
    @pl.when(pl.program_id(0) == 0)
    def _():
        zero_ref[...] = jnp.zeros_like(zero_ref)

        def zcopy(b):
            return pltpu.make_async_copy(
                zero_ref, xs_ref.at[pl.ds(pl.multiple_of(b * tm_moe, tm_moe), tm_moe), :], zsem)

        def zstart(b, carry):
            zcopy(b).start()
            return carry

        def zwait(b, carry):
            zcopy(b).wait()
            return carry

        for e in range(n_e):
            @pl.when(fill_ref[e] >= 0)
            def _():
                zcopy(fill_ref[e]).start()
        lax.fori_loop(fill_ref[n_e], n_blocks, zstart, 0)
        for e in range(n_e):
            @pl.when(fill_ref[e] >= 0)
            def _():
                zcopy(fill_ref[e]).wait()
        lax.fori_loop(fill_ref[n_e], n_blocks, zwait, 0)

    i = pl.program_id(0)
    groups = tm // ROW_GROUP

    def load(tile, slot):
        rows = pl.ds(pl.multiple_of(tile * groups, groups), groups)
        return pltpu.make_async_copy(x_hbm.at[rows], x_buf.at[slot], lsem.at[slot])

    def scatters(slot):
        def copy(k, g, j):
            row = dest_ref[k * tm + g * ROW_GROUP + j]
            return pltpu.make_async_copy(x_buf.at[slot, g, pl.ds(j, 1), :], xs_ref.at[pl.ds(row, 1), :],
                                         sem.at[slot])
        return copy

    def step(slot):
        nxt, prv = (slot + 1) % DISPATCH_BUFFERS, (slot - 1) % DISPATCH_BUFFERS

        @pl.when(i == 0)
        def _():
            load(i, slot).start()

        @pl.when(i + 1 < pl.num_programs(0))
        def _():
            load(i + 1, nxt).start()

        load(i, slot).wait()
        _start_row_copies(tm, scatters(slot))

        @pl.when(i > 0)
        def _():
            _wait_row_copies(tm, scatters(prv))

        @pl.when(i + 1 == pl.num_programs(0))
        def _():
            _wait_row_copies(tm, scatters(slot))

    for slot in range(DISPATCH_BUFFERS):
        pl.when(i % DISPATCH_BUFFERS == slot)(functools.partial(step, slot))


DISPATCH_BUFFERS = 3

ROW_GROUP = 8


def _start_row_copies(tm, copy):
    def start(g, carry):
        for j in range(ROW_GROUP):
            for k in range(TOP_K):
                copy(k, g, j).start(priority=(j * TOP_K + k) % 2)
        return carry

    lax.fori_loop(0, tm // ROW_GROUP, start, 0)


def _wait_row_copies(tm, copy):
    def wait(g, carry):
        for j in range(ROW_GROUP):
            for k in range(TOP_K):
                copy(k, g, j).wait()
        return carry

    lax.fori_loop(0, tm // ROW_GROUP, wait, 0)


def _for_each_row_copy(tm, copy):
    _start_row_copies(tm, copy)
    _wait_row_copies(tm, copy)


def _tile_major(dest, tm):
    k, t = dest.shape
    return dest.reshape(k, t // tm, tm).transpose(1, 0, 2).reshape(-1)


def _dispatch(dest_tiles, fill, xn, n_blocks, tm_moe, tm):
    t, d = xn.shape
    n_e = fill.shape[0] - 1
    kernel = functools.partial(_dispatch_kernel, n_e=n_e, n_blocks=n_blocks, tm_moe=tm_moe, tm=tm)
    return pl.pallas_call(
        kernel,
        grid=(t // tm,),
        in_specs=[pl.BlockSpec((TOP_K * tm,), lambda i: (i,), memory_space=pltpu.SMEM),
                  pl.BlockSpec(memory_space=pltpu.SMEM),
                  pl.BlockSpec(memory_space=pl.ANY)],
        out_specs=pl.BlockSpec(memory_space=pl.ANY),
        out_shape=jax.ShapeDtypeStruct((n_blocks * tm_moe, d), F32),
        scratch_shapes=[pltpu.VMEM((tm_moe, d), F32),
                        pltpu.VMEM((DISPATCH_BUFFERS, tm // ROW_GROUP, ROW_GROUP, d), F32),
                        pltpu.SemaphoreType.DMA((DISPATCH_BUFFERS,)), pltpu.SemaphoreType.DMA((DISPATCH_BUFFERS,)),
                        pltpu.SemaphoreType.DMA(())],
        compiler_params=_params("arbitrary"),
        name="moe_dispatch",
    )(dest_tiles, fill, xn.reshape(t // ROW_GROUP, ROW_GROUP, d))


WEIGHT_K_CHUNKS = 4


def _dot_rounding_tile(x, stage_ref, slot, which, bf_ref):
    kc = bf_ref.shape[0] // WEIGHT_K_CHUNKS
    acc = None
    for c in range(WEIGHT_K_CHUNKS):
        rows = slice(c * kc, (c + 1) * kc)
        w = stage_ref[slot, which, rows, :].astype(BF16)
        bf_ref[rows, :] = w
        part = jnp.dot(x[:, rows], w, preferred_element_type=F32)
        acc = part if acc is None else acc + part
    return acc


N_ITEM_ARRAYS = 12
MOE_HALVES = 2


def _expert_items(blk_start, blk_count, half_count, n_col, n_blocks):
    n_e = blk_count.shape[0]
    n_items = n_blocks * n_col
    item_end = jnp.cumsum(blk_count * n_col)
    item_start = item_end - blk_count * n_col
    n_used = item_end[-1]
    blocks_used = jnp.sum(blk_count)
    i = jnp.arange(n_items, dtype=jnp.int32)
    ok = i < n_used
    nonempty_before = jnp.cumsum(blk_count > 0) - (blk_count > 0)
    experts = jnp.arange(n_e, dtype=jnp.int32)

    def at(idx):
        ic = jnp.clip(idx, 0, jnp.maximum(n_used - 1, 0))
        e = jnp.sum(ic[:, None] >= item_end[None, :], axis=1).astype(jnp.int32)
        mine = e[:, None] == experts[None, :]
        pick = lambda table: jnp.sum(jnp.where(mine, table[None, :], 0), axis=1)
        local = ic - pick(item_start)
        cnt = jnp.maximum(pick(blk_count), 1)
        first_blk = pick(blk_start)
        second_half = (local % cnt) * MOE_HALVES + 1 < pick(half_count)
        return e, local // cnt, first_blk + local % cnt, cnt, first_blk, pick(nonempty_before), second_half

    e, wcol, xblk, cnt, first_blk, groups_before, second_half = at(i)
    spare = i - n_used
    oblk = jnp.where(ok, xblk, blocks_used + spare // n_col)
    ocol = jnp.where(ok, wcol, spare % n_col)
    new = ok & (xblk == first_blk)
    slot = (groups_before * n_col + wcol) % 2
    nxt = i + cnt
    nxt_e, nxt_col = at(nxt)[:2]
    has_nxt = new & (nxt < n_used)
    as_i32 = lambda a: a.astype(jnp.int32)
    items = tuple(map(as_i32, (xblk, oblk, wcol, ocol, e, new, ok, slot, nxt_e, nxt_col, has_nxt,
                               ok & second_half)))
    assert len(items) == N_ITEM_ARRAYS and n_e == item_end.shape[0]
    return items


def _weight_tile_copy(w_hbm, stage_ref, sem, e, col, slot, which, tn):
    cols = pl.ds(pl.multiple_of(col * tn, tn), tn)
    return pltpu.make_async_copy(w_hbm.at[e, :, cols], stage_ref.at[slot, which], sem.at[slot, which])


def _stream_weight_tiles(i, items, w_hbms, stage_ref, sem, tn):
    _, _, wcol, _, exp, new, _, slot, nxt_e, nxt_col, has_nxt, _ = items

    @pl.when(new[i] == 1)
    def _():
        s = slot[i]

        @pl.when(i == 0)
        def _():
            for which, w in enumerate(w_hbms):
                _weight_tile_copy(w, stage_ref, sem, exp[i], wcol[i], s, which, tn).start(priority=1)

        for which, w in enumerate(w_hbms):
            _weight_tile_copy(w, stage_ref, sem, exp[i], wcol[i], s, which, tn).wait()

        @pl.when(has_nxt[i] == 1)
        def _():
            for which, w in enumerate(w_hbms):
                _weight_tile_copy(w, stage_ref, sem, nxt_e[i], nxt_col[i], 1 - s, which, tn).start(priority=1)


def _run_item_kind(new, ok, second, plain, rounding, o_ref, half):
    both, lo = slice(0, MOE_HALVES * half), slice(0, half)
    kind = ok * (2 * new + second + 1) - 1
    pl.when(kind == 3)(functools.partial(rounding, both))
    pl.when(kind == 2)(functools.partial(rounding, lo))
    pl.when(kind == 1)(functools.partial(plain, both))
    pl.when(kind == 0)(functools.partial(plain, lo))

    @pl.when((kind == 0) | (kind == 2))
    def _():
        o_ref[half:MOE_HALVES * half, :] = jnp.zeros(((MOE_HALVES - 1) * half, o_ref.shape[1]), o_ref.dtype)

    @pl.when(kind < 0)
    def _():
        o_ref[...] = jnp.zeros_like(o_ref)


def _up_kernel(*refs, n_col):
    items = refs[:N_ITEM_ARRAYS]
    x_ref, wg_hbm, wu_hbm, bg_ref, bu_ref, o_ref, stage_ref, wgb_ref, wub_ref, sem = refs[N_ITEM_ARRAYS:]
    wcol, exp, new, ok, slot, second = items[2], items[4], items[5], items[6], items[7], items[11]
    i = pl.program_id(0)
    _stream_weight_tiles(i, items, (wg_hbm, wu_hbm), stage_ref, sem, wgb_ref.shape[1])
    tile = exp[i] * n_col + wcol[i]
    half = x_ref.shape[0] // MOE_HALVES

    def finish(rows, g, u):
        g = jnp.minimum(g + bg_ref[pl.ds(tile, 1), :], SWIGLU_LIMIT)
        u = jnp.clip(u + bu_ref[pl.ds(tile, 1), :], -SWIGLU_LIMIT, SWIGLU_LIMIT)
        o_ref[rows, :] = ((u + 1.0) * (g * jax.nn.sigmoid(SWIGLU_ALPHA * g))).astype(o_ref.dtype)

    def plain(rows):
        x = x_ref[rows, :].astype(BF16)
        finish(rows, jnp.dot(x, wgb_ref[...], preferred_element_type=F32),
               jnp.dot(x, wub_ref[...], preferred_element_type=F32))

    def rounding(rows):
        x = x_ref[rows, :].astype(BF16)
        finish(rows, _dot_rounding_tile(x, stage_ref, slot[i], 0, wgb_ref),
               _dot_rounding_tile(x, stage_ref, slot[i], 1, wub_ref))

    _run_item_kind(new[i], ok[i], second[i], plain, rounding, o_ref, half)


def _down_kernel(*refs, n_col):
    items = refs[:N_ITEM_ARRAYS]
    h_ref, wd_hbm, bd_ref, o_ref, stage_ref, wdb_ref, sem = refs[N_ITEM_ARRAYS:]
    wcol, exp, new, ok, slot, second = items[2], items[4], items[5], items[6], items[7], items[11]
    i = pl.program_id(0)
    _stream_weight_tiles(i, items, (wd_hbm,), stage_ref, sem, wdb_ref.shape[1])
    tile = exp[i] * n_col + wcol[i]
    half = h_ref.shape[0] // MOE_HALVES

    def plain(rows):
        o_ref[rows, :] = (jnp.dot(h_ref[rows, :], wdb_ref[...], preferred_element_type=F32)
                          + bd_ref[pl.ds(tile, 1), :])

    def rounding(rows):
        o_ref[rows, :] = (_dot_rounding_tile(h_ref[rows, :], stage_ref, slot[i], 0, wdb_ref)
                          + bd_ref[pl.ds(tile, 1), :])

    _run_item_kind(new[i], ok[i], second[i], plain, rounding, o_ref, half)


def _item_map(fn):
    return lambda i, *items: fn(i, *items[:5])


def _expert_up(items, xs, w_gate, b_gate, w_up, b_up, tm, tf):
    rows, d = xs.shape
    n_e, _, f = w_gate.shape
    n_items = items[0].shape[0]
    n_col = f // tf
    bspec = pl.BlockSpec((n_e * n_col, tf), lambda i, *_: (0, 0))
    hbm = pl.BlockSpec(memory_space=pl.ANY)
    grid_spec = pltpu.PrefetchScalarGridSpec(
        num_scalar_prefetch=N_ITEM_ARRAYS,
        grid=(n_items,),
        in_specs=[pl.BlockSpec((tm, d), _item_map(lambda i, xb, ob, wc, oc, e: (xb[i], 0))),
                  hbm, hbm, bspec, bspec],
        out_specs=pl.BlockSpec((tm, tf), _item_map(lambda i, xb, ob, wc, oc, e: (ob[i], oc[i]))),
        scratch_shapes=[pltpu.VMEM((2, 2, d, tf), F32), pltpu.VMEM((d, tf), BF16), pltpu.VMEM((d, tf), BF16),
                        pltpu.SemaphoreType.DMA((2, 2))],
    )
    return pl.pallas_call(
        functools.partial(_up_kernel, n_col=n_col),
        grid_spec=grid_spec,
        out_shape=jax.ShapeDtypeStruct((rows, f), BF16),
        compiler_params=_params("arbitrary"),
        name="moe_up",
    )(*items, xs, w_gate, w_up, b_gate.reshape(n_e * n_col, tf), b_up.reshape(n_e * n_col, tf))


def _expert_down(items, hid, w_down, b_down, tm, tn):
    rows, f = hid.shape
    n_e, _, d = w_down.shape
    n_items = items[0].shape[0]
    n_col = d // tn
    grid_spec = pltpu.PrefetchScalarGridSpec(
        num_scalar_prefetch=N_ITEM_ARRAYS,
        grid=(n_items,),
        in_specs=[pl.BlockSpec((tm, f), _item_map(lambda i, xb, ob, wc, oc, e: (xb[i], 0))),
                  pl.BlockSpec(memory_space=pl.ANY),
                  pl.BlockSpec((n_e * n_col, tn), lambda i, *_: (0, 0))],
        out_specs=pl.BlockSpec((tm, tn), _item_map(lambda i, xb, ob, wc, oc, e: (ob[i], oc[i]))),
        scratch_shapes=[pltpu.VMEM((2, 1, f, tn), F32), pltpu.VMEM((f, tn), BF16),
                        pltpu.SemaphoreType.DMA((2, 1))],
    )
    return pl.pallas_call(
        functools.partial(_down_kernel, n_col=n_col),
        grid_spec=grid_spec,
        out_shape=jax.ShapeDtypeStruct((rows, d), F32),
        compiler_params=_params("arbitrary"),
        name="moe_down",
    )(*items, hid, w_down, b_down.reshape(n_e * n_col, tn))


def _combine_kernel(dest_ref, next_ref, h1_ref, w_ref, g_ref, ys_ref, o_ref, buf_ref, sem):
    tm, d = h1_ref.shape
    i = pl.program_id(0)

    def copies(idx_ref, slot):
        def copy(k, g, j):
            row = idx_ref[k * tm + g * ROW_GROUP + j]
            return pltpu.make_async_copy(ys_ref.at[pl.ds(row, 1), :], buf_ref.at[slot, k, g, pl.ds(j, 1), :],
                                         sem.at[slot])
        return copy

    def step(slot):
        @pl.when(i == 0)
        def _():
            _start_row_copies(tm, copies(dest_ref, slot))

        @pl.when(i + 1 < pl.num_programs(0))
        def _():
            _start_row_copies(tm, copies(next_ref, 1 - slot))

        _wait_row_copies(tm, copies(dest_ref, slot))
        h2 = h1_ref[...]
        for k in range(TOP_K):
            h2 = h2 + w_ref[:, k:k + 1] * buf_ref[slot, k].reshape(tm, d)
        ms = jnp.mean(h2 * h2, axis=-1, keepdims=True)
        o_ref[...] = h2 * lax.rsqrt(ms + NORM_EPS) * g_ref[...]

    for slot in range(2):
        pl.when(i % 2 == slot)(functools.partial(step, slot))


def _combine(dest_tiles, h1, w_tk, g, ys, tm):
    t, d = h1.shape
    last = t // tm - 1
    return pl.pallas_call(
        _combine_kernel,
        grid=(t // tm,),
        in_specs=[pl.BlockSpec((TOP_K * tm,), lambda i: (i,), memory_space=pltpu.SMEM),
                  pl.BlockSpec((TOP_K * tm,), lambda i: (jnp.minimum(i + 1, last),), memory_space=pltpu.SMEM),
                  pl.BlockSpec((tm, d), lambda i: (i, 0)),
                  pl.BlockSpec((tm, TOP_K), lambda i: (i, 0)),
                  pl.BlockSpec((1, d), lambda i: (0, 0)),
                  pl.BlockSpec(memory_space=pl.ANY)],
        out_specs=pl.BlockSpec((tm, d), lambda i: (i, 0)),
        out_shape=jax.ShapeDtypeStruct((t, d), F32),
        scratch_shapes=[pltpu.VMEM((2, TOP_K, tm // ROW_GROUP, ROW_GROUP, d), F32),
                        pltpu.SemaphoreType.DMA((2,))],
        compiler_params=_params("arbitrary"),
        name="moe_combine",
    )(dest_tiles, dest_tiles, h1, w_tk, g.reshape(1, d), ys)


def _rope_tables(seq):
    half = ROPE_DIM // 2
    inv_freq = jnp.exp(-math.log(ROPE_THETA) * jnp.arange(half, dtype=F32) / half)
    ang = jnp.arange(seq, dtype=jnp.int32).astype(F32)[:, None] * inv_freq[None, :]
    cos, sin = jnp.cos(ang), jnp.sin(ang)
    zeros = jnp.zeros((seq, HEAD_DIM - ROPE_DIM), F32)
    zh = jnp.zeros((seq, half), F32)
    c = jnp.concatenate([cos, cos, jnp.ones_like(zeros)], axis=1)
    sa = jnp.concatenate([-sin, zh, zeros], axis=1)
    sb = jnp.concatenate([zh, sin, zeros], axis=1)
    return c, sa, sb


def _dest_kernel(base_ref, te_ref, rk_ref, o_ref):
    te = te_ref[...]
    rows = rk_ref[...]
    for e in range(base_ref.shape[0]):
        rows = rows + jnp.where(te == e, base_ref[e], 0)
    o_ref[...] = rows


def _dest_rows(base, top_e, rank, tl):
    k, t = top_e.shape
    blk = pl.BlockSpec((k, tl), lambda i: (0, i))
    return pl.pallas_call(
        _dest_kernel,
        grid=(t // tl,),
        in_specs=[pl.BlockSpec(memory_space=pltpu.SMEM), blk, blk],
        out_specs=blk,
        out_shape=jax.ShapeDtypeStruct((k, t), jnp.int32),
        compiler_params=_params("arbitrary"),
        name="moe_dest_rows",
    )(base, top_e, rank)


def _routing_tables(counts, tm):
    half_count = (counts + tm - 1) // tm
    blk_count = (half_count + MOE_HALVES - 1) // MOE_HALVES
    blk_end = jnp.cumsum(blk_count)
    blk_start = blk_end - blk_count
    first_half = blk_start * MOE_HALVES
    padded = jnp.where(half_count > 0, first_half + half_count - 1, -1)
    unused = jnp.where(half_count < blk_count * MOE_HALVES, first_half + half_count, -1)
    fill = jnp.concatenate([padded, unused, blk_end[-1:] * MOE_HALVES])
    as_i32 = lambda a: a.astype(jnp.int32)
    return as_i32(blk_start), as_i32(blk_count), as_i32(half_count), as_i32(first_half * tm), as_i32(fill)


def kernel(x, norm_mix, w_in, conv_w, w_conv_out, w_attn_out, w_o, norm_ffn, w_router, b_router,
           w_gate, b_gate, w_up, b_up, w_down, b_down, norm_final):
    batch, seq, d = x.shape
    cw = w_conv_out.shape[1]
    aw = w_attn_out.shape[1]
    n_heads = aw // HEAD_DIM
    n_e = w_router.shape[-1]
    f = w_gate.shape[-1]
    t = batch * seq
    assert w_in.shape[0] == 1, "single layer: the final rmsnorm is fused into the MoE combine"
    assert cw == d and aw == d and seq % MOBA_BLOCK == 0

    tm_norm = min(512, t)
    tm_in, tn_in = min(1024, t), min(1024, w_in.shape[-1])
    tm_mid = min(256, seq)
    tm_moe = min(256, t)
    tf_up = min(1024, f)
    tn_down = min(2048, d)
    tm_disp = min(256, t)
    tm_comb = min(256, t)
    tm_item = MOE_HALVES * tm_moe
    n_blocks = (t * TOP_K + n_e * (tm_item - 1)) // tm_item
    col = lambda width_off: width_off // HEAD_DIM

    rope_c, rope_sa, rope_sb = _rope_tables(seq)
    h = x.reshape(t, d)
    xn = _rmsnorm(h, norm_mix[0], tm_norm, BF16)
    proj = _inproj(xn, w_in[0], tm_in, tn_in)
    attn = _attention(proj, rope_c, rope_sa, rope_sb, batch, seq, n_heads,
                      col(3 * cw), col(3 * cw + aw), col(3 * cw + 2 * aw))
    merged = _merge(proj, attn, conv_w[0], w_conv_out[0].astype(BF16), w_attn_out[0].astype(BF16),
                    seq, cw, aw, d, tm_mid)
    h1, xn2, top_e, top_w, rank, counts = _oproj(merged, h, w_o[0].astype(BF16), norm_ffn[0],
                                                 w_router[0].T, b_router[0], tm_mid)
    blk_start, blk_count, half_count, base, fill = _routing_tables(counts[:, 0], tm_moe)
    dest = _dest_rows(base, top_e, rank, min(2048, t))
    xs = _dispatch(_tile_major(dest, tm_disp), fill, xn2, n_blocks * MOE_HALVES, tm_moe, tm_disp)
    up_items = _expert_items(blk_start, blk_count, half_count, f // tf_up, n_blocks)
    hid = _expert_up(up_items, xs, w_gate[0], b_gate[0], w_up[0], b_up[0], tm_item, tf_up)
    same_tiling = f // tf_up == d // tn_down
    down_items = (up_items if same_tiling else
                  _expert_items(blk_start, blk_count, half_count, d // tn_down, n_blocks))
    ys = _expert_down(down_items, hid, w_down[0], b_down[0], tm_item, tn_down)
    out = _combine(_tile_major(dest, tm_comb), h1, top_w.T, norm_final, ys, tm_comb)
    return out.reshape(batch, seq, d)
```

```python
import functools
import math

import jax
import jax.numpy as jnp
from jax import lax
from jax.experimental import pallas as pl
from jax.experimental.pallas import tpu as pltpu

HEAD_DIM = 128
ROPE_DIM = HEAD_DIM // 4
ROPE_THETA = 500000.0
MOBA_BLOCK = 256
MOBA_TOPK = 3
CONV_K = 3
TOP_K = 4
SWIGLU_LIMIT = 7.0
SWIGLU_ALPHA = 1.702
NORM_EPS = 1e-5
NEG = -1e30
ATTN_HEADS_PER_STEP = 4

V7X_LANES = 128
V7X_BF16_SUBLANES = 16
V7X_VMEM_BYTES = 64 * 1024 * 1024
VMEM_LIMIT = V7X_VMEM_BYTES - 4 * 1024 * 1024

F32 = jnp.float32
BF16 = jnp.bfloat16
NT_DIMS = (((1,), (1,)), ((), ()))
NN_DIMS = (((1,), (0,)), ((), ()))


def _params(*sem):
    return pltpu.CompilerParams(dimension_semantics=sem, vmem_limit_bytes=VMEM_LIMIT)


def _split_bf16(a):
    hi = a.astype(BF16)
    return hi, (a - hi.astype(F32)).astype(BF16)


def _dot3(a, b, dims):
    ah, al = _split_bf16(a)
    bh, bl = _split_bf16(b)
    rows = a.shape[0]
    dot = lambda u, v: lax.dot_general(u, v, dims, preferred_element_type=F32)
    both = dot(jnp.concatenate([ah, al], axis=0), bh)
    return both[:rows] + (both[rows:] + dot(ah, bl))


def _rmsnorm_kernel(x_ref, g_ref, o_ref):
    x = x_ref[...]
    ms = jnp.mean(x * x, axis=-1, keepdims=True)
    o_ref[...] = (x * lax.rsqrt(ms + NORM_EPS) * g_ref[...]).astype(o_ref.dtype)


def _rmsnorm(x, g, tm, out_dtype):
    t, d = x.shape
    return pl.pallas_call(
        _rmsnorm_kernel,
        grid=(t // tm,),
        in_specs=[pl.BlockSpec((tm, d), lambda i: (i, 0)), pl.BlockSpec((1, d), lambda i: (0, 0))],
        out_specs=pl.BlockSpec((tm, d), lambda i: (i, 0)),
        out_shape=jax.ShapeDtypeStruct((t, d), out_dtype),
        compiler_params=_params("arbitrary"),
        name="rmsnorm",
    )(x, g.reshape(1, d))


def _inproj_kernel(x_ref, w_ref, o_ref, wbf_ref):
    @pl.when(pl.program_id(1) == 0)
    def _():
        wbf_ref[...] = w_ref[...].astype(BF16)

    o_ref[...] = jnp.dot(x_ref[...], wbf_ref[...], preferred_element_type=F32).astype(o_ref.dtype)


def _inproj(xn, w, tm, tn):
    t, d = xn.shape
    n = w.shape[1]
    return pl.pallas_call(
        _inproj_kernel,
        grid=(n // tn, t // tm),
        in_specs=[pl.BlockSpec((tm, d), lambda j, i: (i, 0)), pl.BlockSpec((d, tn), lambda j, i: (0, j))],
        out_specs=pl.BlockSpec((tm, tn), lambda j, i: (i, j)),
        out_shape=jax.ShapeDtypeStruct((t, n), BF16),
        scratch_shapes=[pltpu.VMEM((d, tn), BF16)],
        compiler_params=_params("arbitrary", "arbitrary"),
        name="inproj",
    )(xn, w)


def _rope(t, c, sa, sb):
    half = ROPE_DIM // 2
    return t * c + pltpu.roll(t, HEAD_DIM - half, 1) * sa + pltpu.roll(t, half, 1) * sb


def _attn_kernel(q_ref, k_ref, v_ref, c_ref, sa_ref, sb_ref, o_ref,
                 kaug_ref, vt_ref, kmean_ref, m_ref, l_ref, acc_ref, qaug_ref, s0_ref, s1_ref,
                 *, n_kb, kb_pad, n_hd, exp_scale):
    blk = MOBA_BLOCK
    qi = pl.program_id(1)
    heads = range(n_hd)
    hcols = lambda hd: slice(hd * HEAD_DIM, (hd + 1) * HEAD_DIM)

    @pl.when(qi == 0)
    def _prep():
        kmean_ref[...] = jnp.zeros_like(kmean_ref)
        qaug_ref[...] = jnp.zeros_like(qaug_ref)
        lane = lax.broadcasted_iota(jnp.int32, (blk, HEAD_DIM), 1)

        def body(j, carry):
            rows = pl.ds(pl.multiple_of(j * blk, blk), blk)
            onehot = jnp.where(lane == j, 1.0, 0.0).astype(BF16)
            for hd in heads:
                kr = _rope(k_ref[rows, hcols(hd)].astype(F32), c_ref[rows, :], sa_ref[rows, :], sb_ref[rows, :])
                kaug_ref[hd, rows, 0:HEAD_DIM] = kr.astype(BF16)
                kaug_ref[hd, rows, HEAD_DIM:2 * HEAD_DIM] = onehot
                kmean_ref[hd, pl.ds(j, 1), :] = jnp.mean(kr, axis=0, keepdims=True)
                vt_ref[hd, j] = v_ref[rows, hcols(hd)].astype(F32).T.astype(BF16)
            return carry

        lax.fori_loop(0, n_kb, body, 0)

    rows = pl.ds(pl.multiple_of(qi * blk, blk), blk)
    sub = lax.broadcasted_iota(jnp.int32, (kb_pad, blk), 0)
    valid = sub < qi
    key_i = lax.broadcasted_iota(jnp.int32, (blk, blk), 0)
    qry_i = lax.broadcasted_iota(jnp.int32, (blk, blk), 1)
    own_scores = []
    for hd in heads:
        qr = _rope(q_ref[:, hcols(hd)].astype(F32), c_ref[rows, :], sa_ref[rows, :], sb_ref[rows, :])
        qrt = qr.T
        qrt_bf = (qrt * exp_scale).astype(BF16)
        qaug_ref[hd, 0:HEAD_DIM, :] = qrt_bf
        own_scores.append(jnp.dot(kaug_ref[hd, rows, 0:HEAD_DIM], qrt_bf, preferred_element_type=F32))

        gate = _dot3(kmean_ref[hd], qrt, NN_DIMS)
        g = jnp.where(valid, gate, -jnp.inf)
        bias = jnp.full((kb_pad, blk), NEG, F32)
        for _ in range(MOBA_TOPK):
            top = jnp.max(g, axis=0, keepdims=True)
            idx = jnp.min(jnp.where(g == top, sub, kb_pad), axis=0, keepdims=True)
            pick = sub == idx
            bias = jnp.where(pick, jnp.where(valid, 0.0, bias), bias)
            g = jnp.where(pick, -jnp.inf, g)
        qaug_ref[hd, HEAD_DIM:HEAD_DIM + kb_pad, :] = bias.astype(BF16)

    def scores(trip, s_ref):
        keys = pl.ds(pl.multiple_of(trip * 2 * blk, 2 * blk), 2 * blk)
        for hd in heads:
            s_ref[hd] = jnp.dot(kaug_ref[hd, keys, :], qaug_ref[hd], preferred_element_type=F32)

    def consume(trip, s_ref):
        j0 = 2 * trip
        pbs, alphas = [], []
        for hd in heads:
            sj = s_ref[hd]
            m_prev = m_ref[hd]
            m_new = jnp.maximum(m_prev, jnp.max(sj, axis=0, keepdims=True))
            alpha = jnp.exp2(m_prev - m_new)
            pj = jnp.exp2(sj - m_new)
            l_ref[hd] = alpha * l_ref[hd] + jnp.sum(pj, axis=0, keepdims=True)
            m_ref[hd] = m_new
            pbs.append(pj.astype(BF16))
            alphas.append(alpha)
        for hd in heads:
            pv = (jnp.dot(vt_ref[hd, j0], pbs[hd][0:blk], preferred_element_type=F32)
                  + jnp.dot(vt_ref[hd, j0 + 1], pbs[hd][blk:2 * blk], preferred_element_type=F32))
            acc_ref[hd] = alphas[hd] * acc_ref[hd] + pv

    def trip_ahead(trip, cur_ref, nxt_ref):
        scores(trip + 1, nxt_ref)
        consume(trip, cur_ref)

    scores(0, s0_ref)
    own_ps = []
    for hd in heads:
        s = jnp.where(key_i <= qry_i, own_scores[hd], NEG)
        m0 = jnp.max(s, axis=0, keepdims=True)
        p = jnp.exp2(s - m0)
        m_ref[hd] = m0
        l_ref[hd] = jnp.sum(p, axis=0, keepdims=True)
        own_ps.append(p.astype(BF16))
    for hd in heads:
        acc_ref[hd] = jnp.dot(vt_ref[hd, qi], own_ps[hd], preferred_element_type=F32)

    n_trips = (qi + 1) // 2

    @pl.when(n_trips > 0)
    def _():
        def two_trips(r, carry):
            trip_ahead(2 * r, s0_ref, s1_ref)
            trip_ahead(2 * r + 1, s1_ref, s0_ref)
            return carry

        lax.fori_loop(0, (n_trips - 1) // 2, two_trips, 0)
        last = n_trips - 1

        @pl.when(n_trips % 2 == 1)
        def _():
            consume(last, s0_ref)

        @pl.when(n_trips % 2 == 0)
        def _():
            trip_ahead(last - 1, s0_ref, s1_ref)
            consume(last, s1_ref)

    for hd in heads:
        o_ref[:, hcols(hd)] = (acc_ref[hd] / l_ref[hd]).T.astype(o_ref.dtype)


def _attention(proj, rope_c, rope_sa, rope_sb, batch, seq, n_heads, q_col, k_col, v_col):
    t = proj.shape[0]
    blk = MOBA_BLOCK
    n_kb = seq // blk
    kb_pad = -(-n_kb // V7X_BF16_SUBLANES) * V7X_BF16_SUBLANES
    n_hd = ATTN_HEADS_PER_STEP
    wide = n_hd * HEAD_DIM
    assert n_kb % 2 == 0 and kb_pad <= HEAD_DIM and n_heads % n_hd == 0
    assert q_col % n_hd == 0 and k_col % n_hd == 0 and v_col % n_hd == 0
    groups = n_heads // n_hd
    kernel = functools.partial(_attn_kernel, n_kb=n_kb, kb_pad=kb_pad, n_hd=n_hd,
                               exp_scale=HEAD_DIM ** -0.5 * math.log2(math.e))
    tab = pl.BlockSpec((seq, HEAD_DIM), lambda bg, qi: (0, 0))
    return pl.pallas_call(
        kernel,
        grid=(batch * groups, n_kb),
        in_specs=[
            pl.BlockSpec((blk, wide), lambda bg, qi: ((bg // groups) * n_kb + qi, q_col // n_hd + bg % groups)),
            pl.BlockSpec((seq, wide), lambda bg, qi: (bg // groups, k_col // n_hd + bg % groups)),
            pl.BlockSpec((seq, wide), lambda bg, qi: (bg // groups, v_col // n_hd + bg % groups)),
            tab, tab, tab,
        ],
        out_specs=pl.BlockSpec((blk, wide), lambda bg, qi: ((bg // groups) * n_kb + qi, bg % groups)),
        out_shape=jax.ShapeDtypeStruct((t, n_heads * HEAD_DIM), BF16),
        scratch_shapes=[
            pltpu.VMEM((n_hd, seq, 2 * HEAD_DIM), BF16),
            pltpu.VMEM((n_hd, n_kb, HEAD_DIM, blk), BF16),
            pltpu.VMEM((n_hd, kb_pad, HEAD_DIM), F32),
            pltpu.VMEM((n_hd, 1, blk), F32),
            pltpu.VMEM((n_hd, 1, blk), F32),
            pltpu.VMEM((n_hd, HEAD_DIM, blk), F32),
            pltpu.VMEM((n_hd, 2 * HEAD_DIM, blk), BF16),
            pltpu.VMEM((n_hd, 2 * blk, blk), F32),
            pltpu.VMEM((n_hd, 2 * blk, blk), F32),
        ],
        compiler_params=_params("arbitrary", "arbitrary"),
        name="moba_attention",
    )(proj, proj, proj, rope_c, rope_sa, rope_sb)


def _merge_kernel(b_ref, c_ref, h_ref, cp_ref, hp_ref, gc_ref, ga_ref, at_ref, cw_ref, wc_ref, wa_ref,
                  o_ref, *, tiles_per_seq):
    hal = V7X_BF16_SUBLANES
    u = c_ref[...].astype(F32) * h_ref[...].astype(F32)
    up = cp_ref[...].astype(F32) * hp_ref[...].astype(F32)
    up = jnp.where(pl.program_id(0) % tiles_per_seq == 0, 0.0, up)
    row = lax.broadcasted_iota(jnp.int32, u.shape, 0)
    u1 = jnp.where(row == 0, up[hal - 1:hal, :], pltpu.roll(u, 1, 0))
    u2 = jnp.where(row == 0, up[hal - 2:hal - 1, :],
                   jnp.where(row == 1, up[hal - 1:hal, :], pltpu.roll(u, 2, 0)))
    y = cw_ref[0:1, :] * u2 + cw_ref[1:2, :] * u1 + cw_ref[2:3, :] * u
    cm = (b_ref[...].astype(F32) * y).astype(BF16)
    yc = jnp.dot(cm, wc_ref[...], preferred_element_type=F32)
    ya = jnp.dot(at_ref[...], wa_ref[...], preferred_element_type=F32)
    merged = jax.nn.sigmoid(gc_ref[...].astype(F32)) * yc + jax.nn.sigmoid(ga_ref[...].astype(F32)) * ya
    o_ref[...] = merged.astype(o_ref.dtype)


def _merge(proj, attn, conv_w, wc_bf, wa_bf, seq, cw, aw, d, tm):
    t = proj.shape[0]
    hal = V7X_BF16_SUBLANES
    kernel = functools.partial(_merge_kernel, tiles_per_seq=seq // tm)
    col = lambda cidx: pl.BlockSpec((tm, cw), lambda i: (i, cidx))
    halo = lambda cidx: pl.BlockSpec((hal, cw), lambda i: (jnp.maximum(i * (tm // hal) - 1, 0), cidx))
    const = lambda shape: pl.BlockSpec(shape, lambda i: (0, 0))
    return pl.pallas_call(
        kernel,
        grid=(t // tm,),
        in_specs=[col(0), col(1), col(2), halo(1), halo(2), col(6), col(7),
                  pl.BlockSpec((tm, aw), lambda i: (i, 0)),
                  const((CONV_K, cw)), const((cw, d)), const((aw, d))],
        out_specs=pl.BlockSpec((tm, d), lambda i: (i, 0)),
        out_shape=jax.ShapeDtypeStruct((t, d), BF16),
        compiler_params=_params("arbitrary"),
        name="merge_branches",
    )(proj, proj, proj, proj, proj, proj, proj, attn, conv_w, wc_bf, wa_bf)


def _oproj_kernel(mg_ref, x_ref, wo_ref, g_ref, wrt_ref, br_ref,
                  h1_ref, xn_ref, te_ref, tw_ref, rk_ref, cnt_ref, run_ref):
    h1 = x_ref[...] + jnp.dot(mg_ref[...], wo_ref[...], preferred_element_type=F32)
    h1_ref[...] = h1
    ms = jnp.mean(h1 * h1, axis=-1, keepdims=True)
    xn = h1 * lax.rsqrt(ms + NORM_EPS) * g_ref[...]
    xn_ref[...] = xn
    lg = _dot3(wrt_ref[...], xn, NT_DIMS) + br_ref[...]
    n_e, tm = lg.shape
    sub = lax.broadcasted_iota(jnp.int32, lg.shape, 0)
    vals, idxs = [], []
    for _ in range(TOP_K):
        top = jnp.max(lg, axis=0, keepdims=True)
        idx = jnp.min(jnp.where(lg == top, sub, n_e), axis=0, keepdims=True)
        vals.append(top)
        idxs.append(idx)
        lg = jnp.where(sub == idx, -jnp.inf, lg)
    v = jnp.concatenate(vals, axis=0)
    ex = jnp.exp(v - vals[0])
    te_ref[...] = jnp.concatenate(idxs, axis=0)
    tw_ref[...] = ex / jnp.sum(ex, axis=0, keepdims=True)

    @pl.when(pl.program_id(0) == 0)
    def _():
        run_ref[...] = jnp.zeros_like(run_ref)

    upper = jnp.where(lax.broadcasted_iota(jnp.int32, (tm, tm), 0) <= lax.broadcasted_iota(jnp.int32, (tm, tm), 1),
                      1.0, 0.0).astype(BF16)
    run = run_ref[...]
    ranks = []
    for k in range(TOP_K):
        onehot = jnp.where(sub == idxs[k], 1.0, 0.0)
        incl = jnp.dot(onehot.astype(BF16), upper, preferred_element_type=F32)
        ranks.append(jnp.sum(onehot * (incl - 1.0 + run[:, 0:1]), axis=0, keepdims=True))
        run = run + incl[:, tm - 1:tm]
    run_ref[...] = run
    rk_ref[...] = jnp.concatenate(ranks, axis=0).astype(jnp.int32)
    cnt_ref[...] = run.astype(jnp.int32)


def _oproj(merged, x2, wo_bf, g, wr_t, br, tm):
    t, d = x2.shape
    n_e = wr_t.shape[0]
    const = lambda shape: pl.BlockSpec(shape, lambda i: (0, 0))
    row = pl.BlockSpec((tm, d), lambda i: (i, 0))
    top = pl.BlockSpec((TOP_K, tm), lambda i: (0, i))
    return pl.pallas_call(
        _oproj_kernel,
        grid=(t // tm,),
        in_specs=[row, row, const((d, d)), const((1, d)), const((n_e, d)), const((n_e, 1))],
        out_specs=[row, row, top, top, top, const((n_e, V7X_LANES))],
        out_shape=[jax.ShapeDtypeStruct((t, d), F32), jax.ShapeDtypeStruct((t, d), F32),
                   jax.ShapeDtypeStruct((TOP_K, t), jnp.int32), jax.ShapeDtypeStruct((TOP_K, t), F32),
                   jax.ShapeDtypeStruct((TOP_K, t), jnp.int32), jax.ShapeDtypeStruct((n_e, V7X_LANES), jnp.int32)],
        scratch_shapes=[pltpu.VMEM((n_e, V7X_LANES), F32)],
        compiler_params=_params("arbitrary"),
        name="oproj_router",
    )(merged, x2, wo_bf, g.reshape(1, d), wr_t, br.reshape(n_e, 1))


def _dispatch_kernel(dest_ref, fill_ref, x_hbm, xs_ref, zero_ref, x_buf, sem, lsem, zsem,
                     *, n_e, n_blocks, tm_moe, tm):
    @pl.when(pl.program_id(0) == 0)
    def _():
        zero_ref[...] = jnp.zeros_like(zero_ref)

        def zcopy(b):
            return pltpu.make_async_copy(
                zero_ref, xs_ref.at[pl.ds(pl.multiple_of(b * tm_moe, tm_moe), tm_moe), :], zsem)

        def zstart(b, carry):
            zcopy(b).start()
            return carry

        def zwait(b, carry):
            zcopy(b).wait()
            return carry

        for e in range(n_e):
            @pl.when(fill_ref[e] >= 0)
            def _():
                zcopy(fill_ref[e]).start()
        lax.fori_loop(fill_ref[n_e], n_blocks, zstart, 0)
        for e in range(n_e):
            @pl.when(fill_ref[e] >= 0)
            def _():
                zcopy(fill_ref[e]).wait()
        lax.fori_loop(fill_ref[n_e], n_blocks, zwait, 0)

    i = pl.program_id(0)
    groups = tm // ROW_GROUP

    def load(tile, slot):
        rows = pl.ds(pl.multiple_of(tile * groups, groups), groups)
        return pltpu.make_async_copy(x_hbm.at[rows], x_buf.at[slot], lsem.at[slot])

    def scatters(slot):
        def copy(k, g, j):
            row = dest_ref[k * tm + g * ROW_GROUP + j]
            return pltpu.make_async_copy(x_buf.at[slot, g, pl.ds(j, 1), :], xs_ref.at[pl.ds(row, 1), :],
                                         sem.at[slot])
        return copy

    def step(slot):
        nxt, prv = (slot + 1) % DISPATCH_BUFFERS, (slot - 1) % DISPATCH_BUFFERS

        @pl.when(i == 0)
        def _():
            load(i, slot).start()

        @pl.when(i + 1 < pl.num_programs(0))
        def _():
            load(i + 1, nxt).start()

        load(i, slot).wait()
        _start_row_copies(tm, scatters(slot))

        @pl.when(i > 0)
        def _():
            _wait_row_copies(tm, scatters(prv))

        @pl.when(i + 1 == pl.num_programs(0))
        def _():
            _wait_row_copies(tm, scatters(slot))

    for slot in range(DISPATCH_BUFFERS):
        pl.when(i % DISPATCH_BUFFERS == slot)(functools.partial(step, slot))


DISPATCH_BUFFERS = 3

ROW_GROUP = 8


def _start_row_copies(tm, copy):
    def start(g, carry):
        for j in range(ROW_GROUP):
            for k in range(TOP_K):
                copy(k, g, j).start(priority=(j * TOP_K + k) % 2)
        return carry

    lax.fori_loop(0, tm // ROW_GROUP, start, 0)


def _wait_row_copies(tm, copy):
    def wait(g, carry):
        for j in range(ROW_GROUP):
            for k in range(TOP_K):
                copy(k, g, j).wait()
        return carry

    lax.fori_loop(0, tm // ROW_GROUP, wait, 0)


def _for_each_row_copy(tm, copy):
    _start_row_copies(tm, copy)
    _wait_row_copies(tm, copy)


def _tile_major(dest, tm):
    k, t = dest.shape
    return dest.reshape(k, t // tm, tm).transpose(1, 0, 2).reshape(-1)


def _dispatch(dest_tiles, fill, xn, n_blocks, tm_moe, tm):
    t, d = xn.shape
    n_e = fill.shape[0] - 1
    kernel = functools.partial(_dispatch_kernel, n_e=n_e, n_blocks=n_blocks, tm_moe=tm_moe, tm=tm)
    return pl.pallas_call(
        kernel,
        grid=(t // tm,),
        in_specs=[pl.BlockSpec((TOP_K * tm,), lambda i: (i,), memory_space=pltpu.SMEM),
                  pl.BlockSpec(memory_space=pltpu.SMEM),
                  pl.BlockSpec(memory_space=pl.ANY)],
        out_specs=pl.BlockSpec(memory_space=pl.ANY),
        out_shape=jax.ShapeDtypeStruct((n_blocks * tm_moe, d), F32),
        scratch_shapes=[pltpu.VMEM((tm_moe, d), F32),
                        pltpu.VMEM((DISPATCH_BUFFERS, tm // ROW_GROUP, ROW_GROUP, d), F32),
                        pltpu.SemaphoreType.DMA((DISPATCH_BUFFERS,)), pltpu.SemaphoreType.DMA((DISPATCH_BUFFERS,)),
                        pltpu.SemaphoreType.DMA(())],
        compiler_params=_params("arbitrary"),
        name="moe_dispatch",
    )(dest_tiles, fill, xn.reshape(t // ROW_GROUP, ROW_GROUP, d))


WEIGHT_K_CHUNKS = 4


def _dot_rounding_tile(x, stage_ref, slot, which, bf_ref):
    kc = bf_ref.shape[0] // WEIGHT_K_CHUNKS
    acc = None
    for c in range(WEIGHT_K_CHUNKS):
        rows = slice(c * kc, (c + 1) * kc)
        w = stage_ref[slot, which, rows, :].astype(BF16)
        bf_ref[rows, :] = w
        part = jnp.dot(x[:, rows], w, preferred_element_type=F32)
        acc = part if acc is None else acc + part
    return acc


N_ITEM_ARRAYS = 12
MOE_HALVES = 2


def _expert_items(blk_start, blk_count, half_count, n_col, n_blocks):
    n_e = blk_count.shape[0]
    n_items = n_blocks * n_col
    item_end = jnp.cumsum(blk_count * n_col)
    item_start = item_end - blk_count * n_col
    n_used = item_end[-1]
    blocks_used = jnp.sum(blk_count)
    i = jnp.arange(n_items, dtype=jnp.int32)
    ok = i < n_used
    nonempty_before = jnp.cumsum(blk_count > 0) - (blk_count > 0)
    experts = jnp.arange(n_e, dtype=jnp.int32)

    def at(idx):
        ic = jnp.clip(idx, 0, jnp.maximum(n_used - 1, 0))
        e = jnp.sum(ic[:, None] >= item_end[None, :], axis=1).astype(jnp.int32)
        mine = e[:, None] == experts[None, :]
        pick = lambda table: jnp.sum(jnp.where(mine, table[None, :], 0), axis=1)
        local = ic - pick(item_start)
        cnt = jnp.maximum(pick(blk_count), 1)
        first_blk = pick(blk_start)
        second_half = (local % cnt) * MOE_HALVES + 1 < pick(half_count)
        return e, local // cnt, first_blk + local % cnt, cnt, first_blk, pick(nonempty_before), second_half

    e, wcol, xblk, cnt, first_blk, groups_before, second_half = at(i)
    spare = i - n_used
    oblk = jnp.where(ok, xblk, blocks_used + spare // n_col)
    ocol = jnp.where(ok, wcol, spare % n_col)
    new = ok & (xblk == first_blk)
    slot = (groups_before * n_col + wcol) % 2
    nxt = i + cnt
    nxt_e, nxt_col = at(nxt)[:2]
    has_nxt = new & (nxt < n_used)
    as_i32 = lambda a: a.astype(jnp.int32)
    items = tuple(map(as_i32, (xblk, oblk, wcol, ocol, e, new, ok, slot, nxt_e, nxt_col, has_nxt,
                               ok & second_half)))
    assert len(items) == N_ITEM_ARRAYS and n_e == item_end.shape[0]
    return items


def _weight_tile_copy(w_hbm, stage_ref, sem, e, col, slot, which, tn):
    cols = pl.ds(pl.multiple_of(col * tn, tn), tn)
    return pltpu.make_async_copy(w_hbm.at[e, :, cols], stage_ref.at[slot, which], sem.at[slot, which])


def _stream_weight_tiles(i, items, w_hbms, stage_ref, sem, tn):
    _, _, wcol, _, exp, new, _, slot, nxt_e, nxt_col, has_nxt, _ = items

    @pl.when(new[i] == 1)
    def _():
        s = slot[i]

        @pl.when(i == 0)
        def _():
            for which, w in enumerate(w_hbms):
                _weight_tile_copy(w, stage_ref, sem, exp[i], wcol[i], s, which, tn).start(priority=1)

        for which, w in enumerate(w_hbms):
            _weight_tile_copy(w, stage_ref, sem, exp[i], wcol[i], s, which, tn).wait()

        @pl.when(has_nxt[i] == 1)
        def _():
            for which, w in enumerate(w_hbms):
                _weight_tile_copy(w, stage_ref, sem, nxt_e[i], nxt_col[i], 1 - s, which, tn).start(priority=1)


def _run_item_kind(new, ok, second, plain, rounding, o_ref, half):
    both, lo = slice(0, MOE_HALVES * half), slice(0, half)
    kind = ok * (2 * new + second + 1) - 1
    pl.when(kind == 3)(functools.partial(rounding, both))
    pl.when(kind == 2)(functools.partial(rounding, lo))
    pl.when(kind == 1)(functools.partial(plain, both))
    pl.when(kind == 0)(functools.partial(plain, lo))

    @pl.when((kind == 0) | (kind == 2))
    def _():
        o_ref[half:MOE_HALVES * half, :] = jnp.zeros(((MOE_HALVES - 1) * half, o_ref.shape[1]), o_ref.dtype)

    @pl.when(kind < 0)
    def _():
        o_ref[...] = jnp.zeros_like(o_ref)


def _up_kernel(*refs, n_col):
    items = refs[:N_ITEM_ARRAYS]
    x_ref, wg_hbm, wu_hbm, bg_ref, bu_ref, o_ref, stage_ref, wgb_ref, wub_ref, sem = refs[N_ITEM_ARRAYS:]
    wcol, exp, new, ok, slot, second = items[2], items[4], items[5], items[6], items[7], items[11]
    i = pl.program_id(0)
    _stream_weight_tiles(i, items, (wg_hbm, wu_hbm), stage_ref, sem, wgb_ref.shape[1])
    tile = exp[i] * n_col + wcol[i]
    half = x_ref.shape[0] // MOE_HALVES

    def finish(rows, g, u):
        g = jnp.minimum(g + bg_ref[pl.ds(tile, 1), :], SWIGLU_LIMIT)
        u = jnp.clip(u + bu_ref[pl.ds(tile, 1), :], -SWIGLU_LIMIT, SWIGLU_LIMIT)
        o_ref[rows, :] = ((u + 1.0) * (g * jax.nn.sigmoid(SWIGLU_ALPHA * g))).astype(o_ref.dtype)

    def plain(rows):
        x = x_ref[rows, :].astype(BF16)
        finish(rows, jnp.dot(x, wgb_ref[...], preferred_element_type=F32),
               jnp.dot(x, wub_ref[...], preferred_element_type=F32))

    def rounding(rows):
        x = x_ref[rows, :].astype(BF16)
        finish(rows, _dot_rounding_tile(x, stage_ref, slot[i], 0, wgb_ref),
               _dot_rounding_tile(x, stage_ref, slot[i], 1, wub_ref))

    _run_item_kind(new[i], ok[i], second[i], plain, rounding, o_ref, half)


def _down_kernel(*refs, n_col):
    items = refs[:N_ITEM_ARRAYS]
    h_ref, wd_hbm, bd_ref, o_ref, stage_ref, wdb_ref, sem = refs[N_ITEM_ARRAYS:]
    wcol, exp, new, ok, slot, second = items[2], items[4], items[5], items[6], items[7], items[11]
    i = pl.program_id(0)
    _stream_weight_tiles(i, items, (wd_hbm,), stage_ref, sem, wdb_ref.shape[1])
    tile = exp[i] * n_col + wcol[i]
    half = h_ref.shape[0] // MOE_HALVES

    def plain(rows):
        o_ref[rows, :] = (jnp.dot(h_ref[rows, :], wdb_ref[...], preferred_element_type=F32)
                          + bd_ref[pl.ds(tile, 1), :])

    def rounding(rows):
        o_ref[rows, :] = (_dot_rounding_tile(h_ref[rows, :], stage_ref, slot[i], 0, wdb_ref)
                          + bd_ref[pl.ds(tile, 1), :])

    _run_item_kind(new[i], ok[i], second[i], plain, rounding, o_ref, half)


def _item_map(fn):
    return lambda i, *items: fn(i, *items[:5])


def _expert_up(items, xs, w_gate, b_gate, w_up, b_up, tm, tf):
    rows, d = xs.shape
    n_e, _, f = w_gate.shape
    n_items = items[0].shape[0]
    n_col = f // tf
    bspec = pl.BlockSpec((n_e * n_col, tf), lambda i, *_: (0, 0))
    hbm = pl.BlockSpec(memory_space=pl.ANY)
    grid_spec = pltpu.PrefetchScalarGridSpec(
        num_scalar_prefetch=N_ITEM_ARRAYS,
        grid=(n_items,),
        in_specs=[pl.BlockSpec((tm, d), _item_map(lambda i, xb, ob, wc, oc, e: (xb[i], 0))),
                  hbm, hbm, bspec, bspec],
        out_specs=pl.BlockSpec((tm, tf), _item_map(lambda i, xb, ob, wc, oc, e: (ob[i], oc[i]))),
        scratch_shapes=[pltpu.VMEM((2, 2, d, tf), F32), pltpu.VMEM((d, tf), BF16), pltpu.VMEM((d, tf), BF16),
                        pltpu.SemaphoreType.DMA((2, 2))],
    )
    return pl.pallas_call(
        functools.partial(_up_kernel, n_col=n_col),
        grid_spec=grid_spec,
        out_shape=jax.ShapeDtypeStruct((rows, f), BF16),
        compiler_params=_params("arbitrary"),
        name="moe_up",
    )(*items, xs, w_gate, w_up, b_gate.reshape(n_e * n_col, tf), b_up.reshape(n_e * n_col, tf))


def _expert_down(items, hid, w_down, b_down, tm, tn):
    rows, f = hid.shape
    n_e, _, d = w_down.shape
    n_items = items[0].shape[0]
    n_col = d // tn
    grid_spec = pltpu.PrefetchScalarGridSpec(
        num_scalar_prefetch=N_ITEM_ARRAYS,
        grid=(n_items,),
        in_specs=[pl.BlockSpec((tm, f), _item_map(lambda i, xb, ob, wc, oc, e: (xb[i], 0))),
                  pl.BlockSpec(memory_space=pl.ANY),
                  pl.BlockSpec((n_e * n_col, tn), lambda i, *_: (0, 0))],
        out_specs=pl.BlockSpec((tm, tn), _item_map(lambda i, xb, ob, wc, oc, e: (ob[i], oc[i]))),
        scratch_shapes=[pltpu.VMEM((2, 1, f, tn), F32), pltpu.VMEM((f, tn), BF16),
                        pltpu.SemaphoreType.DMA((2, 1))],
    )
    return pl.pallas_call(
        functools.partial(_down_kernel, n_col=n_col),
        grid_spec=grid_spec,
        out_shape=jax.ShapeDtypeStruct((rows, d), F32),
        compiler_params=_params("arbitrary"),
        name="moe_down",
    )(*items, hid, w_down, b_down.reshape(n_e * n_col, tn))


def _combine_kernel(dest_ref, next_ref, h1_ref, w_ref, g_ref, ys_ref, o_ref, buf_ref, sem):
    tm, d = h1_ref.shape
    i = pl.program_id(0)

    def copies(idx_ref, slot):
        def copy(k, g, j):
            row = idx_ref[k * tm + g * ROW_GROUP + j]
            return pltpu.make_async_copy(ys_ref.at[pl.ds(row, 1), :], buf_ref.at[slot, k, g, pl.ds(j, 1), :],
                                         sem.at[slot])
        return copy

    def step(slot):
        @pl.when(i == 0)
        def _():
            _start_row_copies(tm, copies(dest_ref, slot))

        @pl.when(i + 1 < pl.num_programs(0))
        def _():
            _start_row_copies(tm, copies(next_ref, 1 - slot))

        _wait_row_copies(tm, copies(dest_ref, slot))
        h2 = h1_ref[...]
        for k in range(TOP_K):
            h2 = h2 + w_ref[:, k:k + 1] * buf_ref[slot, k].reshape(tm, d)
        ms = jnp.mean(h2 * h2, axis=-1, keepdims=True)
        o_ref[...] = h2 * lax.rsqrt(ms + NORM_EPS) * g_ref[...]

    for slot in range(2):
        pl.when(i % 2 == slot)(functools.partial(step, slot))


def _combine(dest_tiles, h1, w_tk, g, ys, tm):
    t, d = h1.shape
    last = t // tm - 1
    return pl.pallas_call(
        _combine_kernel,
        grid=(t // tm,),
        in_specs=[pl.BlockSpec((TOP_K * tm,), lambda i: (i,), memory_space=pltpu.SMEM),
                  pl.BlockSpec((TOP_K * tm,), lambda i: (jnp.minimum(i + 1, last),), memory_space=pltpu.SMEM),
                  pl.BlockSpec((tm, d), lambda i: (i, 0)),
                  pl.BlockSpec((tm, TOP_K), lambda i: (i, 0)),
                  pl.BlockSpec((1, d), lambda i: (0, 0)),
                  pl.BlockSpec(memory_space=pl.ANY)],
        out_specs=pl.BlockSpec((tm, d), lambda i: (i, 0)),
        out_shape=jax.ShapeDtypeStruct((t, d), F32),
        scratch_shapes=[pltpu.VMEM((2, TOP_K, tm // ROW_GROUP, ROW_GROUP, d), F32),
                        pltpu.SemaphoreType.DMA((2,))],
        compiler_params=_params("arbitrary"),
        name="moe_combine",
    )(dest_tiles, dest_tiles, h1, w_tk, g.reshape(1, d), ys)


def _rope_tables(seq):
    half = ROPE_DIM // 2
    inv_freq = jnp.exp(-math.log(ROPE_THETA) * jnp.arange(half, dtype=F32) / half)
    ang = jnp.arange(seq, dtype=jnp.int32).astype(F32)[:, None] * inv_freq[None, :]
    cos, sin = jnp.cos(ang), jnp.sin(ang)
    zeros = jnp.zeros((seq, HEAD_DIM - ROPE_DIM), F32)
    zh = jnp.zeros((seq, half), F32)
    c = jnp.concatenate([cos, cos, jnp.ones_like(zeros)], axis=1)
    sa = jnp.concatenate([-sin, zh, zeros], axis=1)
    sb = jnp.concatenate([zh, sin, zeros], axis=1)
    return c, sa, sb


def _dest_kernel(base_ref, te_ref, rk_ref, o_ref):
    te = te_ref[...]
    rows = rk_ref[...]
    for e in range(base_ref.shape[0]):
        rows = rows + jnp.where(te == e, base_ref[e], 0)
    o_ref[...] = rows


def _dest_rows(base, top_e, rank, tl):
    k, t = top_e.shape
    blk = pl.BlockSpec((k, tl), lambda i: (0, i))
    return pl.pallas_call(
        _dest_kernel,
        grid=(t // tl,),
        in_specs=[pl.BlockSpec(memory_space=pltpu.SMEM), blk, blk],
        out_specs=blk,
        out_shape=jax.ShapeDtypeStruct((k, t), jnp.int32),
        compiler_params=_params("arbitrary"),
        name="moe_dest_rows",
    )(base, top_e, rank)


def _routing_tables(counts, tm):
    half_count = (counts + tm - 1) // tm
    blk_count = (half_count + MOE_HALVES - 1) // MOE_HALVES
    blk_end = jnp.cumsum(blk_count)
    blk_start = blk_end - blk_count
    first_half = blk_start * MOE_HALVES
    padded = jnp.where(half_count > 0, first_half + half_count - 1, -1)
    unused = jnp.where(half_count < blk_count * MOE_HALVES, first_half + half_count, -1)
    fill = jnp.concatenate([padded, unused, blk_end[-1:] * MOE_HALVES])
    as_i32 = lambda a: a.astype(jnp.int32)
    return as_i32(blk_start), as_i32(blk_count), as_i32(half_count), as_i32(first_half * tm), as_i32(fill)


def kernel(x, norm_mix, w_in, conv_w, w_conv_out, w_attn_out, w_o, norm_ffn, w_router, b_router,
           w_gate, b_gate, w_up, b_up, w_down, b_down, norm_final):
    batch, seq, d = x.shape
    cw = w_conv_out.shape[1]
    aw = w_attn_out.shape[1]
    n_heads = aw // HEAD_DIM
    n_e = w_router.shape[-1]
    f = w_gate.shape[-1]
    t = batch * seq
    assert w_in.shape[0] == 1, "single layer: the final rmsnorm is fused into the MoE combine"
    assert cw == d and aw == d and seq % MOBA_BLOCK == 0

    tm_norm = min(512, t)
    tm_in, tn_in = min(2048, t), min(1024, w_in.shape[-1])
    tm_mid = min(256, seq)
    tm_moe = min(256, t)
    tf_up = min(1024, f)
    tn_down = min(2048, d)
    tm_disp = min(256, t)
    tm_comb = min(256, t)
    tm_item = MOE_HALVES * tm_moe
    n_blocks = (t * TOP_K + n_e * (tm_item - 1)) // tm_item
    col = lambda width_off: width_off // HEAD_DIM

    rope_c, rope_sa, rope_sb = _rope_tables(seq)
    h = x.reshape(t, d)
    xn = _rmsnorm(h, norm_mix[0], tm_norm, BF16)
    proj = _inproj(xn, w_in[0], tm_in, tn_in)
    attn = _attention(proj, rope_c, rope_sa, rope_sb, batch, seq, n_heads,
                      col(3 * cw), col(3 * cw + aw), col(3 * cw + 2 * aw))
    merged = _merge(proj, attn, conv_w[0], w_conv_out[0].astype(BF16), w_attn_out[0].astype(BF16),
                    seq, cw, aw, d, tm_mid)
    h1, xn2, top_e, top_w, rank, counts = _oproj(merged, h, w_o[0].astype(BF16), norm_ffn[0],
                                                 w_router[0].T, b_router[0], tm_mid)
    blk_start, blk_count, half_count, base, fill = _routing_tables(counts[:, 0], tm_moe)
    dest = _dest_rows(base, top_e, rank, min(2048, t))
    xs = _dispatch(_tile_major(dest, tm_disp), fill, xn2, n_blocks * MOE_HALVES, tm_moe, tm_disp)
    up_items = _expert_items(blk_start, blk_count, half_count, f // tf_up, n_blocks)
    hid = _expert_up(up_items, xs, w_gate[0], b_gate[0], w_up[0], b_up[0], tm_item, tf_up)
    same_tiling = f // tf_up == d // tn_down
    down_items = (up_items if same_tiling else
                  _expert_items(blk_start, blk_count, half_count, d // tn_down, n_blocks))
    ys = _expert_down(down_items, hid, w_down[0], b_down[0], tm_item, tn_down)
    out = _combine(_tile_major(dest, tm_comb), h1, top_w.T, norm_final, ys, tm_comb)
    return out.reshape(batch, seq, d)
```
